```python
import math, functools
import jax, jax.numpy as jnp
from jax import lax
import numpy as np

D_MODEL = 1024
BATCH = 32
SEQ = 256
DEPTH = 2
DEC_BATCH = 4
DEC_SEQ = 2048
PAST_LEN = 256

GRID_W = 64
POOL_WIDTH = D_MODEL // 4
POOL_GROUPS = 4
POOL_GROUP_DIM = POOL_WIDTH // POOL_GROUPS
POOL_WINDOWS = (2, 4, 8, 16)
HY_WIDTH = D_MODEL // 4
HY_ORDER = 2
HY_DIRS = 2
HY_BANDS = 16
HY_EMB = 1 + 2 * HY_BANDS
HY_FFN = 64
HY_SHORT = 3
HY_TARGET = 1e-2
HY_FAST = 0.3
HY_SLOW = 1.5
NA_HEAD_DIM = 64
NA_WIDTH = D_MODEL // 2
NA_HEADS = NA_WIDTH // NA_HEAD_DIM
NA_MAX_KH = 8
NA_KW = 16
MIX_WIDTH = POOL_WIDTH + HY_WIDTH + NA_WIDTH
IN_WIDTH = POOL_WIDTH + (HY_ORDER + 1) * HY_WIDTH + 3 * NA_WIDTH
D_FF = 4 * D_MODEL
N_MOD = 6
NORM_EPS = 1e-6
Q_BLOCK = 128

kernel_name = "hybrid_pool_hyena_natten_diffusion_step"


def rms_norm(x, g):
    x32 = x.astype(jnp.float32)
    y = x32 * lax.rsqrt(jnp.mean(x32 * x32, axis=-1, keepdims=True) + NORM_EPS)
    return (y * g.astype(jnp.float32)).astype(x.dtype)


def adaln(cond, w_mod, b_mod):
    m = jax.nn.silu(cond) @ w_mod + b_mod
    return jnp.split(m, N_MOD, axis=-1)


def pool_mixer(u, pool_w, pool_scale):
    B, L, _ = u.shape
    u32 = u.astype(jnp.float32).reshape(B, L, POOL_GROUPS, POOL_GROUP_DIM)
    cs = jnp.concatenate([jnp.zeros_like(u32[:, :1]), jnp.cumsum(u32, axis=1)], axis=1)
    t = jnp.arange(L)
    outs = []
    for g, w in enumerate(POOL_WINDOWS):
        lo = jnp.clip(t - w // 2, 0, L - 1)
        hi = jnp.clip(t + (w - 1 - w // 2), 0, L - 1)
        cnt = (hi - lo + 1).astype(jnp.float32)[None, :, None]
        cs_g = cs[:, :, g]
        mean = (cs_g[:, hi + 1] - cs_g[:, lo]) / cnt
        outs.append(mean - u32[:, :, g])
    pooled = jnp.stack(outs, axis=2).astype(u.dtype)
    y = jnp.einsum('blgc,gcd->blgd', pooled, pool_w)
    return y.reshape(B, L, POOL_WIDTH) * pool_scale


def hyena_position_features(L):
    t = jnp.linspace(0.0, 1.0, L, dtype=jnp.float32)[:, None]
    w = 2.0 * math.pi * jnp.arange(L, dtype=jnp.float32)[:, None] / L
    f = jnp.linspace(1e-4, HY_BANDS - 1, HY_BANDS, dtype=jnp.float32)[None, :]
    z = jnp.concatenate([t, jnp.cos(f * w), -jnp.sin(f * w)], axis=-1)
    return z, t


def hyena_filters(L, f1_w, f1_b, f1_freq, f2_w, f2_b, f2_freq, f3_w):
    f32 = jnp.float32
    z, t = hyena_position_features(L)
    h = jnp.sin(f1_freq.astype(f32) * (z @ f1_w.astype(f32) + f1_b.astype(f32)))
    h = jnp.sin(f2_freq.astype(f32) * (h @ f2_w.astype(f32) + f2_b.astype(f32)))
    k = (h @ f3_w.astype(f32)).reshape(L, HY_DIRS, HY_ORDER, HY_WIDTH)
    deltas = jnp.abs(jnp.linspace(math.log(HY_TARGET) / HY_SLOW, math.log(HY_TARGET) / HY_FAST,
                                  HY_WIDTH, dtype=f32))
    decay = jnp.exp(-t * deltas[None, :])
    k = k * decay[:, None, None, :]
    return k / (jnp.sum(jnp.abs(k), axis=0, keepdims=True) + 1e-6)


def two_sided_fft_conv(z, kf, kb):
    L = z.shape[1]
    filt = jnp.concatenate([kf, jnp.zeros_like(kf[:1]), kb[:0:-1]], axis=0)
    zf = jnp.fft.rfft(z, n=2 * L, axis=1)
    ff = jnp.fft.rfft(filt, axis=0)
    return jnp.fft.irfft(zf * ff[None], n=2 * L, axis=1)[:, :L]


def short_conv(u, w, b):
    L = u.shape[1]
    pad = HY_SHORT // 2
    up = jnp.pad(u, ((0, 0), (pad, HY_SHORT - 1 - pad), (0, 0)))
    y = b
    for j in range(HY_SHORT):
        y = y + up[:, j:j + L] * w[j]
    return y


def hyena_mixer(u, conv_w, conv_b, f1_w, f1_b, f1_freq, f2_w, f2_b, f2_freq, f3_w, hy_bias):
    L = u.shape[1]
    u = short_conv(u, conv_w, conv_b)
    x1, x2, v = jnp.split(u, 3, axis=-1)
    k = hyena_filters(L, f1_w, f1_b, f1_freq, f2_w, f2_b, f2_freq, f3_w)
    z = v.astype(jnp.float32)
    for o, gate in enumerate((x1, x2)):
        conv = two_sided_fft_conv(z, k[:, 0, o], k[:, 1, o])
        z = gate.astype(jnp.float32) * (conv + hy_bias[o].astype(jnp.float32) * z)
    return z.astype(u.dtype)


def context_attention(q, k, v):
    B, L, H, Dh = q.shape
    nb = L // Q_BLOCK
    qb = q.reshape(B, nb, Q_BLOCK, H, Dh).transpose(1, 0, 2, 3, 4)
    scale = Dh ** -0.5

    def block(qi):
        s = jnp.einsum('bqhd,bkhd->bhqk', qi, k).astype(jnp.float32) * scale
        p = jax.nn.softmax(s, axis=-1).astype(v.dtype)
        return jnp.einsum('bhqk,bkhd->bqhd', p, v)

    o = lax.map(block, qb)
    return o.transpose(1, 0, 2, 3, 4).reshape(B, L, H * Dh)


def neighbourhood_attention(q, k, v, k_ctx, v_ctx, rel_bias):
    B, L, H, Dh = q.shape
    rows = L // GRID_W
    kh = min(NA_MAX_KH, rows)
    r = jnp.arange(rows)
    c = jnp.arange(GRID_W)
    row_start = jnp.clip(r - kh // 2, 0, rows - kh)
    row_idx = row_start[:, None] + jnp.arange(kh)[None, :]
    col_start = jnp.clip(c - NA_KW // 2, 0, GRID_W - NA_KW)
    col_mask = (c[None, :] >= col_start[:, None]) & (c[None, :] < col_start[:, None] + NA_KW)
    dr = row_idx - r[:, None] + NA_MAX_KH - 1
    dc = jnp.clip(c[None, :] - c[:, None], -(NA_KW - 1), NA_KW - 1) + NA_KW - 1
    bias = rel_bias[:, dr][:, :, :, dc]
    bias = bias.transpose(0, 1, 3, 2, 4).astype(jnp.float32)

    scale = Dh ** -0.5
    qg = q.reshape(B, rows, GRID_W, H, Dh)
    kg = k.reshape(B, rows, GRID_W, H, Dh)[:, row_idx]
    vg = v.reshape(B, rows, GRID_W, H, Dh)[:, row_idx]
    s_nb = jnp.einsum('brchd,brikhd->bhrcik', qg, kg).astype(jnp.float32) * scale + bias[None]
    s_nb = jnp.where(col_mask[:, None, :], s_nb, -jnp.inf)
    s_nb = s_nb.reshape(B, H, rows, GRID_W, kh * GRID_W)
    s_ctx = jnp.einsum('brchd,bkhd->bhrck', qg, k_ctx).astype(jnp.float32) * scale
    p = jax.nn.softmax(jnp.concatenate([s_nb, s_ctx], axis=-1), axis=-1).astype(v.dtype)
    p_nb = p[..., :kh * GRID_W].reshape(B, H, rows, GRID_W, kh, GRID_W)
    p_ctx = p[..., kh * GRID_W:]
    o = (jnp.einsum('bhrcik,brikhd->brchd', p_nb, vg)
         + jnp.einsum('bhrck,bkhd->brchd', p_ctx, v_ctx))
    return o.reshape(B, L, H * Dh)


def trunk_layer(x, cond, attend, norm1_g, norm2_g, w_mod, b_mod, w_in, pool_w, pool_scale,
                hy_conv_w, hy_conv_b, hy_f1_w, hy_f1_b, hy_f1_freq, hy_f2_w, hy_f2_b, hy_f2_freq,
                hy_f3_w, hy_bias, q_norm_g, k_norm_g, w_out, w_up, w_down):
    B, L, _ = x.shape
    shift1, scale1, gate1, shift2, scale2, gate2 = adaln(cond, w_mod, b_mod)
    h = rms_norm(x, norm1_g) * (1.0 + scale1) + shift1
    u = h @ w_in
    s1 = POOL_WIDTH
    s2 = s1 + (HY_ORDER + 1) * HY_WIDTH
    s3 = s2 + NA_WIDTH
    s4 = s3 + NA_WIDTH
    u_pool, u_hy, u_q, u_k, u_v = jnp.split(u, [s1, s2, s3, s4], axis=-1)
    y_pool = pool_mixer(u_pool, pool_w, pool_scale)
    y_hy = hyena_mixer(u_hy, hy_conv_w, hy_conv_b, hy_f1_w, hy_f1_b, hy_f1_freq,
                       hy_f2_w, hy_f2_b, hy_f2_freq, hy_f3_w, hy_bias)
    q = rms_norm(u_q.reshape(B, L, NA_HEADS, NA_HEAD_DIM), q_norm_g)
    k = rms_norm(u_k.reshape(B, L, NA_HEADS, NA_HEAD_DIM), k_norm_g)
    v = u_v.reshape(B, L, NA_HEADS, NA_HEAD_DIM)
    y_na = attend(q, k, v)
    y = jnp.concatenate([y_pool, y_hy, y_na], axis=-1) @ w_out
    x = x + gate1 * y
    h = rms_norm(x, norm2_g) * (1.0 + scale2) + shift2
    x = x + gate2 * (jnp.square(jax.nn.relu(h @ w_up)) @ w_down)
    return x, k, v


def setup_inputs(seed: int = 0) -> dict:
    key = jax.random.key(seed)
    ks = jax.random.split(key, 32)
    f32 = jnp.float32
    D = D_MODEL

    def nrm(k, shape, scale):
        return scale * jax.random.normal(k, shape, f32)

    return {
        "x_prompt": nrm(ks[0], (BATCH, SEQ, D), 1.0),
        "x_sample": nrm(ks[1], (DEC_BATCH, DEC_SEQ, D), 1.0),
        "cache_k": nrm(ks[2], (DEC_BATCH, DEPTH, PAST_LEN, NA_HEADS, NA_HEAD_DIM), 1.0),
        "cache_v": nrm(ks[3], (DEC_BATCH, DEPTH, PAST_LEN, NA_HEADS, NA_HEAD_DIM), 1.0),
        "c": nrm(ks[4], (DEC_BATCH, D), 1.0),
        "c_ctx": nrm(ks[5], (D,), 1.0),
        "norm1_g": 1.0 + nrm(ks[6], (DEPTH, D), 0.02),
        "norm2_g": 1.0 + nrm(ks[7], (DEPTH, D), 0.02),
        "w_mod": nrm(ks[8], (DEPTH, D, N_MOD * D), 0.5 * D ** -0.5),
        "b_mod": nrm(ks[9], (DEPTH, N_MOD * D), 0.01),
        "w_in": nrm(ks[10], (DEPTH, D, IN_WIDTH), D ** -0.5),
        "pool_w": nrm(ks[11], (DEPTH, POOL_GROUPS, POOL_GROUP_DIM, POOL_GROUP_DIM), POOL_GROUP_DIM ** -0.5),
        "pool_scale": 1.0 + nrm(ks[12], (DEPTH, POOL_WIDTH), 0.1),
        "hy_conv_w": nrm(ks[13], (DEPTH, HY_SHORT, (HY_ORDER + 1) * HY_WIDTH), HY_SHORT ** -0.5),
        "hy_conv_b": nrm(ks[14], (DEPTH, (HY_ORDER + 1) * HY_WIDTH), 0.01),
        "hy_f1_w": nrm(ks[15], (DEPTH, HY_EMB, HY_FFN), 2.0 * HY_EMB ** -0.5),
        "hy_f1_b": nrm(ks[16], (DEPTH, HY_FFN), 0.1),
        "hy_f1_freq": 1.0 + nrm(ks[17], (DEPTH, HY_FFN), 0.1),
        "hy_f2_w": nrm(ks[18], (DEPTH, HY_FFN, HY_FFN), 2.0 * HY_FFN ** -0.5),
        "hy_f2_b": nrm(ks[19], (DEPTH, HY_FFN), 0.1),
        "hy_f2_freq": 1.0 + nrm(ks[20], (DEPTH, HY_FFN), 0.1),
        "hy_f3_w": nrm(ks[21], (DEPTH, HY_FFN, HY_DIRS * HY_ORDER * HY_WIDTH), HY_FFN ** -0.5),
        "hy_bias": nrm(ks[22], (DEPTH, HY_ORDER, HY_WIDTH), 0.1),
        "q_norm_g": 1.0 + nrm(ks[23], (DEPTH, NA_HEAD_DIM), 0.02),
        "k_norm_g": 1.0 + nrm(ks[24], (DEPTH, NA_HEAD_DIM), 0.02),
        "rel_bias": nrm(ks[25], (DEPTH, NA_HEADS, 2 * NA_MAX_KH - 1, 2 * NA_KW - 1), 0.1),
        "w_out": nrm(ks[26], (DEPTH, MIX_WIDTH, D), MIX_WIDTH ** -0.5),
        "w_up": nrm(ks[27], (DEPTH, D, D_FF), D ** -0.5),
        "w_down": nrm(ks[28], (DEPTH, D_FF, D), D_FF ** -0.5),
    }


def reference(x_prompt, x_sample, cache_k, cache_v, c, c_ctx, norm1_g, norm2_g, w_mod, b_mod,
              w_in, pool_w, pool_scale, hy_conv_w, hy_conv_b, hy_f1_w, hy_f1_b, hy_f1_freq,
              hy_f2_w, hy_f2_b, hy_f2_freq, hy_f3_w, hy_bias, q_norm_g, k_norm_g, rel_bias,
              w_out, w_up, w_down):
    def layer_params(l):
        return (norm1_g[l], norm2_g[l], w_mod[l], b_mod[l], w_in[l], pool_w[l], pool_scale[l],
                hy_conv_w[l], hy_conv_b[l], hy_f1_w[l], hy_f1_b[l], hy_f1_freq[l],
                hy_f2_w[l], hy_f2_b[l], hy_f2_freq[l], hy_f3_w[l], hy_bias[l],
                q_norm_g[l], k_norm_g[l], w_out[l], w_up[l], w_down[l])

    cond_ctx = c_ctx[None, None, :]
    xp = x_prompt
    ks, vs = [], []
    for l in range(DEPTH):
        xp, k_l, v_l = trunk_layer(xp, cond_ctx, context_attention, *layer_params(l))
        ks.append(k_l)
        vs.append(v_l)
    new_k = jnp.stack(ks, axis=1)
    new_v = jnp.stack(vs, axis=1)

    cond_lat = c[:, None, :]
    xs = x_sample
    for l in range(DEPTH):
        attend = functools.partial(neighbourhood_attention, k_ctx=cache_k[:, l], v_ctx=cache_v[:, l],
                                   rel_bias=rel_bias[l])
        xs, _, _ = trunk_layer(xs, cond_lat, attend, *layer_params(l))

    return (xp, xs, new_k, new_v)
```

```python
import functools
import math

import numpy as np
import jax
import jax.numpy as jnp
from jax import lax
from jax.experimental import pallas as pl
from jax.experimental.pallas import tpu as pltpu

F32 = jnp.float32
BF16 = jnp.bfloat16

D_MODEL = 1024
DEPTH = 2
GRID_W = 64
POOL_WIDTH = 256
POOL_GROUPS = 4
POOL_GROUP_DIM = 64
HY_WIDTH = 256
HY_BANDS = 16
HY_EMB = 1 + 2 * HY_BANDS
HY_FFN = 64
HY_TARGET = 1e-2
HY_FAST = 0.3
HY_SLOW = 1.5
NA_HEAD_DIM = 64
NA_WIDTH = 512
NA_HEADS = 8
NA_KH = 8
NA_KW = 16
N_MOD = 6
D_FF = 4096
NORM_EPS = 1e-6
ATT_SCALE = NA_HEAD_DIM ** -0.5

C_POOL = 0
C_HY = POOL_WIDTH
C_Q = C_HY + 3 * HY_WIDTH
C_K = C_Q + NA_WIDTH
C_V = C_K + NA_WIDTH
IN_WIDTH = C_V + NA_WIDTH

DFT_BLOCK = 256
DFT_N = 2 * DFT_BLOCK
MASK_VALUE = -1e30
VMEM_LIMIT = 56 * 1024 * 1024
TOKEN_TILE = 512
MOD_ROWS = 8


def _params(*sem):
    return pltpu.CompilerParams(dimension_semantics=sem, vmem_limit_bytes=VMEM_LIMIT)


def _resident(shape):
    return pl.BlockSpec(shape, lambda *_: (0,) * len(shape), pipeline_mode=pl.Buffered(1))


def _dot(a, b):
    return jnp.dot(a, b, preferred_element_type=F32)


def _dot_nt(a, b):
    return lax.dot_general(a, b, (((1,), (1,)), ((), ())), preferred_element_type=F32)


def _dot_f32(a, b):
    return jnp.dot(a, b, preferred_element_type=F32, precision=lax.Precision.HIGHEST)


def _adaln_kernel(c_ref, w_ref, b_ref, o_ref):
    c = c_ref[...]
    s = c * (1.0 / (1.0 + jnp.exp(-c)))
    o_ref[0] = _dot(s.astype(BF16), w_ref[0].astype(BF16)) + b_ref[0]


def _adaln(conds, w_mod, b_mod):
    tn = 1536
    n = N_MOD * D_MODEL
    return pl.pallas_call(
        _adaln_kernel,
        grid=(DEPTH, n // tn),
        in_specs=[
            pl.BlockSpec((MOD_ROWS, D_MODEL), lambda l, j: (0, 0)),
            pl.BlockSpec((1, D_MODEL, tn), lambda l, j: (l, 0, j)),
            pl.BlockSpec((1, 1, tn), lambda l, j: (l, 0, j)),
        ],
        out_specs=pl.BlockSpec((1, MOD_ROWS, tn), lambda l, j: (l, 0, j)),
        out_shape=jax.ShapeDtypeStruct((DEPTH, MOD_ROWS, n), F32),
        compiler_params=_params("parallel", "parallel"),
        name="adaln",
    )(conds, w_mod, b_mod.reshape(DEPTH, 1, n))


def _modulated_norm(x, g, shift, scale):
    ms = jnp.mean(x * x, axis=-1, keepdims=True)
    return (x * lax.rsqrt(ms + NORM_EPS) * g) * (1.0 + scale) + shift


def _inproj_kernel(x_ref, mod_ref, g_ref, w_ref, hm_ref, qg_ref, kg_ref,
                   up_ref, uh_ref, q_ref, k_ref, v_ref):
    h = _modulated_norm(x_ref[0], g_ref[...], mod_ref[0, 0:1, :], mod_ref[0, 1:2, :]).astype(BF16)
    up_ref[0] = _dot(h, w_ref[:, C_POOL:C_HY])
    uh_ref[0] = _dot(h, w_ref[:, C_HY:C_Q])

    def head_norm(u, g):
        ms = _dot((u * u).astype(BF16), hm_ref[...])
        return u * lax.rsqrt(ms + NORM_EPS) * g

    q = head_norm(_dot(h, w_ref[:, C_Q:C_K]), qg_ref[...])
    q_ref[0] = (q * ATT_SCALE).astype(q_ref.dtype)
    k_ref[0] = head_norm(_dot(h, w_ref[:, C_K:C_V]), kg_ref[...]).astype(k_ref.dtype)
    v_ref[0] = _dot(h, w_ref[:, C_V:IN_WIDTH]).astype(v_ref.dtype)


def _in_proj(x, mod, g1, w_in, head_mean, qg, kg, kv_dtype):
    B, L, _ = x.shape
    tm = TOKEN_TILE
    tok = lambda w: pl.BlockSpec((1, tm, w), lambda b, i: (b, i, 0))
    return pl.pallas_call(
        _inproj_kernel,
        grid=(B, L // tm),
        in_specs=[
            tok(D_MODEL),
            pl.BlockSpec((1, N_MOD, D_MODEL), lambda b, i: (b, 0, 0)),
            _resident((1, D_MODEL)),
            _resident((D_MODEL, IN_WIDTH)),
            _resident((NA_WIDTH, NA_WIDTH)),
            _resident((1, NA_WIDTH)),
            _resident((1, NA_WIDTH)),
        ],
        out_specs=[tok(POOL_WIDTH), tok(3 * HY_WIDTH), tok(NA_WIDTH), tok(NA_WIDTH), tok(NA_WIDTH)],
        out_shape=[
            jax.ShapeDtypeStruct((B, L, POOL_WIDTH), F32),
            jax.ShapeDtypeStruct((B, L, 3 * HY_WIDTH), F32),
            jax.ShapeDtypeStruct((B, L, NA_WIDTH), BF16),
            jax.ShapeDtypeStruct((B, L, NA_WIDTH), kv_dtype),
            jax.ShapeDtypeStruct((B, L, NA_WIDTH), kv_dtype),
        ],
        compiler_params=_params("parallel", "parallel"),
        name="in_proj",
    )(x, mod, g1, w_in, head_mean, qg, kg)


POOL_PAD = 8


def _pool_kernel(u_ref, w_ref, s_ref, o_ref, pad_ref):
    L = u_ref.shape[1]
    n = L + 2 * POOL_PAD
    u = u_ref[0]
    pad_ref[0:POOL_PAD, :] = jnp.zeros((POOL_PAD, POOL_WIDTH), F32)
    pad_ref[POOL_PAD + L:n, :] = jnp.zeros((POOL_PAD, POOL_WIDTH), F32)
    pad_ref[POOL_PAD:POOL_PAD + L, :] = u
    a = pad_ref[...]

    def sh(v, s):
        return pltpu.roll(v, s % n, axis=0)

    p2 = a + sh(a, 1)
    p4 = sh(p2, 1) + sh(p2, -1)
    p8 = sh(p4, 2) + sh(p4, -2)
    p16 = sh(p8, 4) + sh(p8, -4)
    lane_p = lax.broadcasted_iota(jnp.int32, (n, POOL_WIDTH), 1)
    win = jnp.where(lane_p < 64, p2, jnp.where(lane_p < 128, p4, jnp.where(lane_p < 192, p8, p16)))
    win = win[POOL_PAD:POOL_PAD + L, :]

    t = lax.broadcasted_iota(jnp.int32, (L, POOL_WIDTH), 0)
    lane = lax.broadcasted_iota(jnp.int32, (L, POOL_WIDTH), 1)
    half = jnp.where(lane < 64, 1, jnp.where(lane < 128, 2, jnp.where(lane < 192, 4, 8)))
    lo = jnp.maximum(t - half, 0)
    hi = jnp.minimum(t + half - 1, L - 1)
    cnt = (hi - lo + 1).astype(F32)
    pooled = win / cnt - u
    o_ref[0] = (_dot(pooled.astype(BF16), w_ref[...]) * s_ref[...]).astype(o_ref.dtype)


def _pool_mixer(u_pool, w_bd, scale):
    B, L, _ = u_pool.shape
    return pl.pallas_call(
        _pool_kernel,
        grid=(B,),
        in_specs=[
            pl.BlockSpec((1, L, POOL_WIDTH), lambda b: (b, 0, 0)),
            _resident((POOL_WIDTH, POOL_WIDTH)),
            _resident((1, POOL_WIDTH)),
        ],
        out_specs=pl.BlockSpec((1, L, POOL_WIDTH), lambda b: (b, 0, 0)),
        out_shape=jax.ShapeDtypeStruct((B, L, POOL_WIDTH), BF16),
        scratch_shapes=[pltpu.VMEM((L + 2 * POOL_PAD, POOL_WIDTH), F32)],
        compiler_params=_params("parallel"),
        name="pool_mixer",
    )(u_pool, w_bd, scale)


@functools.lru_cache(maxsize=None)
def _hyena_position_consts(L):
    p = np.abs(np.arange(2 * L, dtype=np.float64) - L)
    p[0] = 0.0
    t = p / (L - 1)
    w = 2.0 * math.pi * p / L
    f = np.linspace(1e-4, HY_BANDS - 1, HY_BANDS)
    z = np.zeros((2 * L, 128), np.float64)
    z[:, 0] = t
    z[:, 1:1 + HY_BANDS] = np.cos(f[None, :] * w[:, None])
    z[:, 1 + HY_BANDS:HY_EMB] = -np.sin(f[None, :] * w[:, None])
    deltas = np.abs(np.linspace(math.log(HY_TARGET) / HY_SLOW, math.log(HY_TARGET) / HY_FAST, HY_WIDTH))
    decay = np.exp(-t[:, None] * deltas[None, :])
    return z.astype(np.float32), decay.astype(np.float32)


def _filter_kernel(z_ref, dec_ref, w1_ref, b1_ref, f1_ref, w2_ref, b2_ref, f2_ref, w3_ref, o_ref):
    L = z_ref.shape[0] // 2
    h = jnp.sin(f1_ref[...] * (_dot_f32(z_ref[...], w1_ref[...]) + b1_ref[...]))
    h = jnp.sin(f2_ref[...] * (_dot_f32(h, w2_ref[...]) + b2_ref[...]))
    for half, dirn in ((0, 1), (1, 0)):
        rows = slice(half * L, (half + 1) * L)
        k = _dot_f32(h[rows], w3_ref[:, dirn * 2 * HY_WIDTH:(dirn + 1) * 2 * HY_WIDTH])
        for o in range(2):
            ko = k[:, o * HY_WIDTH:(o + 1) * HY_WIDTH] * dec_ref[rows, :]
            ko = ko / (jnp.sum(jnp.abs(ko), axis=0, keepdims=True) + 1e-6)
            if half == 0:
                ko = jnp.where(lax.broadcasted_iota(jnp.int32, ko.shape, 0) == 0, 0.0, ko)
            o_ref[o, rows, :] = ko


def _hyena_filters(L, w1, b1, f1, w2, b2, f2, w3):
    z, decay = _hyena_position_consts(L)
    return pl.pallas_call(
        _filter_kernel,
        out_shape=jax.ShapeDtypeStruct((2, 2 * L, HY_WIDTH), F32),
        compiler_params=pltpu.CompilerParams(vmem_limit_bytes=VMEM_LIMIT),
        name="hyena_filters",
    )(jnp.asarray(z), jnp.asarray(decay), w1, b1, f1, w2, b2, f2, w3)


@functools.lru_cache(maxsize=None)
def _dft_consts():
    n = np.arange(DFT_BLOCK, dtype=np.float64)
    j = np.arange(DFT_BLOCK, dtype=np.float64)
    ang = 2.0 * math.pi * np.outer(j, n) / DFT_N
    fwd = np.concatenate([np.cos(ang), -np.sin(ang)], axis=0)
    fwd[DFT_BLOCK] = np.cos(math.pi * n)
    wgt = np.where(j == 0, 1.0, 2.0)[None, :]
    inv = np.concatenate([wgt * np.cos(ang.T), -2.0 * np.sin(ang.T)], axis=1) / DFT_N
    inv[:, DFT_BLOCK] = np.cos(math.pi * n) / DFT_N
    sign = np.tile(np.where(np.arange(DFT_BLOCK) % 2 == 0, 1.0, -1.0), 2)[:, None]
    col0 = fwd[:, 0:1].copy()
    return fwd, inv, sign.astype(np.float32), col0.astype(np.float32)


def _split3(a64):
    hi = _np_bf16(a64)
    r1 = a64 - hi.astype(np.float64)
    mid = _np_bf16(r1)
    r2 = r1 - mid.astype(np.float64)
    return hi, mid, _np_bf16(r2)


def _np_bf16(a64):
    return a64.astype(np.float32).astype(BF16)


def _spectra_kernel(a_ref, fh_ref, fm_ref, fl_ref, sign_ref, col0_ref, g_ref, ny_ref, prev_ref):
    e = pl.program_id(1)
    a = a_ref[0]
    a_hi = a.astype(BF16)
    r1 = a - a_hi.astype(F32)
    a_mid = r1.astype(BF16)
    a_lo = (r1 - a_mid.astype(F32)).astype(BF16)
    fa = (_dot(fh_ref[...], a_hi) + (_dot(fh_ref[...], a_mid) + _dot(fm_ref[...], a_hi))
          + (_dot(fh_ref[...], a_lo) + _dot(fm_ref[...], a_mid) + _dot(fl_ref[...], a_hi)))

    @pl.when(e > 0)
    def _():
        g = fa + sign_ref[...] * prev_ref[...]
        ny_ref[0, 0] = g[DFT_BLOCK:DFT_BLOCK + 1, :]
        row = lax.broadcasted_iota(jnp.int32, (DFT_N, HY_WIDTH), 0)
        g_ref[0, 0] = jnp.where(row == DFT_BLOCK, 0.0, g)

    prev_ref[...] = fa - col0_ref[...] * a[0:1, :]


def _hyena_spectra(gg):
    nb = gg.shape[1] // (2 * DFT_BLOCK)
    nd = 2 * nb - 1
    fwd, _, sign, col0 = _dft_consts()
    fh, fm, fl = _split3(fwd)
    return pl.pallas_call(
        _spectra_kernel,
        grid=(2, 2 * nb),
        in_specs=[
            pl.BlockSpec((1, DFT_BLOCK, HY_WIDTH), lambda o, e: (o, e, 0)),
            _resident((DFT_N, DFT_BLOCK)), _resident((DFT_N, DFT_BLOCK)), _resident((DFT_N, DFT_BLOCK)),
            _resident((DFT_N, 1)), _resident((DFT_N, 1)),
        ],
        out_specs=[
            pl.BlockSpec((1, 1, DFT_N, HY_WIDTH), lambda o, e: (o, jnp.maximum(e - 1, 0), 0, 0)),
            pl.BlockSpec((1, 1, 1, HY_WIDTH), lambda o, e: (o, jnp.maximum(e - 1, 0), 0, 0)),
        ],
        out_shape=[
            jax.ShapeDtypeStruct((2, nd, DFT_N, HY_WIDTH), F32),
            jax.ShapeDtypeStruct((2, nd, 1, HY_WIDTH), F32),
        ],
        scratch_shapes=[pltpu.VMEM((DFT_N, HY_WIDTH), F32)],
        compiler_params=_params("parallel", "arbitrary"),
        name="hyena_spectra",
    )(gg, fh, fm, fl, jnp.asarray(sign), jnp.asarray(col0))


def _hyena_kernel(u_ref, cw_ref, cb_ref, hb_ref, g_ref, ny_ref, fwd_ref, inv_ref, o_ref,
                  z_ref, gate_ref, zf_ref):
    L = u_ref.shape[1]
    nb = L // DFT_BLOCK
    o = pl.program_id(1)

    @pl.when(o == 0)
    def _():
        t = lax.broadcasted_iota(jnp.int32, (L, HY_WIDTH), 0)
        for part in range(3):
            cols = slice(part * HY_WIDTH, (part + 1) * HY_WIDTH)
            u = u_ref[0, :, cols]
            prev = jnp.where(t == 0, 0.0, pltpu.roll(u, 1, axis=0))
            nxt = jnp.where(t == L - 1, 0.0, pltpu.roll(u, L - 1, axis=0))
            y = cb_ref[:, cols] + prev * cw_ref[0:1, cols] + u * cw_ref[1:2, cols] + nxt * cw_ref[2:3, cols]
            if part < 2:
                gate_ref[part] = y
            else:
                z_ref[...] = y

    for j in range(nb):
        blk = slice(j * DFT_BLOCK, (j + 1) * DFT_BLOCK)
        zf_ref[j] = _dot(fwd_ref[...], z_ref[blk, :].astype(BF16))

    row0 = lax.broadcasted_iota(jnp.int32, (DFT_BLOCK, HY_WIDTH), 0) == 0
    bias = hb_ref[0]

    def out_block(i, carry):
        top = jnp.zeros((DFT_BLOCK, HY_WIDTH), F32)
        bot = jnp.zeros((DFT_BLOCK, HY_WIDTH), F32)
        ny = jnp.zeros((1, HY_WIDTH), F32)
        for j in range(nb):
            d = i - j + (nb - 1)
            gr = g_ref[0, d, 0:DFT_BLOCK, :]
            gi = g_ref[0, d, DFT_BLOCK:DFT_N, :]
            zr = zf_ref[j, 0:DFT_BLOCK, :]
            zi = zf_ref[j, DFT_BLOCK:DFT_N, :]
            top = top + (gr * zr - gi * zi)
            bot = bot + (gr * zi + gi * zr)
            ny = ny + ny_ref[0, d] * zf_ref[j, DFT_BLOCK:DFT_BLOCK + 1, :]
        bot = jnp.where(row0, ny, bot)
        conv = (_dot(inv_ref[:, 0:DFT_BLOCK], top.astype(BF16))
                + _dot(inv_ref[:, DFT_BLOCK:DFT_N], bot.astype(BF16)))
        start = pl.multiple_of(i * DFT_BLOCK, DFT_BLOCK)
        zi_old = z_ref[pl.ds(start, DFT_BLOCK), :]
        z_ref[pl.ds(start, DFT_BLOCK), :] = gate_ref[o, pl.ds(start, DFT_BLOCK), :] * (conv + bias * zi_old)
        return carry

    lax.fori_loop(0, nb, out_block, 0)

    @pl.when(o == 1)
    def _():
        o_ref[0] = z_ref[...].astype(o_ref.dtype)


def _hyena_mixer(u_hy, conv_w, conv_b, hy_bias, spec, nyq):
    B, L, _ = u_hy.shape
    nd = spec.shape[1]
    fwd, inv, _, _ = _dft_consts()
    return pl.pallas_call(
        _hyena_kernel,
        grid=(B, 2),
        in_specs=[
            pl.BlockSpec((1, L, 3 * HY_WIDTH), lambda b, o: (b, 0, 0)),
            _resident((3, 3 * HY_WIDTH)),
            _resident((1, 3 * HY_WIDTH)),
            pl.BlockSpec((1, 1, HY_WIDTH), lambda b, o: (o, 0, 0)),
            pl.BlockSpec((1, nd, DFT_N, HY_WIDTH), lambda b, o: (o, 0, 0, 0)),
            pl.BlockSpec((1, nd, 1, HY_WIDTH), lambda b, o: (o, 0, 0, 0)),
            _resident((DFT_N, DFT_BLOCK)),
            _resident((DFT_BLOCK, DFT_N)),
        ],
        out_specs=pl.BlockSpec((1, L, HY_WIDTH), lambda b, o: (b, 0, 0)),
        out_shape=jax.ShapeDtypeStruct((B, L, HY_WIDTH), BF16),
        scratch_shapes=[
            pltpu.VMEM((L, HY_WIDTH), F32),
            pltpu.VMEM((2, L, HY_WIDTH), F32),
            pltpu.VMEM((L // DFT_BLOCK, DFT_N, HY_WIDTH), F32),
        ],
        compiler_params=_params("parallel", "arbitrary"),
        name="hyena_mixer",
    )(u_hy, conv_w, conv_b, hy_bias.reshape(2, 1, HY_WIDTH), spec, nyq, _np_bf16(fwd), _np_bf16(inv))


def _softmax_pv(scores, values):
    m = functools.reduce(jnp.maximum, [jnp.max(s, axis=-1, keepdims=True) for s in scores])
    ps = [jnp.exp(s - m) for s in scores]
    denom = functools.reduce(jnp.add, [jnp.sum(p, axis=-1, keepdims=True) for p in ps])
    acc = functools.reduce(jnp.add, [_dot(p.astype(BF16), v) for p, v in zip(ps, values)])
    return acc / denom


def _ctx_attn_kernel(q_ref, k_ref, v_ref, o_ref):
    q = q_ref[0]
    k = k_ref[0].astype(BF16)
    v = v_ref[0].astype(BF16)
    outs = []
    for h in range(NA_HEADS):
        sl = slice(h * NA_HEAD_DIM, (h + 1) * NA_HEAD_DIM)
        outs.append(_softmax_pv([_dot_nt(q[:, sl], k[:, sl])], [v[:, sl]]))
    o_ref[0] = jnp.concatenate(outs, axis=-1).astype(o_ref.dtype)


def _context_attention(q, k, v):
    B, L, _ = q.shape
    spec = pl.BlockSpec((1, L, NA_WIDTH), lambda b: (b, 0, 0))
    return pl.pallas_call(
        _ctx_attn_kernel,
        grid=(B,),
        in_specs=[spec, spec, spec],
        out_specs=spec,
        out_shape=jax.ShapeDtypeStruct((B, L, NA_WIDTH), BF16),
        compiler_params=_params("parallel"),
        name="context_attention",
    )(q, k, v)


N_DR = 2 * NA_KH - 1
N_DC = 2 * NA_KW - 1
NB_KEYS = NA_KH * GRID_W


def _bias_kernel(rb_ref, o_ref):
    h = pl.program_id(0)
    qc = lax.broadcasted_iota(jnp.int32, (GRID_W, GRID_W), 0)
    kc = lax.broadcasted_iota(jnp.int32, (GRID_W, GRID_W), 1)
    dc = jnp.clip(kc - qc, -(NA_KW - 1), NA_KW - 1) + (NA_KW - 1)
    col_start = jnp.clip(qc - NA_KW // 2, 0, GRID_W - NA_KW)
    visible = (kc >= col_start) & (kc < col_start + NA_KW)
    blocks = []
    for dr in range(N_DR):
        acc = jnp.zeros((GRID_W, GRID_W), F32)
        for d in range(N_DC):
            acc = jnp.where(dc == d, rb_ref[h * N_DR + dr, d], acc)
        blocks.append(jnp.where(visible, acc, MASK_VALUE))
    for delta in range(NA_KH):
        o_ref[0, delta] = jnp.concatenate([blocks[i - delta + NA_KH - 1] for i in range(NA_KH)], axis=-1)


def _bias_table(rel_bias_l):
    return pl.pallas_call(
        _bias_kernel,
        grid=(NA_HEADS,),
        in_specs=[pl.BlockSpec(memory_space=pltpu.SMEM)],
        out_specs=pl.BlockSpec((1, NA_KH, GRID_W, NB_KEYS), lambda h: (h, 0, 0, 0)),
        out_shape=jax.ShapeDtypeStruct((NA_HEADS, NA_KH, GRID_W, NB_KEYS), F32),
        compiler_params=_params("parallel"),
        name="na_bias_table",
    )(rel_bias_l.reshape(NA_HEADS * N_DR, N_DC))


def _row_start(r, rows):
    return jnp.clip(r - NA_KH // 2, 0, rows - NA_KH)


def _na_kernel(q_ref, k_ref, v_ref, kc_ref, vc_ref, tab_ref, o_ref):
    rows = k_ref.shape[1] // GRID_W
    r = pl.program_id(1)
    start = pl.multiple_of(_row_start(r, rows) * GRID_W, GRID_W)
    q = q_ref[0]
    kw = k_ref[0, pl.ds(start, NB_KEYS), :]
    vw = v_ref[0, pl.ds(start, NB_KEYS), :]
    kc = kc_ref[0, 0].astype(BF16)
    vc = vc_ref[0, 0].astype(BF16)
    outs = []
    for h in range(NA_HEADS):
        sl = slice(h * NA_HEAD_DIM, (h + 1) * NA_HEAD_DIM)
        s_nb = _dot_nt(q[:, sl], kw[:, sl]) + tab_ref[h, 0]
        s_ctx = _dot_nt(q[:, sl], kc[:, sl])
        outs.append(_softmax_pv([s_nb, s_ctx], [vw[:, sl], vc[:, sl]]))
    o_ref[0] = jnp.concatenate(outs, axis=-1).astype(o_ref.dtype)


def _neighbourhood_attention(q, k, v, cache_k, cache_v, layer, table):
    B, L, _ = q.shape
    rows = L // GRID_W
    Lc = cache_k.shape[2]
    seq = pl.BlockSpec((1, L, NA_WIDTH), lambda b, r: (b, 0, 0))
    row = pl.BlockSpec((1, GRID_W, NA_WIDTH), lambda b, r: (b, r, 0))
    ctx = pl.BlockSpec((1, 1, Lc, NA_WIDTH), lambda b, r: (b, layer, 0, 0))
    tab = pl.BlockSpec((NA_HEADS, 1, GRID_W, NB_KEYS), lambda b, r: (0, r - _row_start(r, rows), 0, 0))
    return pl.pallas_call(
        _na_kernel,
        grid=(B, rows),
        in_specs=[row, seq, seq, ctx, ctx, tab],
        out_specs=row,
        out_shape=jax.ShapeDtypeStruct((B, L, NA_WIDTH), BF16),
        compiler_params=_params("parallel", "arbitrary"),
        name="neighbourhood_attention",
    )(q, k, v, cache_k, cache_v, table)


FF_CHUNK = 1024


def _out_mlp_kernel(x_ref, yp_ref, yh_ref, ya_ref, mod_ref, g_ref, wo_ref, wu_ref, wd_ref, o_ref):
    y = (_dot(yp_ref[0], wo_ref[0:POOL_WIDTH, :])
         + _dot(yh_ref[0], wo_ref[POOL_WIDTH:POOL_WIDTH + HY_WIDTH, :])
         + _dot(ya_ref[0], wo_ref[POOL_WIDTH + HY_WIDTH:D_MODEL, :]))
    x = x_ref[0] + mod_ref[0, 2:3, :] * y
    h = _modulated_norm(x, g_ref[...], mod_ref[0, 3:4, :], mod_ref[0, 4:5, :]).astype(BF16)
    acc = jnp.zeros(x.shape, F32)
    for c in range(D_FF // FF_CHUNK):
        cols = slice(c * FF_CHUNK, (c + 1) * FF_CHUNK)
        a = jnp.maximum(_dot(h, wu_ref[:, cols]), 0.0)
        acc = acc + _dot((a * a).astype(BF16), wd_ref[cols, :])
    o_ref[0] = x + mod_ref[0, 5:6, :] * acc


def _out_mlp(x, y_pool, y_hy, y_na, mod, g2, w_out, w_up, w_down):
    B, L, _ = x.shape
    tm = TOKEN_TILE
    tok = lambda w: pl.BlockSpec((1, tm, w), lambda b, i: (b, i, 0))
    return pl.pallas_call(
        _out_mlp_kernel,
        grid=(B, L // tm),
        in_specs=[
            tok(D_MODEL), tok(POOL_WIDTH), tok(HY_WIDTH), tok(NA_WIDTH),
            pl.BlockSpec((1, N_MOD, D_MODEL), lambda b, i: (b, 0, 0)),
            _resident((1, D_MODEL)),
            _resident((D_MODEL, D_MODEL)),
            _resident((D_MODEL, D_FF)),
            _resident((D_FF, D_MODEL)),
        ],
        out_specs=tok(D_MODEL),
        out_shape=jax.ShapeDtypeStruct((B, L, D_MODEL), F32),
        compiler_params=_params("parallel", "parallel"),
        name="out_mlp",
    )(x, y_pool, y_hy, y_na, mod, g2, w_out, w_up, w_down)


def _block_diag(blocks):
    g, c, d = blocks.shape
    out = jnp.zeros((g * c, g * d), blocks.dtype)
    for i in range(g):
        out = out.at[i * c:(i + 1) * c, i * d:(i + 1) * d].set(blocks[i])
    return out


def kernel(x_prompt, x_sample, cache_k, cache_v, c, c_ctx, norm1_g, norm2_g, w_mod, b_mod, w_in,
           pool_w, pool_scale, hy_conv_w, hy_conv_b, hy_f1_w, hy_f1_b, hy_f1_freq, hy_f2_w, hy_f2_b,
           hy_f2_freq, hy_f3_w, hy_bias, q_norm_g, k_norm_g, rel_bias, w_out, w_up, w_down):
    n_prompt, seq, _ = x_prompt.shape
    n_dec, dec_seq, _ = x_sample.shape

    conds = jnp.zeros((MOD_ROWS, D_MODEL), F32).at[0].set(c_ctx).at[1:1 + n_dec].set(c)
    mod = _adaln(conds, w_mod, b_mod).reshape(DEPTH, MOD_ROWS, N_MOD, D_MODEL)

    head_mean = _block_diag(jnp.full((NA_HEADS, NA_HEAD_DIM, NA_HEAD_DIM), 1.0 / NA_HEAD_DIM, BF16))
    cache_k = cache_k.reshape(n_dec, DEPTH, -1, NA_WIDTH)
    cache_v = cache_v.reshape(n_dec, DEPTH, -1, NA_WIDTH)

    xp = x_prompt.reshape(1, n_prompt * seq, D_MODEL)
    xs = x_sample
    new_k, new_v = [], []
    for l in range(DEPTH):
        g1 = norm1_g[l].reshape(1, D_MODEL)
        g2 = norm2_g[l].reshape(1, D_MODEL)
        w_in_l = w_in[l].astype(BF16)
        w_out_l = w_out[l].astype(BF16)
        w_up_l = w_up[l].astype(BF16)
        w_down_l = w_down[l].astype(BF16)
        pool_bd = _block_diag(pool_w[l]).astype(BF16)
        pool_s = pool_scale[l].reshape(1, POOL_WIDTH)
        qg = jnp.tile(q_norm_g[l], NA_HEADS).reshape(1, NA_WIDTH)
        kg = jnp.tile(k_norm_g[l], NA_HEADS).reshape(1, NA_WIDTH)
        f1_w = jnp.zeros((128, HY_FFN), F32).at[:HY_EMB].set(hy_f1_w[l])
        filt_args = (f1_w, hy_f1_b[l].reshape(1, HY_FFN), hy_f1_freq[l].reshape(1, HY_FFN),
                     hy_f2_w[l], hy_f2_b[l].reshape(1, HY_FFN), hy_f2_freq[l].reshape(1, HY_FFN), hy_f3_w[l])
        conv_b = hy_conv_b[l].reshape(1, 3 * HY_WIDTH)

        mod_ctx = mod[l, 0:1]
        u_pool, u_hy, q, k, v = _in_proj(xp, mod_ctx, g1, w_in_l, head_mean, qg, kg, F32)
        per_seq = lambda a: a.reshape(n_prompt, seq, a.shape[-1])
        new_k.append(per_seq(k))
        new_v.append(per_seq(v))
        spec, nyq = _hyena_spectra(_hyena_filters(seq, *filt_args))
        y_pool = _pool_mixer(per_seq(u_pool), pool_bd, pool_s)
        y_hy = _hyena_mixer(per_seq(u_hy), hy_conv_w[l], conv_b, hy_bias[l], spec, nyq)
        y_na = _context_attention(per_seq(q), per_seq(k), per_seq(v))
        flat = lambda a: a.reshape(1, n_prompt * seq, a.shape[-1])
        xp = _out_mlp(xp, flat(y_pool), flat(y_hy), flat(y_na), mod_ctx, g2, w_out_l, w_up_l, w_down_l)

        mod_lat = mod[l, 1:1 + n_dec]
        u_pool, u_hy, q, k, v = _in_proj(xs, mod_lat, g1, w_in_l, head_mean, qg, kg, BF16)
        spec, nyq = _hyena_spectra(_hyena_filters(dec_seq, *filt_args))
        y_pool = _pool_mixer(u_pool, pool_bd, pool_s)
        y_hy = _hyena_mixer(u_hy, hy_conv_w[l], conv_b, hy_bias[l], spec, nyq)
        y_na = _neighbourhood_attention(q, k, v, cache_k, cache_v, l, _bias_table(rel_bias[l]))
        xs = _out_mlp(xs, y_pool, y_hy, y_na, mod_lat, g2, w_out_l, w_up_l, w_down_l)

    shape_kv = (n_prompt, DEPTH, seq, NA_HEADS, NA_HEAD_DIM)
    return (xp.reshape(n_prompt, seq, D_MODEL), xs,
            jnp.stack(new_k, axis=1).reshape(shape_kv), jnp.stack(new_v, axis=1).reshape(shape_kv))
```

```python
import functools
import math

import numpy as np
import jax
import jax.numpy as jnp
from jax import lax
from jax.experimental import pallas as pl
from jax.experimental.pallas import tpu as pltpu

F32 = jnp.float32
BF16 = jnp.bfloat16

D_MODEL = 1024
DEPTH = 2
GRID_W = 64
POOL_WIDTH = 256
POOL_GROUPS = 4
POOL_GROUP_DIM = 64
HY_WIDTH = 256
HY_BANDS = 16
HY_EMB = 1 + 2 * HY_BANDS
HY_FFN = 64
HY_TARGET = 1e-2
HY_FAST = 0.3
HY_SLOW = 1.5
NA_HEAD_DIM = 64
NA_WIDTH = 512
NA_HEADS = 8
NA_KH = 8
NA_KW = 16
N_MOD = 6
D_FF = 4096
NORM_EPS = 1e-6
ATT_SCALE = NA_HEAD_DIM ** -0.5

C_POOL = 0
C_HY = POOL_WIDTH
C_Q = C_HY + 3 * HY_WIDTH
C_K = C_Q + NA_WIDTH
C_V = C_K + NA_WIDTH
IN_WIDTH = C_V + NA_WIDTH

DFT_BLOCK = 256
DFT_N = 2 * DFT_BLOCK
MASK_VALUE = -1e30
VMEM_LIMIT = 56 * 1024 * 1024
TOKEN_TILE = 512
MOD_ROWS = 8


def _params(*sem):
    return pltpu.CompilerParams(dimension_semantics=sem, vmem_limit_bytes=VMEM_LIMIT)


def _resident(shape):
    return pl.BlockSpec(shape, lambda *_: (0,) * len(shape), pipeline_mode=pl.Buffered(1))


def _dot(a, b):
    return jnp.dot(a, b, preferred_element_type=F32)


def _dot_nt(a, b):
    return lax.dot_general(a, b, (((1,), (1,)), ((), ())), preferred_element_type=F32)


def _dot_f32(a, b):
    return jnp.dot(a, b, preferred_element_type=F32, precision=lax.Precision.HIGHEST)


def _adaln_kernel(c_ref, w_ref, b_ref, o_ref):
    c = c_ref[...]
    s = c * (1.0 / (1.0 + jnp.exp(-c)))
    o_ref[0] = _dot(s.astype(BF16), w_ref[0].astype(BF16)) + b_ref[0]


def _adaln(conds, w_mod, b_mod):
    tn = 1536
    n = N_MOD * D_MODEL
    return pl.pallas_call(
        _adaln_kernel,
        grid=(DEPTH, n // tn),
        in_specs=[
            pl.BlockSpec((MOD_ROWS, D_MODEL), lambda l, j: (0, 0)),
            pl.BlockSpec((1, D_MODEL, tn), lambda l, j: (l, 0, j)),
            pl.BlockSpec((1, 1, tn), lambda l, j: (l, 0, j)),
        ],
        out_specs=pl.BlockSpec((1, MOD_ROWS, tn), lambda l, j: (l, 0, j)),
        out_shape=jax.ShapeDtypeStruct((DEPTH, MOD_ROWS, n), F32),
        compiler_params=_params("parallel", "parallel"),
        name="adaln",
    )(conds, w_mod, b_mod.reshape(DEPTH, 1, n))


def _modulated_norm(x, g, shift, scale):
    ms = jnp.mean(x * x, axis=-1, keepdims=True)
    return (x * lax.rsqrt(ms + NORM_EPS) * g) * (1.0 + scale) + shift


def _inproj_kernel(x_ref, mod_ref, g_ref, w_ref, hm_ref, qg_ref, kg_ref, *rest):
    up_ref, uh_ref, q_ref, k_ref, v_ref = rest[-5:]
    h = _modulated_norm(x_ref[0], g_ref[...], mod_ref[0, 0:1, :], mod_ref[0, 1:2, :]).astype(BF16)
    up_ref[0] = _dot(h, w_ref[:, C_POOL:C_HY])
    uh_ref[0] = _dot(h, w_ref[:, C_HY:C_Q])

    def head_norm(u, g):
        ms = _dot((u * u).astype(BF16), hm_ref[...])
        return u * lax.rsqrt(ms + NORM_EPS) * g

    q = head_norm(_dot(h, w_ref[:, C_Q:C_K]), qg_ref[...])
    q_ref[0] = (q * ATT_SCALE).astype(q_ref.dtype)
    k = head_norm(_dot(h, w_ref[:, C_K:C_V]), kg_ref[...])
    v = _dot(h, w_ref[:, C_V:IN_WIDTH])
    k_ref[...] = k.astype(k_ref.dtype).reshape(k_ref.shape)
    v_ref[...] = v.astype(v_ref.dtype).reshape(v_ref.shape)


def _in_proj(x, mod, g1, w_in, head_mean, qg, kg, cache=None):
    B, L, _ = x.shape
    tm = TOKEN_TILE
    tok = lambda w: pl.BlockSpec((1, tm, w), lambda b, i: (b, i, 0))
    in_specs = [
        tok(D_MODEL),
        pl.BlockSpec((1, N_MOD, D_MODEL), lambda b, i: (b, 0, 0)),
        _resident((1, D_MODEL)),
        _resident((D_MODEL, IN_WIDTH)),
        _resident((NA_WIDTH, NA_WIDTH)),
        _resident((1, NA_WIDTH)),
        _resident((1, NA_WIDTH)),
    ]
    args = [x, mod, g1, w_in, head_mean, qg, kg]
    aliases = {}
    if cache is None:
        kv_spec = tok(NA_WIDTH)
        kv_shape = jax.ShapeDtypeStruct((B, L, NA_WIDTH), BF16)
    else:
        layer, seq, k_cache, v_cache = cache
        assert B == 1 and tm % seq == 0
        per_tile = tm // seq
        kv_spec = pl.BlockSpec((per_tile, 1, seq, NA_WIDTH), lambda b, i: (i, layer, 0, 0))
        kv_shape = jax.ShapeDtypeStruct((L // seq, DEPTH, seq, NA_WIDTH), F32)
        if k_cache is not None:
            in_specs += [pl.BlockSpec(memory_space=pl.ANY)] * 2
            aliases = {len(args): 3, len(args) + 1: 4}
            args += [k_cache, v_cache]
    return pl.pallas_call(
        _inproj_kernel,
        grid=(B, L // tm),
        in_specs=in_specs,
        out_specs=[tok(POOL_WIDTH), tok(3 * HY_WIDTH), tok(NA_WIDTH), kv_spec, kv_spec],
        out_shape=[
            jax.ShapeDtypeStruct((B, L, POOL_WIDTH), F32),
            jax.ShapeDtypeStruct((B, L, 3 * HY_WIDTH), F32),
            jax.ShapeDtypeStruct((B, L, NA_WIDTH), BF16),
            kv_shape,
            kv_shape,
        ],
        input_output_aliases=aliases,
        compiler_params=_params("parallel", "parallel"),
        name="in_proj",
    )(*args)


POOL_PAD = 8


def _pool_kernel(u_ref, w_ref, s_ref, o_ref, pad_ref):
    L = u_ref.shape[1]
    n = L + 2 * POOL_PAD
    u = u_ref[0]
    pad_ref[0:POOL_PAD, :] = jnp.zeros((POOL_PAD, POOL_WIDTH), F32)
    pad_ref[POOL_PAD + L:n, :] = jnp.zeros((POOL_PAD, POOL_WIDTH), F32)
    pad_ref[POOL_PAD:POOL_PAD + L, :] = u
    a = pad_ref[...]

    def sh(v, s):
        return pltpu.roll(v, s % n, axis=0)

    p2 = a + sh(a, 1)
    p4 = sh(p2, 1) + sh(p2, -1)
    p8 = sh(p4, 2) + sh(p4, -2)
    p16 = sh(p8, 4) + sh(p8, -4)
    lane_p = lax.broadcasted_iota(jnp.int32, (n, POOL_WIDTH), 1)
    win = jnp.where(lane_p < 64, p2, jnp.where(lane_p < 128, p4, jnp.where(lane_p < 192, p8, p16)))
    win = win[POOL_PAD:POOL_PAD + L, :]

    t = lax.broadcasted_iota(jnp.int32, (L, POOL_WIDTH), 0)
    lane = lax.broadcasted_iota(jnp.int32, (L, POOL_WIDTH), 1)
    half = jnp.where(lane < 64, 1, jnp.where(lane < 128, 2, jnp.where(lane < 192, 4, 8)))
    lo = jnp.maximum(t - half, 0)
    hi = jnp.minimum(t + half - 1, L - 1)
    cnt = (hi - lo + 1).astype(F32)
    pooled = win / cnt - u
    o_ref[0] = (_dot(pooled.astype(BF16), w_ref[...]) * s_ref[...]).astype(o_ref.dtype)


def _pool_mixer(u_pool, w_bd, scale):
    B, L, _ = u_pool.shape
    return pl.pallas_call(
        _pool_kernel,
        grid=(B,),
        in_specs=[
            pl.BlockSpec((1, L, POOL_WIDTH), lambda b: (b, 0, 0)),
            _resident((POOL_WIDTH, POOL_WIDTH)),
            _resident((1, POOL_WIDTH)),
        ],
        out_specs=pl.BlockSpec((1, L, POOL_WIDTH), lambda b: (b, 0, 0)),
        out_shape=jax.ShapeDtypeStruct((B, L, POOL_WIDTH), BF16),
        scratch_shapes=[pltpu.VMEM((L + 2 * POOL_PAD, POOL_WIDTH), F32)],
        compiler_params=_params("parallel"),
        name="pool_mixer",
    )(u_pool, w_bd, scale)


@functools.lru_cache(maxsize=None)
def _hyena_position_consts(L):
    p = np.abs(np.arange(2 * L, dtype=np.float64) - L)
    p[0] = 0.0
    t = p / (L - 1)
    w = 2.0 * math.pi * p / L
    f = np.linspace(1e-4, HY_BANDS - 1, HY_BANDS)
    z = np.zeros((2 * L, 128), np.float64)
    z[:, 0] = t
    z[:, 1:1 + HY_BANDS] = np.cos(f[None, :] * w[:, None])
    z[:, 1 + HY_BANDS:HY_EMB] = -np.sin(f[None, :] * w[:, None])
    deltas = np.abs(np.linspace(math.log(HY_TARGET) / HY_SLOW, math.log(HY_TARGET) / HY_FAST, HY_WIDTH))
    decay = np.exp(-t[:, None] * deltas[None, :])
    return z.astype(np.float32), decay.astype(np.float32)


def _filter_kernel(z_ref, dec_ref, w1_ref, b1_ref, f1_ref, w2_ref, b2_ref, f2_ref, w3_ref, o_ref):
    L = z_ref.shape[0] // 2
    h = jnp.sin(f1_ref[...] * (_dot_f32(z_ref[...], w1_ref[...]) + b1_ref[...]))
    h = jnp.sin(f2_ref[...] * (_dot_f32(h, w2_ref[...]) + b2_ref[...]))
    for half, dirn in ((0, 1), (1, 0)):
        rows = slice(half * L, (half + 1) * L)
        k = _dot_f32(h[rows], w3_ref[:, dirn * 2 * HY_WIDTH:(dirn + 1) * 2 * HY_WIDTH])
        for o in range(2):
            ko = k[:, o * HY_WIDTH:(o + 1) * HY_WIDTH] * dec_ref[rows, :]
            ko = ko / (jnp.sum(jnp.abs(ko), axis=0, keepdims=True) + 1e-6)
            if half == 0:
                ko = jnp.where(lax.broadcasted_iota(jnp.int32, ko.shape, 0) == 0, 0.0, ko)
            o_ref[o, rows, :] = ko


def _hyena_filters(L, w1, b1, f1, w2, b2, f2, w3):
    z, decay = _hyena_position_consts(L)
    return pl.pallas_call(
        _filter_kernel,
        out_shape=jax.ShapeDtypeStruct((2, 2 * L, HY_WIDTH), F32),
        compiler_params=pltpu.CompilerParams(vmem_limit_bytes=VMEM_LIMIT),
        name="hyena_filters",
    )(jnp.asarray(z), jnp.asarray(decay), w1, b1, f1, w2, b2, f2, w3)


@functools.lru_cache(maxsize=None)
def _dft_consts():
    n = np.arange(DFT_BLOCK, dtype=np.float64)
    j = np.arange(DFT_BLOCK, dtype=np.float64)
    ang = 2.0 * math.pi * np.outer(j, n) / DFT_N
    fwd = np.concatenate([np.cos(ang), -np.sin(ang)], axis=0)
    fwd[DFT_BLOCK] = np.cos(math.pi * n)
    wgt = np.where(j == 0, 1.0, 2.0)[None, :]
    inv = np.concatenate([wgt * np.cos(ang.T), -2.0 * np.sin(ang.T)], axis=1) / DFT_N
    inv[:, DFT_BLOCK] = np.cos(math.pi * n) / DFT_N
    sign = np.tile(np.where(np.arange(DFT_BLOCK) % 2 == 0, 1.0, -1.0), 2)[:, None]
    col0 = fwd[:, 0:1].copy()
    return fwd, inv, sign.astype(np.float32), col0.astype(np.float32)


def _split3(a64):
    hi = _np_bf16(a64)
    r1 = a64 - hi.astype(np.float64)
    mid = _np_bf16(r1)
    r2 = r1 - mid.astype(np.float64)
    return hi, mid, _np_bf16(r2)


def _np_bf16(a64):
    return a64.astype(np.float32).astype(BF16)


def _spectra_kernel(a_ref, fh_ref, fm_ref, fl_ref, sign_ref, col0_ref, g_ref, ny_ref, prev_ref):
    e = pl.program_id(1)
    a = a_ref[0]
    a_hi = a.astype(BF16)
    r1 = a - a_hi.astype(F32)
    a_mid = r1.astype(BF16)
    a_lo = (r1 - a_mid.astype(F32)).astype(BF16)
    fa = (_dot(fh_ref[...], a_hi) + (_dot(fh_ref[...], a_mid) + _dot(fm_ref[...], a_hi))
          + (_dot(fh_ref[...], a_lo) + _dot(fm_ref[...], a_mid) + _dot(fl_ref[...], a_hi)))

    @pl.when(e > 0)
    def _():
        g = fa + sign_ref[...] * prev_ref[...]
        ny_ref[0, 0] = g[DFT_BLOCK:DFT_BLOCK + 1, :]
        row = lax.broadcasted_iota(jnp.int32, (DFT_N, HY_WIDTH), 0)
        g_ref[0, 0] = jnp.where(row == DFT_BLOCK, 0.0, g)

    prev_ref[...] = fa - col0_ref[...] * a[0:1, :]


def _hyena_spectra(gg):
    nb = gg.shape[1] // (2 * DFT_BLOCK)
    nd = 2 * nb - 1
    fwd, _, sign, col0 = _dft_consts()
    fh, fm, fl = _split3(fwd)
    return pl.pallas_call(
        _spectra_kernel,
        grid=(2, 2 * nb),
        in_specs=[
            pl.BlockSpec((1, DFT_BLOCK, HY_WIDTH), lambda o, e: (o, e, 0)),
            _resident((DFT_N, DFT_BLOCK)), _resident((DFT_N, DFT_BLOCK)), _resident((DFT_N, DFT_BLOCK)),
            _resident((DFT_N, 1)), _resident((DFT_N, 1)),
        ],
        out_specs=[
            pl.BlockSpec((1, 1, DFT_N, HY_WIDTH), lambda o, e: (o, jnp.maximum(e - 1, 0), 0, 0)),
            pl.BlockSpec((1, 1, 1, HY_WIDTH), lambda o, e: (o, jnp.maximum(e - 1, 0), 0, 0)),
        ],
        out_shape=[
            jax.ShapeDtypeStruct((2, nd, DFT_N, HY_WIDTH), F32),
            jax.ShapeDtypeStruct((2, nd, 1, HY_WIDTH), F32),
        ],
        scratch_shapes=[pltpu.VMEM((DFT_N, HY_WIDTH), F32)],
        compiler_params=_params("parallel", "arbitrary"),
        name="hyena_spectra",
    )(gg, fh, fm, fl, jnp.asarray(sign), jnp.asarray(col0))


def _hyena_kernel(u_ref, cw_ref, cb_ref, hb_ref, g_ref, ny_ref, fwd_ref, inv_ref, o_ref,
                  z_ref, gate_ref, zf_ref):
    L = u_ref.shape[1]
    nb = L // DFT_BLOCK
    o = pl.program_id(1)

    @pl.when(o == 0)
    def _():
        t = lax.broadcasted_iota(jnp.int32, (L, HY_WIDTH), 0)
        for part in range(3):
            cols = slice(part * HY_WIDTH, (part + 1) * HY_WIDTH)
            u = u_ref[0, :, cols]
            prev = jnp.where(t == 0, 0.0, pltpu.roll(u, 1, axis=0))
            nxt = jnp.where(t == L - 1, 0.0, pltpu.roll(u, L - 1, axis=0))
            y = cb_ref[:, cols] + prev * cw_ref[0:1, cols] + u * cw_ref[1:2, cols] + nxt * cw_ref[2:3, cols]
            if part < 2:
                gate_ref[part] = y
            else:
                z_ref[...] = y

    for j in range(nb):
        blk = slice(j * DFT_BLOCK, (j + 1) * DFT_BLOCK)
        zf_ref[j] = _dot(fwd_ref[...], z_ref[blk, :].astype(BF16))

    row0 = lax.broadcasted_iota(jnp.int32, (DFT_BLOCK, HY_WIDTH), 0) == 0
    bias = hb_ref[0]

    def out_block(i, carry):
        top = jnp.zeros((DFT_BLOCK, HY_WIDTH), F32)
        bot = jnp.zeros((DFT_BLOCK, HY_WIDTH), F32)
        ny = jnp.zeros((1, HY_WIDTH), F32)
        for j in range(nb):
            d = i - j + (nb - 1)
            gr = g_ref[0, d, 0:DFT_BLOCK, :]
            gi = g_ref[0, d, DFT_BLOCK:DFT_N, :]
            zr = zf_ref[j, 0:DFT_BLOCK, :]
            zi = zf_ref[j, DFT_BLOCK:DFT_N, :]
            top = top + (gr * zr - gi * zi)
            bot = bot + (gr * zi + gi * zr)
            ny = ny + ny_ref[0, d] * zf_ref[j, DFT_BLOCK:DFT_BLOCK + 1, :]
        bot = jnp.where(row0, ny, bot)
        conv = (_dot(inv_ref[:, 0:DFT_BLOCK], top.astype(BF16))
                + _dot(inv_ref[:, DFT_BLOCK:DFT_N], bot.astype(BF16)))
        start = pl.multiple_of(i * DFT_BLOCK, DFT_BLOCK)
        zi_old = z_ref[pl.ds(start, DFT_BLOCK), :]
        z_ref[pl.ds(start, DFT_BLOCK), :] = gate_ref[o, pl.ds(start, DFT_BLOCK), :] * (conv + bias * zi_old)
        return carry

    lax.fori_loop(0, nb, out_block, 0)

    @pl.when(o == 1)
    def _():
        o_ref[0] = z_ref[...].astype(o_ref.dtype)


def _hyena_mixer(u_hy, conv_w, conv_b, hy_bias, spec, nyq):
    B, L, _ = u_hy.shape
    nd = spec.shape[1]
    fwd, inv, _, _ = _dft_consts()
    return pl.pallas_call(
        _hyena_kernel,
        grid=(B, 2),
        in_specs=[
            pl.BlockSpec((1, L, 3 * HY_WIDTH), lambda b, o: (b, 0, 0)),
            _resident((3, 3 * HY_WIDTH)),
            _resident((1, 3 * HY_WIDTH)),
            pl.BlockSpec((1, 1, HY_WIDTH), lambda b, o: (o, 0, 0)),
            pl.BlockSpec((1, nd, DFT_N, HY_WIDTH), lambda b, o: (o, 0, 0, 0)),
            pl.BlockSpec((1, nd, 1, HY_WIDTH), lambda b, o: (o, 0, 0, 0)),
            _resident((DFT_N, DFT_BLOCK)),
            _resident((DFT_BLOCK, DFT_N)),
        ],
        out_specs=pl.BlockSpec((1, L, HY_WIDTH), lambda b, o: (b, 0, 0)),
        out_shape=jax.ShapeDtypeStruct((B, L, HY_WIDTH), BF16),
        scratch_shapes=[
            pltpu.VMEM((L, HY_WIDTH), F32),
            pltpu.VMEM((2, L, HY_WIDTH), F32),
            pltpu.VMEM((L // DFT_BLOCK, DFT_N, HY_WIDTH), F32),
        ],
        compiler_params=_params("parallel", "arbitrary"),
        name="hyena_mixer",
    )(u_hy, conv_w, conv_b, hy_bias.reshape(2, 1, HY_WIDTH), spec, nyq,
      jnp.asarray(fwd, F32).astype(BF16), jnp.asarray(inv, F32).astype(BF16))


def _softmax_pv(scores, values):
    m = functools.reduce(jnp.maximum, [jnp.max(s, axis=-1, keepdims=True) for s in scores])
    ps = [jnp.exp(s - m) for s in scores]
    denom = functools.reduce(jnp.add, [jnp.sum(p, axis=-1, keepdims=True) for p in ps])
    acc = functools.reduce(jnp.add, [_dot(p.astype(BF16), v) for p, v in zip(ps, values)])
    return acc / denom


HEAD_PAIR = 2 * NA_HEAD_DIM


def _paired_heads(q, keys, values, bias=None):
    lane = lax.broadcasted_iota(jnp.int32, (q.shape[0], HEAD_PAIR), 1)
    outs = []
    for p in range(NA_HEADS // 2):
        cols = slice(p * HEAD_PAIR, (p + 1) * HEAD_PAIR)
        qp = q[:, cols]
        ks = [k[:, cols] for k in keys]
        vs = [v[:, cols] for v in values]
        halves = []
        for e in range(2):
            own = (lane < NA_HEAD_DIM) if e == 0 else (lane >= NA_HEAD_DIM)
            qh = jnp.where(own, qp, jnp.zeros_like(qp))
            scores = [_dot_nt(qh, k) for k in ks]
            if bias is not None:
                scores[0] = scores[0] + bias(2 * p + e)
            halves.append(_softmax_pv(scores, vs))
        outs.append(jnp.where(lane < NA_HEAD_DIM, halves[0], halves[1]))
    return jnp.concatenate(outs, axis=-1)


def _ctx_attn_kernel(q_ref, k_ref, v_ref, o_ref):
    k = k_ref[0, 0].astype(BF16)
    v = v_ref[0, 0].astype(BF16)
    o_ref[0] = _paired_heads(q_ref[0], [k], [v]).astype(o_ref.dtype)


def _context_attention(q, k_cache, v_cache, layer):
    B, L, _ = q.shape
    spec = pl.BlockSpec((1, L, NA_WIDTH), lambda b: (b, 0, 0))
    kv = pl.BlockSpec((1, 1, L, NA_WIDTH), lambda b: (b, layer, 0, 0))
    return pl.pallas_call(
        _ctx_attn_kernel,
        grid=(B,),
        in_specs=[spec, kv, kv],
        out_specs=spec,
        out_shape=jax.ShapeDtypeStruct((B, L, NA_WIDTH), BF16),
        compiler_params=_params("parallel"),
        name="context_attention",
    )(q, k_cache, v_cache)


N_DR = 2 * NA_KH - 1
N_DC = 2 * NA_KW - 1
NA_RB = 4
NA_WIN = NA_RB + NA_KH
WIN_KEYS = NA_WIN * GRID_W


def _bias_kernel(rb_ref, o_ref):
    h = pl.program_id(0)
    qc = lax.broadcasted_iota(jnp.int32, (GRID_W, GRID_W), 0)
    kc = lax.broadcasted_iota(jnp.int32, (GRID_W, GRID_W), 1)
    dc = jnp.clip(kc - qc, -(NA_KW - 1), NA_KW - 1) + (NA_KW - 1)
    col_start = jnp.clip(qc - NA_KW // 2, 0, GRID_W - NA_KW)
    visible = (kc >= col_start) & (kc < col_start + NA_KW)
    blocks = []
    for dr in range(N_DR):
        acc = jnp.zeros((GRID_W, GRID_W), F32)
        for d in range(N_DC):
            acc = jnp.where(dc == d, rb_ref[h * N_DR + dr, d], acc)
        blocks.append(jnp.where(visible, acc, MASK_VALUE))
    hidden = jnp.full((GRID_W, GRID_W), MASK_VALUE, F32)
    for variant, (first, base) in enumerate(((0, NA_KH - 1), (None, NA_KH // 2 - 1), (NA_RB, -1))):
        for a in range(NA_RB):
            lo = a if first is None else first
            row = [blocks[i - a + base] if lo <= i < lo + NA_KH else hidden for i in range(NA_WIN)]
            o_ref[variant, 0, a * GRID_W:(a + 1) * GRID_W, :] = jnp.concatenate(row, axis=-1)


def _bias_table(rel_bias_l):
    return pl.pallas_call(
        _bias_kernel,
        grid=(NA_HEADS,),
        in_specs=[pl.BlockSpec(memory_space=pltpu.SMEM)],
        out_specs=pl.BlockSpec((3, 1, NA_RB * GRID_W, WIN_KEYS), lambda h: (0, h, 0, 0)),
        out_shape=jax.ShapeDtypeStruct((3, NA_HEADS, NA_RB * GRID_W, WIN_KEYS), F32),
        compiler_params=_params("parallel"),
        name="na_bias_table",
    )(rel_bias_l.reshape(NA_HEADS * N_DR, N_DC))


def _na_kernel(q_ref, k_ref, v_ref, kc_ref, vc_ref, tab_ref, o_ref):
    rows = k_ref.shape[1] // GRID_W
    rb = pl.program_id(1)
    first_row = jnp.clip(rb * NA_RB - NA_KH // 2, 0, rows - NA_WIN)
    start = pl.multiple_of(first_row * GRID_W, NA_RB * GRID_W)
    keys = [k_ref[0, pl.ds(start, WIN_KEYS), :], kc_ref[0, 0].astype(BF16)]
    values = [v_ref[0, pl.ds(start, WIN_KEYS), :], vc_ref[0, 0].astype(BF16)]
    o_ref[0] = _paired_heads(q_ref[0], keys, values, bias=lambda h: tab_ref[0, h]).astype(o_ref.dtype)


def _neighbourhood_attention(q, k, v, cache_k, cache_v, layer, table):
    B, L, _ = q.shape
    rows = L // GRID_W
    assert rows % NA_RB == 0 and rows >= NA_WIN + NA_RB
    nrb = rows // NA_RB
    Lc = cache_k.shape[2]
    seq = pl.BlockSpec((1, L, NA_WIDTH), lambda b, r: (b, 0, 0))
    blk = pl.BlockSpec((1, NA_RB * GRID_W, NA_WIDTH), lambda b, r: (b, r, 0))
    ctx = pl.BlockSpec((1, 1, Lc, NA_WIDTH), lambda b, r: (b, layer, 0, 0))
    placement = lambda r: jnp.where(r == 0, 0, jnp.where(r == nrb - 1, 2, 1))
    tab = pl.BlockSpec((1, NA_HEADS, NA_RB * GRID_W, WIN_KEYS), lambda b, r: (placement(r), 0, 0, 0))
    return pl.pallas_call(
        _na_kernel,
        grid=(B, nrb),
        in_specs=[blk, seq, seq, ctx, ctx, tab],
        out_specs=blk,
        out_shape=jax.ShapeDtypeStruct((B, L, NA_WIDTH), BF16),
        compiler_params=_params("parallel", "arbitrary"),
        name="neighbourhood_attention",
    )(q, k, v, cache_k, cache_v, table)


FF_CHUNK = 1024


def _out_mlp_kernel(x_ref, yp_ref, yh_ref, ya_ref, mod_ref, g_ref, wo_ref, wu_ref, wd_ref, o_ref):
    y = (_dot(yp_ref[0], wo_ref[0:POOL_WIDTH, :])
         + _dot(yh_ref[0], wo_ref[POOL_WIDTH:POOL_WIDTH + HY_WIDTH, :])
         + _dot(ya_ref[0], wo_ref[POOL_WIDTH + HY_WIDTH:D_MODEL, :]))
    x = x_ref[0] + mod_ref[0, 2:3, :] * y
    h = _modulated_norm(x, g_ref[...], mod_ref[0, 3:4, :], mod_ref[0, 4:5, :]).astype(BF16)
    acc = jnp.zeros(x.shape, F32)
    for c in range(D_FF // FF_CHUNK):
        cols = slice(c * FF_CHUNK, (c + 1) * FF_CHUNK)
        a = jnp.maximum(_dot(h, wu_ref[:, cols]), 0.0)
        acc = acc + _dot((a * a).astype(BF16), wd_ref[cols, :])
    o_ref[0] = x + mod_ref[0, 5:6, :] * acc


def _out_mlp(x, y_pool, y_hy, y_na, mod, g2, w_out, w_up, w_down):
    B, L, _ = x.shape
    tm = TOKEN_TILE
    tok = lambda w: pl.BlockSpec((1, tm, w), lambda b, i: (b, i, 0))
    return pl.pallas_call(
        _out_mlp_kernel,
        grid=(B, L // tm),
        in_specs=[
            tok(D_MODEL), tok(POOL_WIDTH), tok(HY_WIDTH), tok(NA_WIDTH),
            pl.BlockSpec((1, N_MOD, D_MODEL), lambda b, i: (b, 0, 0)),
            _resident((1, D_MODEL)),
            _resident((D_MODEL, D_MODEL)),
            _resident((D_MODEL, D_FF)),
            _resident((D_FF, D_MODEL)),
        ],
        out_specs=tok(D_MODEL),
        out_shape=jax.ShapeDtypeStruct((B, L, D_MODEL), F32),
        compiler_params=_params("parallel", "parallel"),
        name="out_mlp",
    )(x, y_pool, y_hy, y_na, mod, g2, w_out, w_up, w_down)


def _block_diag(blocks):
    g, c, d = blocks.shape
    out = jnp.zeros((g * c, g * d), blocks.dtype)
    for i in range(g):
        out = out.at[i * c:(i + 1) * c, i * d:(i + 1) * d].set(blocks[i])
    return out


def kernel(x_prompt, x_sample, cache_k, cache_v, c, c_ctx, norm1_g, norm2_g, w_mod, b_mod, w_in,
           pool_w, pool_scale, hy_conv_w, hy_conv_b, hy_f1_w, hy_f1_b, hy_f1_freq, hy_f2_w, hy_f2_b,
           hy_f2_freq, hy_f3_w, hy_bias, q_norm_g, k_norm_g, rel_bias, w_out, w_up, w_down):
    n_prompt, seq, _ = x_prompt.shape
    n_dec, dec_seq, _ = x_sample.shape

    conds = jnp.zeros((MOD_ROWS, D_MODEL), F32).at[0].set(c_ctx).at[1:1 + n_dec].set(c)
    mod = _adaln(conds, w_mod, b_mod).reshape(DEPTH, MOD_ROWS, N_MOD, D_MODEL)

    head_mean = _block_diag(jnp.full((NA_HEADS, NA_HEAD_DIM, NA_HEAD_DIM), 1.0 / NA_HEAD_DIM, BF16))
    cache_k = cache_k.reshape(n_dec, DEPTH, -1, NA_WIDTH)
    cache_v = cache_v.reshape(n_dec, DEPTH, -1, NA_WIDTH)

    xp = x_prompt.reshape(1, n_prompt * seq, D_MODEL)
    xs = x_sample
    new_k = new_v = None
    for l in range(DEPTH):
        g1 = norm1_g[l].reshape(1, D_MODEL)
        g2 = norm2_g[l].reshape(1, D_MODEL)
        w_in_l = w_in[l].astype(BF16)
        w_out_l = w_out[l].astype(BF16)
        w_up_l = w_up[l].astype(BF16)
        w_down_l = w_down[l].astype(BF16)
        pool_bd = _block_diag(pool_w[l]).astype(BF16)
        pool_s = pool_scale[l].reshape(1, POOL_WIDTH)
        qg = jnp.tile(q_norm_g[l], NA_HEADS).reshape(1, NA_WIDTH)
        kg = jnp.tile(k_norm_g[l], NA_HEADS).reshape(1, NA_WIDTH)
        f1_w = jnp.zeros((128, HY_FFN), F32).at[:HY_EMB].set(hy_f1_w[l])
        filt_args = (f1_w, hy_f1_b[l].reshape(1, HY_FFN), hy_f1_freq[l].reshape(1, HY_FFN),
                     hy_f2_w[l], hy_f2_b[l].reshape(1, HY_FFN), hy_f2_freq[l].reshape(1, HY_FFN), hy_f3_w[l])
        conv_b = hy_conv_b[l].reshape(1, 3 * HY_WIDTH)

        mod_ctx = mod[l, 0:1]
        u_pool, u_hy, q, new_k, new_v = _in_proj(xp, mod_ctx, g1, w_in_l, head_mean, qg, kg,
                                                 cache=(l, seq, new_k, new_v))
        per_seq = lambda a: a.reshape(n_prompt, seq, a.shape[-1])
        spec, nyq = _hyena_spectra(_hyena_filters(seq, *filt_args))
        y_pool = _pool_mixer(per_seq(u_pool), pool_bd, pool_s)
        y_hy = _hyena_mixer(per_seq(u_hy), hy_conv_w[l], conv_b, hy_bias[l], spec, nyq)
        y_na = _context_attention(per_seq(q), new_k, new_v, l)
        flat = lambda a: a.reshape(1, n_prompt * seq, a.shape[-1])
        xp = _out_mlp(xp, flat(y_pool), flat(y_hy), flat(y_na), mod_ctx, g2, w_out_l, w_up_l, w_down_l)

        mod_lat = mod[l, 1:1 + n_dec]
        u_pool, u_hy, q, k, v = _in_proj(xs, mod_lat, g1, w_in_l, head_mean, qg, kg)
        spec, nyq = _hyena_spectra(_hyena_filters(dec_seq, *filt_args))
        y_pool = _pool_mixer(u_pool, pool_bd, pool_s)
        y_hy = _hyena_mixer(u_hy, hy_conv_w[l], conv_b, hy_bias[l], spec, nyq)
        y_na = _neighbourhood_attention(q, k, v, cache_k, cache_v, l, _bias_table(rel_bias[l]))
        xs = _out_mlp(xs, y_pool, y_hy, y_na, mod_lat, g2, w_out_l, w_up_l, w_down_l)

    shape_kv = (n_prompt, DEPTH, seq, NA_HEADS, NA_HEAD_DIM)
    return (xp.reshape(n_prompt, seq, D_MODEL), xs, new_k.reshape(shape_kv), new_v.reshape(shape_kv))
```

```python
import functools
import math

import numpy as np
import jax
import jax.numpy as jnp
from jax import lax
from jax.experimental import pallas as pl
from jax.experimental.pallas import tpu as pltpu

F32 = jnp.float32
BF16 = jnp.bfloat16

D_MODEL = 1024
DEPTH = 2
GRID_W = 64
POOL_WIDTH = 256
POOL_GROUPS = 4
POOL_GROUP_DIM = 64
HY_WIDTH = 256
HY_BANDS = 16
HY_EMB = 1 + 2 * HY_BANDS
HY_FFN = 64
HY_TARGET = 1e-2
HY_FAST = 0.3
HY_SLOW = 1.5
NA_HEAD_DIM = 64
NA_WIDTH = 512
NA_HEADS = 8
NA_KH = 8
NA_KW = 16
N_MOD = 6
D_FF = 4096
NORM_EPS = 1e-6
ATT_SCALE = NA_HEAD_DIM ** -0.5

C_POOL = 0
C_HY = POOL_WIDTH
C_Q = C_HY + 3 * HY_WIDTH
C_K = C_Q + NA_WIDTH
C_V = C_K + NA_WIDTH
IN_WIDTH = C_V + NA_WIDTH

DFT_BLOCK = 256
DFT_N = 2 * DFT_BLOCK
MASK_VALUE = -1e30
VMEM_LIMIT = 56 * 1024 * 1024
TOKEN_TILE = 512
MOD_ROWS = 8
CTX_PER_STEP = 4


def _params(*sem):
    return pltpu.CompilerParams(dimension_semantics=sem, vmem_limit_bytes=VMEM_LIMIT)


def _resident(shape):
    return pl.BlockSpec(shape, lambda *_: (0,) * len(shape), pipeline_mode=pl.Buffered(1))


def _dot(a, b):
    return jnp.dot(a, b, preferred_element_type=F32)


def _dot_nt(a, b):
    return lax.dot_general(a, b, (((1,), (1,)), ((), ())), preferred_element_type=F32)


def _dot_f32(a, b):
    return jnp.dot(a, b, preferred_element_type=F32, precision=lax.Precision.HIGHEST)


def _adaln_kernel(c_ref, w_ref, b_ref, o_ref):
    c = c_ref[...]
    s = c * (1.0 / (1.0 + jnp.exp(-c)))
    o_ref[0] = _dot(s.astype(BF16), w_ref[0].astype(BF16)) + b_ref[0]


def _adaln(conds, w_mod, b_mod):
    tn = 1536
    n = N_MOD * D_MODEL
    return pl.pallas_call(
        _adaln_kernel,
        grid=(DEPTH, n // tn),
        in_specs=[
            pl.BlockSpec((MOD_ROWS, D_MODEL), lambda l, j: (0, 0)),
            pl.BlockSpec((1, D_MODEL, tn), lambda l, j: (l, 0, j)),
            pl.BlockSpec((1, 1, tn), lambda l, j: (l, 0, j)),
        ],
        out_specs=pl.BlockSpec((1, MOD_ROWS, tn), lambda l, j: (l, 0, j)),
        out_shape=jax.ShapeDtypeStruct((DEPTH, MOD_ROWS, n), F32),
        compiler_params=_params("parallel", "parallel"),
        name="adaln",
    )(conds, w_mod, b_mod.reshape(DEPTH, 1, n))


def _modulated_norm(x, g, shift, scale):
    ms = jnp.mean(x * x, axis=-1, keepdims=True)
    return (x * lax.rsqrt(ms + NORM_EPS) * g) * (1.0 + scale) + shift


def _inproj_kernel(x_ref, mod_ref, g_ref, w_ref, hm_ref, qg_ref, kg_ref, *rest):
    up_ref, uh_ref, q_ref, k_ref, v_ref = rest[-5:]
    h = _modulated_norm(x_ref[0], g_ref[...], mod_ref[0, 0:1, :], mod_ref[0, 1:2, :]).astype(BF16)
    up_ref[0] = _dot(h, w_ref[:, C_POOL:C_HY])
    uh_ref[0] = _dot(h, w_ref[:, C_HY:C_Q])

    def head_norm(u, g):
        ms = _dot((u * u).astype(BF16), hm_ref[...])
        return u * lax.rsqrt(ms + NORM_EPS) * g

    q = head_norm(_dot(h, w_ref[:, C_Q:C_K]), qg_ref[...])
    q_ref[0] = (q * ATT_SCALE).astype(q_ref.dtype)
    k = head_norm(_dot(h, w_ref[:, C_K:C_V]), kg_ref[...])
    v = _dot(h, w_ref[:, C_V:IN_WIDTH])
    k_ref[...] = k.astype(k_ref.dtype).reshape(k_ref.shape)
    v_ref[...] = v.astype(v_ref.dtype).reshape(v_ref.shape)


def _in_proj(x, mod, g1, w_in, head_mean, qg, kg, cache=None):
    B, L, _ = x.shape
    tm = TOKEN_TILE
    tok = lambda w: pl.BlockSpec((1, tm, w), lambda b, i: (b, i, 0))
    in_specs = [
        tok(D_MODEL),
        pl.BlockSpec((1, N_MOD, D_MODEL), lambda b, i: (b, 0, 0)),
        _resident((1, D_MODEL)),
        _resident((D_MODEL, IN_WIDTH)),
        _resident((NA_WIDTH, NA_WIDTH)),
        _resident((1, NA_WIDTH)),
        _resident((1, NA_WIDTH)),
    ]
    args = [x, mod, g1, w_in, head_mean, qg, kg]
    aliases = {}
    if cache is None:
        kv_spec = tok(NA_WIDTH)
        kv_shape = jax.ShapeDtypeStruct((B, L, NA_WIDTH), BF16)
    else:
        layer, seq, k_cache, v_cache = cache
        assert B == 1 and tm % seq == 0
        per_tile = tm // seq
        kv_spec = pl.BlockSpec((per_tile, 1, seq, NA_WIDTH), lambda b, i: (i, layer, 0, 0))
        kv_shape = jax.ShapeDtypeStruct((L // seq, DEPTH, seq, NA_WIDTH), F32)
        if k_cache is not None:
            in_specs += [pl.BlockSpec(memory_space=pl.ANY)] * 2
            aliases = {len(args): 3, len(args) + 1: 4}
            args += [k_cache, v_cache]
    return pl.pallas_call(
        _inproj_kernel,
        grid=(B, L // tm),
        in_specs=in_specs,
        out_specs=[tok(POOL_WIDTH), tok(3 * HY_WIDTH), tok(NA_WIDTH), kv_spec, kv_spec],
        out_shape=[
            jax.ShapeDtypeStruct((B, L, POOL_WIDTH), F32),
            jax.ShapeDtypeStruct((B, L, 3 * HY_WIDTH), F32),
            jax.ShapeDtypeStruct((B, L, NA_WIDTH), BF16),
            kv_shape,
            kv_shape,
        ],
        input_output_aliases=aliases,
        compiler_params=_params("parallel", "parallel"),
        name="in_proj",
    )(*args)


POOL_PAD = 8


def _pool_kernel(u_ref, w_ref, s_ref, o_ref, pad_ref):
    nseq, L = u_ref.shape[0], u_ref.shape[1]
    n = L + 2 * POOL_PAD

    def sh(v, s):
        return pltpu.roll(v, s % n, axis=0)

    lane_p = lax.broadcasted_iota(jnp.int32, (n, POOL_WIDTH), 1)
    t = lax.broadcasted_iota(jnp.int32, (L, POOL_WIDTH), 0)
    lane = lax.broadcasted_iota(jnp.int32, (L, POOL_WIDTH), 1)
    half = jnp.where(lane < 64, 1, jnp.where(lane < 128, 2, jnp.where(lane < 192, 4, 8)))
    lo = jnp.maximum(t - half, 0)
    hi = jnp.minimum(t + half - 1, L - 1)
    inv_cnt = 1.0 / (hi - lo + 1).astype(F32)

    for b in range(nseq):
        u = u_ref[b]
        pad_ref[b, 0:POOL_PAD, :] = jnp.zeros((POOL_PAD, POOL_WIDTH), F32)
        pad_ref[b, POOL_PAD + L:n, :] = jnp.zeros((POOL_PAD, POOL_WIDTH), F32)
        pad_ref[b, POOL_PAD:POOL_PAD + L, :] = u
        a = pad_ref[b]
        p2 = a + sh(a, 1)
        p4 = sh(p2, 1) + sh(p2, -1)
        p8 = sh(p4, 2) + sh(p4, -2)
        p16 = sh(p8, 4) + sh(p8, -4)
        win = jnp.where(lane_p < 64, p2, jnp.where(lane_p < 128, p4, jnp.where(lane_p < 192, p8, p16)))
        pooled = win[POOL_PAD:POOL_PAD + L, :] * inv_cnt - u
        o_ref[b] = (_dot(pooled.astype(BF16), w_ref[...]) * s_ref[...]).astype(o_ref.dtype)


def _pool_mixer(u_pool, w_bd, scale, per_step):
    B, L, _ = u_pool.shape
    blk = pl.BlockSpec((per_step, L, POOL_WIDTH), lambda b: (b, 0, 0))
    return pl.pallas_call(
        _pool_kernel,
        grid=(B // per_step,),
        in_specs=[blk, _resident((POOL_WIDTH, POOL_WIDTH)), _resident((1, POOL_WIDTH))],
        out_specs=blk,
        out_shape=jax.ShapeDtypeStruct((B, L, POOL_WIDTH), BF16),
        scratch_shapes=[pltpu.VMEM((per_step, L + 2 * POOL_PAD, POOL_WIDTH), F32)],
        compiler_params=_params("parallel"),
        name="pool_mixer",
    )(u_pool, w_bd, scale)


@functools.lru_cache(maxsize=None)
def _hyena_position_consts(L):
    p = np.abs(np.arange(2 * L, dtype=np.float64) - L)
    p[0] = 0.0
    t = p / (L - 1)
    w = 2.0 * math.pi * p / L
    f = np.linspace(1e-4, HY_BANDS - 1, HY_BANDS)
    z = np.zeros((2 * L, 128), np.float64)
    z[:, 0] = t
    z[:, 1:1 + HY_BANDS] = np.cos(f[None, :] * w[:, None])
    z[:, 1 + HY_BANDS:HY_EMB] = -np.sin(f[None, :] * w[:, None])
    deltas = np.abs(np.linspace(math.log(HY_TARGET) / HY_SLOW, math.log(HY_TARGET) / HY_FAST, HY_WIDTH))
    decay = np.exp(-t[:, None] * deltas[None, :])
    return z.astype(np.float32), decay.astype(np.float32)


def _filter_kernel(z_ref, dec_ref, w1_ref, b1_ref, f1_ref, w2_ref, b2_ref, f2_ref, w3_ref, o_ref):
    L = z_ref.shape[0] // 2
    h = jnp.sin(f1_ref[...] * (_dot_f32(z_ref[...], w1_ref[...]) + b1_ref[...]))
    h = jnp.sin(f2_ref[...] * (_dot_f32(h, w2_ref[...]) + b2_ref[...]))
    for half, dirn in ((0, 1), (1, 0)):
        rows = slice(half * L, (half + 1) * L)
        k = _dot_f32(h[rows], w3_ref[:, dirn * 2 * HY_WIDTH:(dirn + 1) * 2 * HY_WIDTH])
        for o in range(2):
            ko = k[:, o * HY_WIDTH:(o + 1) * HY_WIDTH] * dec_ref[rows, :]
            ko = ko / (jnp.sum(jnp.abs(ko), axis=0, keepdims=True) + 1e-6)
            if half == 0:
                ko = jnp.where(lax.broadcasted_iota(jnp.int32, ko.shape, 0) == 0, 0.0, ko)
            o_ref[o, rows, :] = ko


def _hyena_filters(L, w1, b1, f1, w2, b2, f2, w3):
    z, decay = _hyena_position_consts(L)
    return pl.pallas_call(
        _filter_kernel,
        out_shape=jax.ShapeDtypeStruct((2, 2 * L, HY_WIDTH), F32),
        compiler_params=pltpu.CompilerParams(vmem_limit_bytes=VMEM_LIMIT),
        name="hyena_filters",
    )(jnp.asarray(z), jnp.asarray(decay), w1, b1, f1, w2, b2, f2, w3)


@functools.lru_cache(maxsize=None)
def _dft_consts():
    n = np.arange(DFT_BLOCK, dtype=np.float64)
    j = np.arange(DFT_BLOCK, dtype=np.float64)
    ang = 2.0 * math.pi * np.outer(j, n) / DFT_N
    fwd = np.concatenate([np.cos(ang), -np.sin(ang)], axis=0)
    fwd[DFT_BLOCK] = np.cos(math.pi * n)
    wgt = np.where(j == 0, 1.0, 2.0)[None, :]
    inv = np.concatenate([wgt * np.cos(ang.T), -2.0 * np.sin(ang.T)], axis=1) / DFT_N
    inv[:, DFT_BLOCK] = np.cos(math.pi * n) / DFT_N
    sign = np.tile(np.where(np.arange(DFT_BLOCK) % 2 == 0, 1.0, -1.0), 2)[:, None]
    col0 = fwd[:, 0:1].copy()
    return fwd, inv, sign.astype(np.float32), col0.astype(np.float32)


def _split3(a64):
    hi = _np_bf16(a64)
    r1 = a64 - hi.astype(np.float64)
    mid = _np_bf16(r1)
    r2 = r1 - mid.astype(np.float64)
    return hi, mid, _np_bf16(r2)


def _np_bf16(a64):
    return a64.astype(np.float32).astype(BF16)


def _spectra_kernel(a_ref, fh_ref, fm_ref, fl_ref, sign_ref, col0_ref, g_ref, ny_ref, prev_ref):
    e = pl.program_id(1)
    a = a_ref[0]
    a_hi = a.astype(BF16)
    r1 = a - a_hi.astype(F32)
    a_mid = r1.astype(BF16)
    a_lo = (r1 - a_mid.astype(F32)).astype(BF16)
    fa = (_dot(fh_ref[...], a_hi) + (_dot(fh_ref[...], a_mid) + _dot(fm_ref[...], a_hi))
          + (_dot(fh_ref[...], a_lo) + _dot(fm_ref[...], a_mid) + _dot(fl_ref[...], a_hi)))

    @pl.when(e > 0)
    def _():
        g = fa + sign_ref[...] * prev_ref[...]
        ny_ref[0, 0] = g[DFT_BLOCK:DFT_BLOCK + 1, :]
        row = lax.broadcasted_iota(jnp.int32, (DFT_N, HY_WIDTH), 0)
        g_ref[0, 0] = jnp.where(row == DFT_BLOCK, 0.0, g)

    prev_ref[...] = fa - col0_ref[...] * a[0:1, :]


def _hyena_spectra(gg):
    nb = gg.shape[1] // (2 * DFT_BLOCK)
    nd = 2 * nb - 1
    fwd, _, sign, col0 = _dft_consts()
    fh, fm, fl = _split3(fwd)
    return pl.pallas_call(
        _spectra_kernel,
        grid=(2, 2 * nb),
        in_specs=[
            pl.BlockSpec((1, DFT_BLOCK, HY_WIDTH), lambda o, e: (o, e, 0)),
            _resident((DFT_N, DFT_BLOCK)), _resident((DFT_N, DFT_BLOCK)), _resident((DFT_N, DFT_BLOCK)),
            _resident((DFT_N, 1)), _resident((DFT_N, 1)),
        ],
        out_specs=[
            pl.BlockSpec((1, 1, DFT_N, HY_WIDTH), lambda o, e: (o, jnp.maximum(e - 1, 0), 0, 0)),
            pl.BlockSpec((1, 1, 1, HY_WIDTH), lambda o, e: (o, jnp.maximum(e - 1, 0), 0, 0)),
        ],
        out_shape=[
            jax.ShapeDtypeStruct((2, nd, DFT_N, HY_WIDTH), F32),
            jax.ShapeDtypeStruct((2, nd, 1, HY_WIDTH), F32),
        ],
        scratch_shapes=[pltpu.VMEM((DFT_N, HY_WIDTH), F32)],
        compiler_params=_params("parallel", "arbitrary"),
        name="hyena_spectra",
    )(gg, fh, fm, fl, jnp.asarray(sign), jnp.asarray(col0))


SPEC_ROWS = 32


def _hyena_kernel(u_ref, cw_ref, cb_ref, hb_ref, g_ref, ny_ref, fwd_ref, inv_ref, o_ref,
                  z_ref, gate_ref, zf_ref, y_ref):
    nseq, L = u_ref.shape[0], u_ref.shape[1]
    nb = L // DFT_BLOCK
    t = lax.broadcasted_iota(jnp.int32, (L, HY_WIDTH), 0)
    row0 = lax.broadcasted_iota(jnp.int32, (SPEC_ROWS, HY_WIDTH), 0) == 0

    for b in range(nseq):
        for part in range(3):
            cols = slice(part * HY_WIDTH, (part + 1) * HY_WIDTH)
            u = u_ref[b, :, cols]
            prev = jnp.where(t == 0, 0.0, pltpu.roll(u, 1, axis=0))
            nxt = jnp.where(t == L - 1, 0.0, pltpu.roll(u, L - 1, axis=0))
            y = cb_ref[:, cols] + prev * cw_ref[0:1, cols] + u * cw_ref[1:2, cols] + nxt * cw_ref[2:3, cols]
            if part < 2:
                gate_ref[part] = y
            else:
                z_ref[...] = y

        for o in range(2):
            for j in range(nb):
                blk = slice(j * DFT_BLOCK, (j + 1) * DFT_BLOCK)
                zf_ref[j] = _dot(fwd_ref[...], z_ref[blk, :].astype(BF16))
            bias = hb_ref[o]

            def out_block(i, carry, o=o, bias=bias):
                for r0 in range(0, DFT_BLOCK, SPEC_ROWS):
                    re = slice(r0, r0 + SPEC_ROWS)
                    im = slice(DFT_BLOCK + r0, DFT_BLOCK + r0 + SPEC_ROWS)
                    top = jnp.zeros((SPEC_ROWS, HY_WIDTH), F32)
                    bot = jnp.zeros((SPEC_ROWS, HY_WIDTH), F32)
                    ny = jnp.zeros((1, HY_WIDTH), F32)
                    for j in range(nb):
                        d = i - j + (nb - 1)
                        gr, gi = g_ref[o, d, re, :], g_ref[o, d, im, :]
                        zr, zi = zf_ref[j, re, :], zf_ref[j, im, :]
                        top = top + (gr * zr - gi * zi)
                        bot = bot + (gr * zi + gi * zr)
                        if r0 == 0:
                            ny = ny + ny_ref[o, d] * zf_ref[j, DFT_BLOCK:DFT_BLOCK + 1, :]
                    if r0 == 0:
                        bot = jnp.where(row0, ny, bot)
                    y_ref[re, :] = top.astype(BF16)
                    y_ref[im, :] = bot.astype(BF16)
                conv = _dot(inv_ref[...], y_ref[...])
                rows = pl.ds(pl.multiple_of(i * DFT_BLOCK, DFT_BLOCK), DFT_BLOCK)
                z_ref[rows, :] = gate_ref[o, rows, :] * (conv + bias * z_ref[rows, :])
                return carry

            lax.fori_loop(0, nb, out_block, 0)
        o_ref[b] = z_ref[...].astype(o_ref.dtype)


def _hyena_mixer(u_hy, conv_w, conv_b, hy_bias, spec, nyq, per_step):
    B, L, _ = u_hy.shape
    nd = spec.shape[1]
    fwd, inv, _, _ = _dft_consts()
    return pl.pallas_call(
        _hyena_kernel,
        grid=(B // per_step,),
        in_specs=[
            pl.BlockSpec((per_step, L, 3 * HY_WIDTH), lambda b: (b, 0, 0)),
            _resident((3, 3 * HY_WIDTH)),
            _resident((1, 3 * HY_WIDTH)),
            _resident((2, 1, HY_WIDTH)),
            _resident((2, nd, DFT_N, HY_WIDTH)),
            _resident((2, nd, 1, HY_WIDTH)),
            _resident((DFT_N, DFT_BLOCK)),
            _resident((DFT_BLOCK, DFT_N)),
        ],
        out_specs=pl.BlockSpec((per_step, L, HY_WIDTH), lambda b: (b, 0, 0)),
        out_shape=jax.ShapeDtypeStruct((B, L, HY_WIDTH), BF16),
        scratch_shapes=[
            pltpu.VMEM((L, HY_WIDTH), F32),
            pltpu.VMEM((2, L, HY_WIDTH), F32),
            pltpu.VMEM((L // DFT_BLOCK, DFT_N, HY_WIDTH), F32),
            pltpu.VMEM((DFT_N, HY_WIDTH), BF16),
        ],
        compiler_params=_params("parallel"),
        name="hyena_mixer",
    )(u_hy, conv_w, conv_b, hy_bias.reshape(2, 1, HY_WIDTH), spec, nyq,
      jnp.asarray(fwd, F32).astype(BF16), jnp.asarray(inv, F32).astype(BF16))


HEAD_PAIR = 2 * NA_HEAD_DIM
SOFTMAX_ROWS = 32


def _softmax_pv(q, keys, values, bias, s_ref, p_ref):
    off = 0
    for i, k in enumerate(keys):
        s = _dot_nt(q, k)
        s_ref[:, off:off + k.shape[0]] = s if (bias is None or i > 0) else s + bias
        off += k.shape[0]
    for r0 in range(0, q.shape[0], SOFTMAX_ROWS):
        rows = slice(r0, r0 + SOFTMAX_ROWS)
        s = s_ref[rows, :]
        p = jnp.exp(s - jnp.max(s, axis=-1, keepdims=True))
        p_ref[rows, :] = (p * (1.0 / jnp.sum(p, axis=-1, keepdims=True))).astype(BF16)
    acc, off = None, 0
    for v in values:
        part = _dot(p_ref[:, off:off + v.shape[0]], v)
        acc = part if acc is None else acc + part
        off += v.shape[0]
    return acc


def _paired_heads(q, keys, values, bias, s_ref, p_ref):
    lane = lax.broadcasted_iota(jnp.int32, (q.shape[0], HEAD_PAIR), 1)
    outs = []
    for p in range(NA_HEADS // 2):
        cols = slice(p * HEAD_PAIR, (p + 1) * HEAD_PAIR)
        qp = q[:, cols]
        ks = [k[:, cols] for k in keys]
        vs = [v[:, cols] for v in values]
        halves = []
        for e in range(2):
            own = (lane < NA_HEAD_DIM) if e == 0 else (lane >= NA_HEAD_DIM)
            qh = jnp.where(own, qp, jnp.zeros_like(qp))
            b = None if bias is None else bias(2 * p + e)
            halves.append(_softmax_pv(qh, ks, vs, b, s_ref.at[e], p_ref.at[e]))
        outs.append(jnp.where(lane < NA_HEAD_DIM, halves[0], halves[1]))
    return jnp.concatenate(outs, axis=-1)


def _ctx_attn_kernel(q_ref, k_ref, v_ref, o_ref, s_ref, p_ref):
    for b in range(q_ref.shape[0]):
        k = k_ref[b, 0].astype(BF16)
        v = v_ref[b, 0].astype(BF16)
        o_ref[b] = _paired_heads(q_ref[b], [k], [v], None, s_ref, p_ref).astype(o_ref.dtype)


def _context_attention(q, k_cache, v_cache, layer, per_step):
    B, L, _ = q.shape
    spec = pl.BlockSpec((per_step, L, NA_WIDTH), lambda b: (b, 0, 0))
    kv = pl.BlockSpec((per_step, 1, L, NA_WIDTH), lambda b: (b, layer, 0, 0))
    return pl.pallas_call(
        _ctx_attn_kernel,
        grid=(B // per_step,),
        in_specs=[spec, kv, kv],
        out_specs=spec,
        out_shape=jax.ShapeDtypeStruct((B, L, NA_WIDTH), BF16),
        scratch_shapes=[pltpu.VMEM((2, L, L), F32), pltpu.VMEM((2, L, L), BF16)],
        compiler_params=_params("parallel"),
        name="context_attention",
    )(q, k_cache, v_cache)


N_DR = 2 * NA_KH - 1
N_DC = 2 * NA_KW - 1
NA_RB = 4
NA_WIN = NA_RB + NA_KH
WIN_KEYS = NA_WIN * GRID_W


def _bias_kernel(rb_ref, o_ref):
    h = pl.program_id(0)
    qc = lax.broadcasted_iota(jnp.int32, (GRID_W, GRID_W), 0)
    kc = lax.broadcasted_iota(jnp.int32, (GRID_W, GRID_W), 1)
    dc = jnp.clip(kc - qc, -(NA_KW - 1), NA_KW - 1) + (NA_KW - 1)
    col_start = jnp.clip(qc - NA_KW // 2, 0, GRID_W - NA_KW)
    visible = (kc >= col_start) & (kc < col_start + NA_KW)
    blocks = []
    for dr in range(N_DR):
        acc = jnp.zeros((GRID_W, GRID_W), F32)
        for d in range(N_DC):
            acc = jnp.where(dc == d, rb_ref[h * N_DR + dr, d], acc)
        blocks.append(jnp.where(visible, acc, MASK_VALUE))
    hidden = jnp.full((GRID_W, GRID_W), MASK_VALUE, F32)
    for variant, (first, base) in enumerate(((0, NA_KH - 1), (None, NA_KH // 2 - 1), (NA_RB, -1))):
        for a in range(NA_RB):
            lo = a if first is None else first
            row = [blocks[i - a + base] if lo <= i < lo + NA_KH else hidden for i in range(NA_WIN)]
            o_ref[variant, 0, a * GRID_W:(a + 1) * GRID_W, :] = jnp.concatenate(row, axis=-1)


def _bias_table(rel_bias_l):
    return pl.pallas_call(
        _bias_kernel,
        grid=(NA_HEADS,),
        in_specs=[pl.BlockSpec(memory_space=pltpu.SMEM)],
        out_specs=pl.BlockSpec((3, 1, NA_RB * GRID_W, WIN_KEYS), lambda h: (0, h, 0, 0)),
        out_shape=jax.ShapeDtypeStruct((3, NA_HEADS, NA_RB * GRID_W, WIN_KEYS), F32),
        compiler_params=_params("parallel"),
        name="na_bias_table",
    )(rel_bias_l.reshape(NA_HEADS * N_DR, N_DC))


def _na_kernel(q_ref, k_ref, v_ref, kc_ref, vc_ref, tab_ref, o_ref, s_ref, p_ref):
    rows = k_ref.shape[1] // GRID_W
    rb = pl.program_id(1)
    first_row = jnp.clip(rb * NA_RB - NA_KH // 2, 0, rows - NA_WIN)
    start = pl.multiple_of(first_row * GRID_W, NA_RB * GRID_W)
    keys = [k_ref[0, pl.ds(start, WIN_KEYS), :], kc_ref[0, 0].astype(BF16)]
    values = [v_ref[0, pl.ds(start, WIN_KEYS), :], vc_ref[0, 0].astype(BF16)]
    out = _paired_heads(q_ref[0], keys, values, lambda h: tab_ref[0, h], s_ref, p_ref)
    o_ref[0] = out.astype(o_ref.dtype)


def _neighbourhood_attention(q, k, v, cache_k, cache_v, layer, table):
    B, L, _ = q.shape
    rows = L // GRID_W
    assert rows % NA_RB == 0 and rows >= NA_WIN + NA_RB
    nrb = rows // NA_RB
    Lc = cache_k.shape[2]
    seq = pl.BlockSpec((1, L, NA_WIDTH), lambda b, r: (b, 0, 0))
    blk = pl.BlockSpec((1, NA_RB * GRID_W, NA_WIDTH), lambda b, r: (b, r, 0))
    ctx = pl.BlockSpec((1, 1, Lc, NA_WIDTH), lambda b, r: (b, layer, 0, 0))
    placement = lambda r: jnp.where(r == 0, 0, jnp.where(r == nrb - 1, 2, 1))
    tab = pl.BlockSpec((1, NA_HEADS, NA_RB * GRID_W, WIN_KEYS), lambda b, r: (placement(r), 0, 0, 0))
    return pl.pallas_call(
        _na_kernel,
        grid=(B, nrb),
        in_specs=[blk, seq, seq, ctx, ctx, tab],
        out_specs=blk,
        out_shape=jax.ShapeDtypeStruct((B, L, NA_WIDTH), BF16),
        scratch_shapes=[pltpu.VMEM((2, NA_RB * GRID_W, WIN_KEYS + Lc), F32),
                        pltpu.VMEM((2, NA_RB * GRID_W, WIN_KEYS + Lc), BF16)],
        compiler_params=_params("parallel", "arbitrary"),
        name="neighbourhood_attention",
    )(q, k, v, cache_k, cache_v, table)


FF_CHUNK = 1024


def _out_mlp_kernel(x_ref, yp_ref, yh_ref, ya_ref, mod_ref, g_ref, wo_ref, wu_ref, wd_ref, o_ref):
    y = (_dot(yp_ref[0], wo_ref[0:POOL_WIDTH, :])
         + _dot(yh_ref[0], wo_ref[POOL_WIDTH:POOL_WIDTH + HY_WIDTH, :])
         + _dot(ya_ref[0], wo_ref[POOL_WIDTH + HY_WIDTH:D_MODEL, :]))
    x = x_ref[0] + mod_ref[0, 2:3, :] * y
    h = _modulated_norm(x, g_ref[...], mod_ref[0, 3:4, :], mod_ref[0, 4:5, :]).astype(BF16)
    acc = jnp.zeros(x.shape, F32)
    for c in range(D_FF // FF_CHUNK):
        cols = slice(c * FF_CHUNK, (c + 1) * FF_CHUNK)
        a = jnp.maximum(_dot(h, wu_ref[:, cols]), 0.0)
        acc = acc + _dot((a * a).astype(BF16), wd_ref[cols, :])
    o_ref[0] = x + mod_ref[0, 5:6, :] * acc


def _out_mlp(x, y_pool, y_hy, y_na, mod, g2, w_out, w_up, w_down):
    B, L, _ = x.shape
    tm = TOKEN_TILE
    tok = lambda w: pl.BlockSpec((1, tm, w), lambda b, i: (b, i, 0))
    return pl.pallas_call(
        _out_mlp_kernel,
        grid=(B, L // tm),
        in_specs=[
            tok(D_MODEL), tok(POOL_WIDTH), tok(HY_WIDTH), tok(NA_WIDTH),
            pl.BlockSpec((1, N_MOD, D_MODEL), lambda b, i: (b, 0, 0)),
            _resident((1, D_MODEL)),
            _resident((D_MODEL, D_MODEL)),
            _resident((D_MODEL, D_FF)),
            _resident((D_FF, D_MODEL)),
        ],
        out_specs=tok(D_MODEL),
        out_shape=jax.ShapeDtypeStruct((B, L, D_MODEL), F32),
        compiler_params=_params("parallel", "parallel"),
        name="out_mlp",
    )(x, y_pool, y_hy, y_na, mod, g2, w_out, w_up, w_down)


def _block_diag(blocks):
    g, c, d = blocks.shape
    out = jnp.zeros((g * c, g * d), blocks.dtype)
    for i in range(g):
        out = out.at[i * c:(i + 1) * c, i * d:(i + 1) * d].set(blocks[i])
    return out


def kernel(x_prompt, x_sample, cache_k, cache_v, c, c_ctx, norm1_g, norm2_g, w_mod, b_mod, w_in,
           pool_w, pool_scale, hy_conv_w, hy_conv_b, hy_f1_w, hy_f1_b, hy_f1_freq, hy_f2_w, hy_f2_b,
           hy_f2_freq, hy_f3_w, hy_bias, q_norm_g, k_norm_g, rel_bias, w_out, w_up, w_down):
    n_prompt, seq, _ = x_prompt.shape
    n_dec, dec_seq, _ = x_sample.shape

    conds = jnp.zeros((MOD_ROWS, D_MODEL), F32).at[0].set(c_ctx).at[1:1 + n_dec].set(c)
    mod = _adaln(conds, w_mod, b_mod).reshape(DEPTH, MOD_ROWS, N_MOD, D_MODEL)

    head_mean = _block_diag(jnp.full((NA_HEADS, NA_HEAD_DIM, NA_HEAD_DIM), 1.0 / NA_HEAD_DIM, BF16))
    cache_k = cache_k.reshape(n_dec, DEPTH, -1, NA_WIDTH)
    cache_v = cache_v.reshape(n_dec, DEPTH, -1, NA_WIDTH)

    xp = x_prompt.reshape(1, n_prompt * seq, D_MODEL)
    xs = x_sample
    new_k = new_v = None
    for l in range(DEPTH):
        g1 = norm1_g[l].reshape(1, D_MODEL)
        g2 = norm2_g[l].reshape(1, D_MODEL)
        w_in_l = w_in[l].astype(BF16)
        w_out_l = w_out[l].astype(BF16)
        w_up_l = w_up[l].astype(BF16)
        w_down_l = w_down[l].astype(BF16)
        pool_bd = _block_diag(pool_w[l]).astype(BF16)
        pool_s = pool_scale[l].reshape(1, POOL_WIDTH)
        qg = jnp.tile(q_norm_g[l], NA_HEADS).reshape(1, NA_WIDTH)
        kg = jnp.tile(k_norm_g[l], NA_HEADS).reshape(1, NA_WIDTH)
        f1_w = jnp.zeros((128, HY_FFN), F32).at[:HY_EMB].set(hy_f1_w[l])
        filt_args = (f1_w, hy_f1_b[l].reshape(1, HY_FFN), hy_f1_freq[l].reshape(1, HY_FFN),
                     hy_f2_w[l], hy_f2_b[l].reshape(1, HY_FFN), hy_f2_freq[l].reshape(1, HY_FFN), hy_f3_w[l])
        conv_b = hy_conv_b[l].reshape(1, 3 * HY_WIDTH)

        mod_ctx = mod[l, 0:1]
        u_pool, u_hy, q, new_k, new_v = _in_proj(xp, mod_ctx, g1, w_in_l, head_mean, qg, kg,
                                                 cache=(l, seq, new_k, new_v))
        per_seq = lambda a: a.reshape(n_prompt, seq, a.shape[-1])
        spec, nyq = _hyena_spectra(_hyena_filters(seq, *filt_args))
        y_pool = _pool_mixer(per_seq(u_pool), pool_bd, pool_s, CTX_PER_STEP)
        y_hy = _hyena_mixer(per_seq(u_hy), hy_conv_w[l], conv_b, hy_bias[l], spec, nyq, CTX_PER_STEP)
        y_na = _context_attention(per_seq(q), new_k, new_v, l, CTX_PER_STEP)
        flat = lambda a: a.reshape(1, n_prompt * seq, a.shape[-1])
        xp = _out_mlp(xp, flat(y_pool), flat(y_hy), flat(y_na), mod_ctx, g2, w_out_l, w_up_l, w_down_l)

        mod_lat = mod[l, 1:1 + n_dec]
        u_pool, u_hy, q, k, v = _in_proj(xs, mod_lat, g1, w_in_l, head_mean, qg, kg)
        spec, nyq = _hyena_spectra(_hyena_filters(dec_seq, *filt_args))
        y_pool = _pool_mixer(u_pool, pool_bd, pool_s, 1)
        y_hy = _hyena_mixer(u_hy, hy_conv_w[l], conv_b, hy_bias[l], spec, nyq, 1)
        y_na = _neighbourhood_attention(q, k, v, cache_k, cache_v, l, _bias_table(rel_bias[l]))
        xs = _out_mlp(xs, y_pool, y_hy, y_na, mod_lat, g2, w_out_l, w_up_l, w_down_l)

    shape_kv = (n_prompt, DEPTH, seq, NA_HEADS, NA_HEAD_DIM)
    return (xp.reshape(n_prompt, seq, D_MODEL), xs, new_k.reshape(shape_kv), new_v.reshape(shape_kv))
```

```python
import functools
import math

import numpy as np
import jax
import jax.numpy as jnp
from jax import lax
from jax.experimental import pallas as pl
from jax.experimental.pallas import tpu as pltpu

F32 = jnp.float32
BF16 = jnp.bfloat16

D_MODEL = 1024
DEPTH = 2
GRID_W = 64
POOL_WIDTH = 256
POOL_GROUPS = 4
POOL_GROUP_DIM = 64
HY_WIDTH = 256
HY_BANDS = 16
HY_EMB = 1 + 2 * HY_BANDS
HY_EMB_PAD = 128
HY_FFN = 64
HY_TARGET = 1e-2
HY_FAST = 0.3
HY_SLOW = 1.5
NA_HEAD_DIM = 64
NA_WIDTH = 512
NA_HEADS = 8
NA_KH = 8
NA_KW = 16
N_MOD = 6
D_FF = 4096
NORM_EPS = 1e-6
ATT_SCALE = NA_HEAD_DIM ** -0.5

C_POOL = 0
C_HY = POOL_WIDTH
C_Q = C_HY + 3 * HY_WIDTH
C_K = C_Q + NA_WIDTH
C_V = C_K + NA_WIDTH
IN_WIDTH = C_V + NA_WIDTH

DFT_BLOCK = 256
DFT_N = 2 * DFT_BLOCK
MASK_VALUE = -1e30
VMEM_LIMIT = 56 * 1024 * 1024
TOKEN_TILE = 512
MOD_ROWS = 8
CTX_PER_STEP = 4


def _params(*sem):
    return pltpu.CompilerParams(dimension_semantics=sem, vmem_limit_bytes=VMEM_LIMIT)


def _resident(shape):
    return pl.BlockSpec(shape, lambda *_: (0,) * len(shape), pipeline_mode=pl.Buffered(1))


def _dot(a, b):
    return jnp.dot(a, b, preferred_element_type=F32)


def _dot_nt(a, b):
    return lax.dot_general(a, b, (((1,), (1,)), ((), ())), preferred_element_type=F32)


def _dot_f32(a, b):
    return jnp.dot(a, b, preferred_element_type=F32, precision=lax.Precision.HIGHEST)


def _adaln_kernel(c_ref, w_ref, b_ref, o_ref):
    c = c_ref[...]
    s = c * (1.0 / (1.0 + jnp.exp(-c)))
    o_ref[0] = _dot(s.astype(BF16), w_ref[0].astype(BF16)) + b_ref[0]


def _adaln(conds, w_mod, b_mod):
    tn = 1536
    n = N_MOD * D_MODEL
    return pl.pallas_call(
        _adaln_kernel,
        grid=(DEPTH, n // tn),
        in_specs=[
            pl.BlockSpec((MOD_ROWS, D_MODEL), lambda l, j: (0, 0)),
            pl.BlockSpec((1, D_MODEL, tn), lambda l, j: (l, 0, j)),
            pl.BlockSpec((1, 1, tn), lambda l, j: (l, 0, j)),
        ],
        out_specs=pl.BlockSpec((1, MOD_ROWS, tn), lambda l, j: (l, 0, j)),
        out_shape=jax.ShapeDtypeStruct((DEPTH, MOD_ROWS, n), F32),
        compiler_params=_params("parallel", "parallel"),
        name="adaln",
    )(conds, w_mod, b_mod.reshape(DEPTH, 1, n))


def _modulated_norm(x, g, shift, scale):
    ms = jnp.mean(x * x, axis=-1, keepdims=True)
    return (x * lax.rsqrt(ms + NORM_EPS) * g) * (1.0 + scale) + shift


def _inproj_kernel(x_ref, mod_ref, g_ref, w_ref, hm_ref, qg_ref, kg_ref, *rest):
    up_ref, uh_ref, q_ref, k_ref, v_ref = rest[-5:]
    h = _modulated_norm(x_ref[0], g_ref[...], mod_ref[0, 0:1, :], mod_ref[0, 1:2, :]).astype(BF16)
    up_ref[0] = _dot(h, w_ref[:, C_POOL:C_HY])
    uh_ref[0] = _dot(h, w_ref[:, C_HY:C_Q])

    def head_norm(u, g):
        ms = _dot((u * u).astype(BF16), hm_ref[...])
        return u * lax.rsqrt(ms + NORM_EPS) * g

    q = head_norm(_dot(h, w_ref[:, C_Q:C_K]), qg_ref[...])
    q_ref[0] = (q * ATT_SCALE).astype(q_ref.dtype)
    k = head_norm(_dot(h, w_ref[:, C_K:C_V]), kg_ref[...])
    v = _dot(h, w_ref[:, C_V:IN_WIDTH])
    k_ref[...] = k.astype(k_ref.dtype).reshape(k_ref.shape)
    v_ref[...] = v.astype(v_ref.dtype).reshape(v_ref.shape)


def _in_proj(x, mod, g1, w_in, head_mean, qg, kg, cache=None):
    B, L, _ = x.shape
    tm = TOKEN_TILE
    tok = lambda w: pl.BlockSpec((1, tm, w), lambda b, i: (b, i, 0))
    in_specs = [
        tok(D_MODEL),
        pl.BlockSpec((1, N_MOD, D_MODEL), lambda b, i: (b, 0, 0)),
        _resident((1, D_MODEL)),
        _resident((D_MODEL, IN_WIDTH)),
        _resident((NA_WIDTH, NA_WIDTH)),
        _resident((1, NA_WIDTH)),
        _resident((1, NA_WIDTH)),
    ]
    args = [x, mod, g1, w_in, head_mean, qg, kg]
    aliases = {}
    if cache is None:
        kv_spec = tok(NA_WIDTH)
        kv_shape = jax.ShapeDtypeStruct((B, L, NA_WIDTH), BF16)
    else:
        layer, seq, k_cache, v_cache = cache
        assert B == 1 and tm % seq == 0
        per_tile = tm // seq
        kv_spec = pl.BlockSpec((per_tile, 1, seq, NA_WIDTH), lambda b, i: (i, layer, 0, 0))
        kv_shape = jax.ShapeDtypeStruct((L // seq, DEPTH, seq, NA_WIDTH), F32)
        if k_cache is not None:
            in_specs += [pl.BlockSpec(memory_space=pl.ANY)] * 2
            aliases = {len(args): 3, len(args) + 1: 4}
            args += [k_cache, v_cache]
    return pl.pallas_call(
        _inproj_kernel,
        grid=(B, L // tm),
        in_specs=in_specs,
        out_specs=[tok(POOL_WIDTH), tok(3 * HY_WIDTH), tok(NA_WIDTH), kv_spec, kv_spec],
        out_shape=[
            jax.ShapeDtypeStruct((B, L, POOL_WIDTH), F32),
            jax.ShapeDtypeStruct((B, L, 3 * HY_WIDTH), F32),
            jax.ShapeDtypeStruct((B, L, NA_WIDTH), BF16),
            kv_shape,
            kv_shape,
        ],
        input_output_aliases=aliases,
        compiler_params=_params("parallel", "parallel"),
        name="in_proj",
    )(*args)


POOL_PAD = 8


def _pool_kernel(u_ref, w_ref, s_ref, o_ref, pad_ref):
    nseq, L = u_ref.shape[0], u_ref.shape[1]
    n = L + 2 * POOL_PAD

    def sh(v, s):
        return pltpu.roll(v, s % n, axis=0)

    lane_p = lax.broadcasted_iota(jnp.int32, (n, POOL_WIDTH), 1)
    t = lax.broadcasted_iota(jnp.int32, (L, POOL_WIDTH), 0)
    lane = lax.broadcasted_iota(jnp.int32, (L, POOL_WIDTH), 1)
    half = jnp.where(lane < 64, 1, jnp.where(lane < 128, 2, jnp.where(lane < 192, 4, 8)))
    lo = jnp.maximum(t - half, 0)
    hi = jnp.minimum(t + half - 1, L - 1)
    inv_cnt = 1.0 / (hi - lo + 1).astype(F32)

    for b in range(nseq):
        u = u_ref[b]
        pad_ref[b, 0:POOL_PAD, :] = jnp.zeros((POOL_PAD, POOL_WIDTH), F32)
        pad_ref[b, POOL_PAD + L:n, :] = jnp.zeros((POOL_PAD, POOL_WIDTH), F32)
        pad_ref[b, POOL_PAD:POOL_PAD + L, :] = u
        a = pad_ref[b]
        p2 = a + sh(a, 1)
        p4 = sh(p2, 1) + sh(p2, -1)
        p8 = sh(p4, 2) + sh(p4, -2)
        p16 = sh(p8, 4) + sh(p8, -4)
        win = jnp.where(lane_p < 64, p2, jnp.where(lane_p < 128, p4, jnp.where(lane_p < 192, p8, p16)))
        pooled = win[POOL_PAD:POOL_PAD + L, :] * inv_cnt - u
        o_ref[b] = (_dot(pooled.astype(BF16), w_ref[...]) * s_ref[...]).astype(o_ref.dtype)


def _pool_mixer(u_pool, w_bd, scale, per_step):
    B, L, _ = u_pool.shape
    blk = pl.BlockSpec((per_step, L, POOL_WIDTH), lambda b: (b, 0, 0))
    return pl.pallas_call(
        _pool_kernel,
        grid=(B // per_step,),
        in_specs=[blk, _resident((POOL_WIDTH, POOL_WIDTH)), _resident((1, POOL_WIDTH))],
        out_specs=blk,
        out_shape=jax.ShapeDtypeStruct((B, L, POOL_WIDTH), BF16),
        scratch_shapes=[pltpu.VMEM((per_step, L + 2 * POOL_PAD, POOL_WIDTH), F32)],
        compiler_params=_params("parallel"),
        name="pool_mixer",
    )(u_pool, w_bd, scale)


@functools.lru_cache(maxsize=None)
def _hyena_position_consts(L):
    p = np.abs(np.arange(2 * L, dtype=np.float64) - L)
    p[0] = 0.0
    t = p / (L - 1)
    w = 2.0 * math.pi * p / L
    f = np.linspace(1e-4, HY_BANDS - 1, HY_BANDS)
    z = np.zeros((HY_EMB_PAD, 2 * L), np.float64)
    z[0] = t
    z[1:1 + HY_BANDS] = np.cos(f[:, None] * w[None, :])
    z[1 + HY_BANDS:HY_EMB] = -np.sin(f[:, None] * w[None, :])
    deltas = np.abs(np.linspace(math.log(HY_TARGET) / HY_SLOW, math.log(HY_TARGET) / HY_FAST, HY_WIDTH))
    decay = np.exp(-t[:, None] * deltas[None, :])
    return z.astype(np.float32), decay.astype(np.float32)


def _filter_kernel(z_ref, dec_ref, w1_ref, b1_ref, f1_ref, w2_ref, b2_ref, f2_ref, w3_ref, o_ref):
    L = z_ref.shape[1] // 2
    h = jnp.sin(f1_ref[...] * (_dot_f32(w1_ref[...], z_ref[...]) + b1_ref[...]))
    h = jnp.sin(f2_ref[...] * (_dot_f32(w2_ref[...], h) + b2_ref[...]))
    for half, dirn in ((0, 1), (1, 0)):
        rows = slice(half * L, (half + 1) * L)
        k = lax.dot_general(h[:, rows], w3_ref[:, dirn * 2 * HY_WIDTH:(dirn + 1) * 2 * HY_WIDTH],
                            (((0,), (0,)), ((), ())), preferred_element_type=F32,
                            precision=lax.Precision.HIGHEST)
        for o in range(2):
            ko = k[:, o * HY_WIDTH:(o + 1) * HY_WIDTH] * dec_ref[rows, :]
            ko = ko / (jnp.sum(jnp.abs(ko), axis=0, keepdims=True) + 1e-6)
            if half == 0:
                ko = jnp.where(lax.broadcasted_iota(jnp.int32, ko.shape, 0) == 0, 0.0, ko)
            o_ref[o, rows, :] = ko


def _hyena_filters(L, w1, b1, f1, w2, b2, f2, w3):
    z, decay = _hyena_position_consts(L)
    return pl.pallas_call(
        _filter_kernel,
        out_shape=jax.ShapeDtypeStruct((2, 2 * L, HY_WIDTH), F32),
        compiler_params=pltpu.CompilerParams(vmem_limit_bytes=VMEM_LIMIT),
        name="hyena_filters",
    )(jnp.asarray(z), jnp.asarray(decay), w1, b1, f1, w2, b2, f2, w3)


@functools.lru_cache(maxsize=None)
def _dft_consts():
    n = np.arange(DFT_BLOCK, dtype=np.float64)
    j = np.arange(DFT_BLOCK, dtype=np.float64)
    ang = 2.0 * math.pi * np.outer(j, n) / DFT_N
    fwd = np.concatenate([np.cos(ang), -np.sin(ang)], axis=0)
    fwd[DFT_BLOCK] = np.cos(math.pi * n)
    wgt = np.where(j == 0, 1.0, 2.0)[None, :]
    inv = np.concatenate([wgt * np.cos(ang.T), -2.0 * np.sin(ang.T)], axis=1) / DFT_N
    inv[:, DFT_BLOCK] = np.cos(math.pi * n) / DFT_N
    sign = np.tile(np.where(np.arange(DFT_BLOCK) % 2 == 0, 1.0, -1.0), 2)[:, None]
    col0 = fwd[:, 0:1].copy()
    return fwd, inv, sign.astype(np.float32), col0.astype(np.float32)


def _split2(a64):
    hi = _np_bf16(a64)
    return hi, _np_bf16(a64 - hi.astype(np.float64))


def _np_bf16(a64):
    return a64.astype(np.float32).astype(BF16)


def _spectra_kernel(a_ref, fh_ref, fl_ref, sign_ref, col0_ref, g_ref, ny_ref, prev_ref):
    e = pl.program_id(1)
    a = a_ref[0]
    a_hi = a.astype(BF16)
    a_lo = (a - a_hi.astype(F32)).astype(BF16)
    fa = _dot(fh_ref[...], a_hi) + (_dot(fh_ref[...], a_lo) + _dot(fl_ref[...], a_hi))

    @pl.when(e > 0)
    def _():
        g = fa + sign_ref[...] * prev_ref[...]
        ny_ref[0, 0] = g[DFT_BLOCK:DFT_BLOCK + 1, :]
        row = lax.broadcasted_iota(jnp.int32, (DFT_N, HY_WIDTH), 0)
        g_ref[0, 0] = jnp.where(row == DFT_BLOCK, 0.0, g)

    prev_ref[...] = fa - col0_ref[...] * a[0:1, :]


def _hyena_spectra(gg):
    nb = gg.shape[1] // (2 * DFT_BLOCK)
    nd = 2 * nb - 1
    fwd, _, sign, col0 = _dft_consts()
    fh, fl = _split2(fwd)
    return pl.pallas_call(
        _spectra_kernel,
        grid=(2, 2 * nb),
        in_specs=[
            pl.BlockSpec((1, DFT_BLOCK, HY_WIDTH), lambda o, e: (o, e, 0)),
            _resident((DFT_N, DFT_BLOCK)), _resident((DFT_N, DFT_BLOCK)),
            _resident((DFT_N, 1)), _resident((DFT_N, 1)),
        ],
        out_specs=[
            pl.BlockSpec((1, 1, DFT_N, HY_WIDTH), lambda o, e: (o, jnp.maximum(e - 1, 0), 0, 0)),
            pl.BlockSpec((1, 1, 1, HY_WIDTH), lambda o, e: (o, jnp.maximum(e - 1, 0), 0, 0)),
        ],
        out_shape=[
            jax.ShapeDtypeStruct((2, nd, DFT_N, HY_WIDTH), F32),
            jax.ShapeDtypeStruct((2, nd, 1, HY_WIDTH), F32),
        ],
        scratch_shapes=[pltpu.VMEM((DFT_N, HY_WIDTH), F32)],
        compiler_params=_params("parallel", "arbitrary"),
        name="hyena_spectra",
    )(gg, fh, fl, jnp.asarray(sign), jnp.asarray(col0))


SPEC_ROWS = 32
NY_ROWS = 16


def _hyena_kernel(u_ref, cw_ref, cb_ref, hb_ref, g_ref, ny_ref, fwd_ref, inv_ref, o_ref,
                  z_ref, gate_ref, zf_ref, y_ref):
    nseq, L = u_ref.shape[0], u_ref.shape[1]
    nb = L // DFT_BLOCK
    t = lax.broadcasted_iota(jnp.int32, (L, HY_WIDTH), 0)
    ny_row = lax.broadcasted_iota(jnp.int32, (NY_ROWS, HY_WIDTH), 0) == 0

    for b in range(nseq):
        for part in range(3):
            cols = slice(part * HY_WIDTH, (part + 1) * HY_WIDTH)
            u = u_ref[b, :, cols]
            prev = jnp.where(t == 0, 0.0, pltpu.roll(u, 1, axis=0))
            nxt = jnp.where(t == L - 1, 0.0, pltpu.roll(u, L - 1, axis=0))
            y = cb_ref[:, cols] + prev * cw_ref[0:1, cols] + u * cw_ref[1:2, cols] + nxt * cw_ref[2:3, cols]
            if part < 2:
                gate_ref[part] = y
            else:
                z_ref[...] = y

        for o in range(2):
            for j in range(nb):
                blk = slice(j * DFT_BLOCK, (j + 1) * DFT_BLOCK)
                zf_ref[j] = _dot(fwd_ref[...], z_ref[blk, :].astype(BF16))
            bias = hb_ref[o]

            def out_block(i, carry, o=o, bias=bias):
                def spec_rows(c, carry2):
                    r0 = pl.multiple_of(c * SPEC_ROWS, SPEC_ROWS)
                    re = pl.ds(r0, SPEC_ROWS)
                    im = pl.ds(DFT_BLOCK + r0, SPEC_ROWS)
                    top = jnp.zeros((SPEC_ROWS, HY_WIDTH), F32)
                    bot = jnp.zeros((SPEC_ROWS, HY_WIDTH), F32)
                    for j in range(nb):
                        d = i - j + (nb - 1)
                        gr, gi = g_ref[o, d, re, :], g_ref[o, d, im, :]
                        zr, zi = zf_ref[j, re, :], zf_ref[j, im, :]
                        top = top + (gr * zr - gi * zi)
                        bot = bot + (gr * zi + gi * zr)
                    y_ref[re, :] = top.astype(BF16)
                    y_ref[im, :] = bot.astype(BF16)
                    return carry2

                lax.fori_loop(0, DFT_BLOCK // SPEC_ROWS, spec_rows, 0)
                ny = jnp.zeros((1, HY_WIDTH), F32)
                for j in range(nb):
                    ny = ny + ny_ref[o, i - j + (nb - 1)] * zf_ref[j, DFT_BLOCK:DFT_BLOCK + 1, :]
                head = y_ref[DFT_BLOCK:DFT_BLOCK + NY_ROWS, :].astype(F32)
                y_ref[DFT_BLOCK:DFT_BLOCK + NY_ROWS, :] = jnp.where(ny_row, ny, head).astype(BF16)
                conv = _dot(inv_ref[...], y_ref[...])
                rows = pl.ds(pl.multiple_of(i * DFT_BLOCK, DFT_BLOCK), DFT_BLOCK)
                z_ref[rows, :] = gate_ref[o, rows, :] * (conv + bias * z_ref[rows, :])
                return carry

            lax.fori_loop(0, nb, out_block, 0)
        o_ref[b] = z_ref[...].astype(o_ref.dtype)


def _hyena_mixer(u_hy, conv_w, conv_b, hy_bias, spec, nyq, per_step):
    B, L, _ = u_hy.shape
    nd = spec.shape[1]
    fwd, inv, _, _ = _dft_consts()
    return pl.pallas_call(
        _hyena_kernel,
        grid=(B // per_step,),
        in_specs=[
            pl.BlockSpec((per_step, L, 3 * HY_WIDTH), lambda b: (b, 0, 0)),
            _resident((3, 3 * HY_WIDTH)),
            _resident((1, 3 * HY_WIDTH)),
            _resident((2, 1, HY_WIDTH)),
            _resident((2, nd, DFT_N, HY_WIDTH)),
            _resident((2, nd, 1, HY_WIDTH)),
            _resident((DFT_N, DFT_BLOCK)),
            _resident((DFT_BLOCK, DFT_N)),
        ],
        out_specs=pl.BlockSpec((per_step, L, HY_WIDTH), lambda b: (b, 0, 0)),
        out_shape=jax.ShapeDtypeStruct((B, L, HY_WIDTH), BF16),
        scratch_shapes=[
            pltpu.VMEM((L, HY_WIDTH), F32),
            pltpu.VMEM((2, L, HY_WIDTH), F32),
            pltpu.VMEM((L // DFT_BLOCK, DFT_N, HY_WIDTH), F32),
            pltpu.VMEM((DFT_N, HY_WIDTH), BF16),
        ],
        compiler_params=_params("parallel"),
        name="hyena_mixer",
    )(u_hy, conv_w, conv_b, hy_bias.reshape(2, 1, HY_WIDTH), spec, nyq,
      jnp.asarray(fwd, F32).astype(BF16), jnp.asarray(inv, F32).astype(BF16))


HEAD_PAIR = 2 * NA_HEAD_DIM
SOFTMAX_ROWS = 64
PV_CHUNK = 256


def _paired_heads(q, keys, values, bias, s_ref, p_ref):
    M = q.shape[0]
    lane = lax.broadcasted_iota(jnp.int32, (M, HEAD_PAIR), 1)

    def own_lanes(ln, e):
        return (ln < NA_HEAD_DIM) if e == 0 else (ln >= NA_HEAD_DIM)

    def scores(h):
        cols = slice((h // 2) * HEAD_PAIR, (h // 2 + 1) * HEAD_PAIR)
        qh = jnp.where(own_lanes(lane, h % 2), q[:, cols], jnp.zeros((M, HEAD_PAIR), BF16))
        off, m = 0, None
        for i, k in enumerate(keys):
            s = _dot_nt(qh, k[:, cols])
            if bias is not None and i == 0:
                s = s + bias(h)
            s_ref[h % 2, :, off:off + k.shape[0]] = s
            mi = jnp.max(s, axis=-1, keepdims=True)
            m = mi if m is None else jnp.maximum(m, mi)
            off += k.shape[0]
        return m

    def attend(h, m):
        cols = slice((h // 2) * HEAD_PAIR, (h // 2 + 1) * HEAD_PAIR)
        acc, off = None, 0
        for v in values:
            vl = lax.broadcasted_iota(jnp.int32, (v.shape[0], HEAD_PAIR), 1)
            vh = jnp.where(own_lanes(vl, h % 2), v[:, cols], jnp.ones((v.shape[0], HEAD_PAIR), BF16))
            for c0 in range(0, v.shape[0], PV_CHUNK):
                kc = slice(off + c0, off + c0 + PV_CHUNK)
                for r0 in range(0, M, SOFTMAX_ROWS):
                    rows = slice(r0, r0 + SOFTMAX_ROWS)
                    p_ref[h % 2, rows, kc] = jnp.exp(s_ref[h % 2, rows, kc] - m[rows]).astype(BF16)
                part = _dot(p_ref[h % 2, :, kc], vh[c0:c0 + PV_CHUNK])
                acc = part if acc is None else acc + part
            off += v.shape[0]
        return acc / pltpu.roll(acc, NA_HEAD_DIM, axis=1)

    outs, halves = [], []
    m_next = scores(0)
    for h in range(NA_HEADS):
        m = m_next
        if h + 1 < NA_HEADS:
            m_next = scores(h + 1)
        halves.append(attend(h, m))
        if h % 2 == 1:
            outs.append(jnp.where(lane < NA_HEAD_DIM, halves[0], halves[1]))
            halves = []
    return jnp.concatenate(outs, axis=-1)


def _ctx_attn_kernel(q_ref, k_ref, v_ref, o_ref, s_ref, p_ref):
    for b in range(q_ref.shape[0]):
        k = k_ref[b, 0].astype(BF16)
        v = v_ref[b, 0].astype(BF16)
        o_ref[b] = _paired_heads(q_ref[b], [k], [v], None, s_ref, p_ref).astype(o_ref.dtype)


def _context_attention(q, k_cache, v_cache, layer, per_step):
    B, L, _ = q.shape
    spec = pl.BlockSpec((per_step, L, NA_WIDTH), lambda b: (b, 0, 0))
    kv = pl.BlockSpec((per_step, 1, L, NA_WIDTH), lambda b: (b, layer, 0, 0))
    return pl.pallas_call(
        _ctx_attn_kernel,
        grid=(B // per_step,),
        in_specs=[spec, kv, kv],
        out_specs=spec,
        out_shape=jax.ShapeDtypeStruct((B, L, NA_WIDTH), BF16),
        scratch_shapes=[pltpu.VMEM((2, L, L), F32), pltpu.VMEM((2, L, L), BF16)],
        compiler_params=_params("parallel"),
        name="context_attention",
    )(q, k_cache, v_cache)


N_DR = 2 * NA_KH - 1
N_DC = 2 * NA_KW - 1
NA_RB = 4
NA_WIN = NA_RB + NA_KH
WIN_KEYS = NA_WIN * GRID_W


def _bias_kernel(rb_ref, o_ref):
    h = pl.program_id(0)
    qc = lax.broadcasted_iota(jnp.int32, (GRID_W, GRID_W), 0)
    kc = lax.broadcasted_iota(jnp.int32, (GRID_W, GRID_W), 1)
    dc = jnp.clip(kc - qc, -(NA_KW - 1), NA_KW - 1) + (NA_KW - 1)
    col_start = jnp.clip(qc - NA_KW // 2, 0, GRID_W - NA_KW)
    visible = (kc >= col_start) & (kc < col_start + NA_KW)
    blocks = []
    for dr in range(N_DR):
        acc = jnp.zeros((GRID_W, GRID_W), F32)
        for d in range(N_DC):
            acc = jnp.where(dc == d, rb_ref[h * N_DR + dr, d], acc)
        blocks.append(jnp.where(visible, acc, MASK_VALUE))
    hidden = jnp.full((GRID_W, GRID_W), MASK_VALUE, F32)
    for variant, (first, base) in enumerate(((0, NA_KH - 1), (None, NA_KH // 2 - 1), (NA_RB, -1))):
        for a in range(NA_RB):
            lo = a if first is None else first
            row = [blocks[i - a + base] if lo <= i < lo + NA_KH else hidden for i in range(NA_WIN)]
            o_ref[variant, 0, a * GRID_W:(a + 1) * GRID_W, :] = jnp.concatenate(row, axis=-1)


def _bias_table(rel_bias_l):
    return pl.pallas_call(
        _bias_kernel,
        grid=(NA_HEADS,),
        in_specs=[pl.BlockSpec(memory_space=pltpu.SMEM)],
        out_specs=pl.BlockSpec((3, 1, NA_RB * GRID_W, WIN_KEYS), lambda h: (0, h, 0, 0)),
        out_shape=jax.ShapeDtypeStruct((3, NA_HEADS, NA_RB * GRID_W, WIN_KEYS), F32),
        compiler_params=_params("parallel"),
        name="na_bias_table",
    )(rel_bias_l.reshape(NA_HEADS * N_DR, N_DC))


def _na_kernel(q_ref, k_ref, v_ref, kc_ref, vc_ref, tab_ref, o_ref, s_ref, p_ref):
    rows = k_ref.shape[1] // GRID_W
    rb = pl.program_id(1)
    first_row = jnp.clip(rb * NA_RB - NA_KH // 2, 0, rows - NA_WIN)
    start = pl.multiple_of(first_row * GRID_W, NA_RB * GRID_W)
    keys = [k_ref[0, pl.ds(start, WIN_KEYS), :], kc_ref[0, 0].astype(BF16)]
    values = [v_ref[0, pl.ds(start, WIN_KEYS), :], vc_ref[0, 0].astype(BF16)]
    out = _paired_heads(q_ref[0], keys, values, lambda h: tab_ref[0, h], s_ref, p_ref)
    o_ref[0] = out.astype(o_ref.dtype)


def _neighbourhood_attention(q, k, v, cache_k, cache_v, layer, table):
    B, L, _ = q.shape
    rows = L // GRID_W
    assert rows % NA_RB == 0 and rows >= NA_WIN + NA_RB
    nrb = rows // NA_RB
    Lc = cache_k.shape[2]
    seq = pl.BlockSpec((1, L, NA_WIDTH), lambda b, r: (b, 0, 0))
    blk = pl.BlockSpec((1, NA_RB * GRID_W, NA_WIDTH), lambda b, r: (b, r, 0))
    ctx = pl.BlockSpec((1, 1, Lc, NA_WIDTH), lambda b, r: (b, layer, 0, 0))
    placement = lambda r: jnp.where(r == 0, 0, jnp.where(r == nrb - 1, 2, 1))
    tab = pl.BlockSpec((1, NA_HEADS, NA_RB * GRID_W, WIN_KEYS), lambda b, r: (placement(r), 0, 0, 0))
    return pl.pallas_call(
        _na_kernel,
        grid=(B, nrb),
        in_specs=[blk, seq, seq, ctx, ctx, tab],
        out_specs=blk,
        out_shape=jax.ShapeDtypeStruct((B, L, NA_WIDTH), BF16),
        scratch_shapes=[pltpu.VMEM((2, NA_RB * GRID_W, WIN_KEYS + Lc), F32),
                        pltpu.VMEM((2, NA_RB * GRID_W, WIN_KEYS + Lc), BF16)],
        compiler_params=_params("parallel", "arbitrary"),
        name="neighbourhood_attention",
    )(q, k, v, cache_k, cache_v, table)


FF_CHUNK = 1024


def _out_mlp_kernel(x_ref, yp_ref, yh_ref, ya_ref, mod_ref, g_ref, wo_ref, wu_ref, wd_ref, o_ref):
    y = (_dot(yp_ref[0], wo_ref[0:POOL_WIDTH, :])
         + _dot(yh_ref[0], wo_ref[POOL_WIDTH:POOL_WIDTH + HY_WIDTH, :])
         + _dot(ya_ref[0], wo_ref[POOL_WIDTH + HY_WIDTH:D_MODEL, :]))
    x = x_ref[0] + mod_ref[0, 2:3, :] * y
    h = _modulated_norm(x, g_ref[...], mod_ref[0, 3:4, :], mod_ref[0, 4:5, :]).astype(BF16)
    acc = jnp.zeros(x.shape, F32)
    for c in range(D_FF // FF_CHUNK):
        cols = slice(c * FF_CHUNK, (c + 1) * FF_CHUNK)
        a = jnp.maximum(_dot(h, wu_ref[:, cols]), 0.0)
        acc = acc + _dot((a * a).astype(BF16), wd_ref[cols, :])
    o_ref[0] = x + mod_ref[0, 5:6, :] * acc


def _out_mlp(x, y_pool, y_hy, y_na, mod, g2, w_out, w_up, w_down):
    B, L, _ = x.shape
    tm = TOKEN_TILE
    tok = lambda w: pl.BlockSpec((1, tm, w), lambda b, i: (b, i, 0))
    return pl.pallas_call(
        _out_mlp_kernel,
        grid=(B, L // tm),
        in_specs=[
            tok(D_MODEL), tok(POOL_WIDTH), tok(HY_WIDTH), tok(NA_WIDTH),
            pl.BlockSpec((1, N_MOD, D_MODEL), lambda b, i: (b, 0, 0)),
            _resident((1, D_MODEL)),
            _resident((D_MODEL, D_MODEL)),
            _resident((D_MODEL, D_FF)),
            _resident((D_FF, D_MODEL)),
        ],
        out_specs=tok(D_MODEL),
        out_shape=jax.ShapeDtypeStruct((B, L, D_MODEL), F32),
        compiler_params=_params("parallel", "parallel"),
        name="out_mlp",
    )(x, y_pool, y_hy, y_na, mod, g2, w_out, w_up, w_down)


def _block_diag(blocks):
    g, c, d = blocks.shape
    out = jnp.zeros((g * c, g * d), blocks.dtype)
    for i in range(g):
        out = out.at[i * c:(i + 1) * c, i * d:(i + 1) * d].set(blocks[i])
    return out


def kernel(x_prompt, x_sample, cache_k, cache_v, c, c_ctx, norm1_g, norm2_g, w_mod, b_mod, w_in,
           pool_w, pool_scale, hy_conv_w, hy_conv_b, hy_f1_w, hy_f1_b, hy_f1_freq, hy_f2_w, hy_f2_b,
           hy_f2_freq, hy_f3_w, hy_bias, q_norm_g, k_norm_g, rel_bias, w_out, w_up, w_down):
    n_prompt, seq, _ = x_prompt.shape
    n_dec, dec_seq, _ = x_sample.shape

    conds = jnp.zeros((MOD_ROWS, D_MODEL), F32).at[0].set(c_ctx).at[1:1 + n_dec].set(c)
    mod = _adaln(conds, w_mod, b_mod).reshape(DEPTH, MOD_ROWS, N_MOD, D_MODEL)

    head_mean = _block_diag(jnp.full((NA_HEADS, NA_HEAD_DIM, NA_HEAD_DIM), 1.0 / NA_HEAD_DIM, BF16))
    cache_k = cache_k.reshape(n_dec, DEPTH, -1, NA_WIDTH)
    cache_v = cache_v.reshape(n_dec, DEPTH, -1, NA_WIDTH)

    xp = x_prompt.reshape(1, n_prompt * seq, D_MODEL)
    xs = x_sample
    new_k = new_v = None
    for l in range(DEPTH):
        g1 = norm1_g[l].reshape(1, D_MODEL)
        g2 = norm2_g[l].reshape(1, D_MODEL)
        w_in_l = w_in[l].astype(BF16)
        w_out_l = w_out[l].astype(BF16)
        w_up_l = w_up[l].astype(BF16)
        w_down_l = w_down[l].astype(BF16)
        pool_bd = _block_diag(pool_w[l]).astype(BF16)
        pool_s = pool_scale[l].reshape(1, POOL_WIDTH)
        qg = jnp.tile(q_norm_g[l], NA_HEADS).reshape(1, NA_WIDTH)
        kg = jnp.tile(k_norm_g[l], NA_HEADS).reshape(1, NA_WIDTH)
        f1_w = jnp.zeros((HY_FFN, HY_EMB_PAD), F32).at[:, :HY_EMB].set(hy_f1_w[l].T)
        filt_args = (f1_w, hy_f1_b[l].reshape(HY_FFN, 1), hy_f1_freq[l].reshape(HY_FFN, 1),
                     hy_f2_w[l].T, hy_f2_b[l].reshape(HY_FFN, 1), hy_f2_freq[l].reshape(HY_FFN, 1), hy_f3_w[l])
        conv_b = hy_conv_b[l].reshape(1, 3 * HY_WIDTH)

        mod_ctx = mod[l, 0:1]
        u_pool, u_hy, q, new_k, new_v = _in_proj(xp, mod_ctx, g1, w_in_l, head_mean, qg, kg,
                                                 cache=(l, seq, new_k, new_v))
        per_seq = lambda a: a.reshape(n_prompt, seq, a.shape[-1])
        spec, nyq = _hyena_spectra(_hyena_filters(seq, *filt_args))
        y_pool = _pool_mixer(per_seq(u_pool), pool_bd, pool_s, CTX_PER_STEP)
        y_hy = _hyena_mixer(per_seq(u_hy), hy_conv_w[l], conv_b, hy_bias[l], spec, nyq, CTX_PER_STEP)
        y_na = _context_attention(per_seq(q), new_k, new_v, l, CTX_PER_STEP)
        flat = lambda a: a.reshape(1, n_prompt * seq, a.shape[-1])
        xp = _out_mlp(xp, flat(y_pool), flat(y_hy), flat(y_na), mod_ctx, g2, w_out_l, w_up_l, w_down_l)

        mod_lat = mod[l, 1:1 + n_dec]
        u_pool, u_hy, q, k, v = _in_proj(xs, mod_lat, g1, w_in_l, head_mean, qg, kg)
        spec, nyq = _hyena_spectra(_hyena_filters(dec_seq, *filt_args))
        y_pool = _pool_mixer(u_pool, pool_bd, pool_s, 1)
        y_hy = _hyena_mixer(u_hy, hy_conv_w[l], conv_b, hy_bias[l], spec, nyq, 1)
        y_na = _neighbourhood_attention(q, k, v, cache_k, cache_v, l, _bias_table(rel_bias[l]))
        xs = _out_mlp(xs, y_pool, y_hy, y_na, mod_lat, g2, w_out_l, w_up_l, w_down_l)

    shape_kv = (n_prompt, DEPTH, seq, NA_HEADS, NA_HEAD_DIM)
    return (xp.reshape(n_prompt, seq, D_MODEL), xs, new_k.reshape(shape_kv), new_v.reshape(shape_kv))
```

```python
import functools
import math

import numpy as np
import jax
import jax.numpy as jnp
from jax import lax
from jax.experimental import pallas as pl
from jax.experimental.pallas import tpu as pltpu

F32 = jnp.float32
BF16 = jnp.bfloat16

D_MODEL = 1024
DEPTH = 2
GRID_W = 64
POOL_WIDTH = 256
POOL_GROUPS = 4
POOL_GROUP_DIM = 64
HY_WIDTH = 256
HY_BANDS = 16
HY_EMB = 1 + 2 * HY_BANDS
HY_EMB_PAD = 128
HY_FFN = 64
HY_TARGET = 1e-2
HY_FAST = 0.3
HY_SLOW = 1.5
NA_HEAD_DIM = 64
NA_WIDTH = 512
NA_HEADS = 8
NA_KH = 8
NA_KW = 16
N_MOD = 6
D_FF = 4096
NORM_EPS = 1e-6
ATT_SCALE = NA_HEAD_DIM ** -0.5

C_POOL = 0
C_HY = POOL_WIDTH
C_Q = C_HY + 3 * HY_WIDTH
C_K = C_Q + NA_WIDTH
C_V = C_K + NA_WIDTH
IN_WIDTH = C_V + NA_WIDTH

DFT_BLOCK = 256
DFT_N = 2 * DFT_BLOCK
MASK_VALUE = -1e30
VMEM_LIMIT = 56 * 1024 * 1024
TOKEN_TILE = 512
MOD_ROWS = 8
CTX_PER_STEP = 4


def _params(*sem):
    return pltpu.CompilerParams(dimension_semantics=sem, vmem_limit_bytes=VMEM_LIMIT)


def _resident(shape):
    return pl.BlockSpec(shape, lambda *_: (0,) * len(shape), pipeline_mode=pl.Buffered(1))


def _dot(a, b):
    return jnp.dot(a, b, preferred_element_type=F32)


def _dot_nt(a, b):
    return lax.dot_general(a, b, (((1,), (1,)), ((), ())), preferred_element_type=F32)


def _dot_f32(a, b):
    return jnp.dot(a, b, preferred_element_type=F32, precision=lax.Precision.HIGHEST)


def _adaln_kernel(c_ref, w_ref, b_ref, o_ref):
    c = c_ref[...]
    s = c * (1.0 / (1.0 + jnp.exp(-c)))
    o_ref[0] = _dot(s.astype(BF16), w_ref[0].astype(BF16)) + b_ref[0]


def _adaln(conds, w_mod, b_mod):
    tn = 1536
    n = N_MOD * D_MODEL
    return pl.pallas_call(
        _adaln_kernel,
        grid=(DEPTH, n // tn),
        in_specs=[
            pl.BlockSpec((MOD_ROWS, D_MODEL), lambda l, j: (0, 0)),
            pl.BlockSpec((1, D_MODEL, tn), lambda l, j: (l, 0, j)),
            pl.BlockSpec((1, 1, tn), lambda l, j: (l, 0, j)),
        ],
        out_specs=pl.BlockSpec((1, MOD_ROWS, tn), lambda l, j: (l, 0, j)),
        out_shape=jax.ShapeDtypeStruct((DEPTH, MOD_ROWS, n), F32),
        compiler_params=_params("parallel", "parallel"),
        name="adaln",
    )(conds, w_mod, b_mod.reshape(DEPTH, 1, n))


def _modulated_norm(x, g, shift, scale):
    ms = jnp.mean(x * x, axis=-1, keepdims=True)
    return (x * lax.rsqrt(ms + NORM_EPS) * g) * (1.0 + scale) + shift


def _inproj_kernel(x_ref, mod_ref, g_ref, w_ref, hm_ref, qg_ref, kg_ref, *rest):
    up_ref, uh_ref, q_ref, k_ref, v_ref = rest[-5:]
    h = _modulated_norm(x_ref[0], g_ref[...], mod_ref[0, 0:1, :], mod_ref[0, 1:2, :]).astype(BF16)
    up_ref[0] = _dot(h, w_ref[:, C_POOL:C_HY])
    uh_ref[0] = _dot(h, w_ref[:, C_HY:C_Q])

    def head_norm(u, g):
        ms = _dot((u * u).astype(BF16), hm_ref[...])
        return u * lax.rsqrt(ms + NORM_EPS) * g

    q = head_norm(_dot(h, w_ref[:, C_Q:C_K]), qg_ref[...])
    q_ref[0] = (q * ATT_SCALE).astype(q_ref.dtype)
    k = head_norm(_dot(h, w_ref[:, C_K:C_V]), kg_ref[...])
    v = _dot(h, w_ref[:, C_V:IN_WIDTH])
    k_ref[...] = k.astype(k_ref.dtype).reshape(k_ref.shape)
    v_ref[...] = v.astype(v_ref.dtype).reshape(v_ref.shape)


def _in_proj(x, mod, g1, w_in, head_mean, qg, kg, cache=None):
    B, L, _ = x.shape
    tm = TOKEN_TILE
    tok = lambda w: pl.BlockSpec((1, tm, w), lambda b, i: (b, i, 0))
    in_specs = [
        tok(D_MODEL),
        pl.BlockSpec((1, N_MOD, D_MODEL), lambda b, i: (b, 0, 0)),
        _resident((1, D_MODEL)),
        _resident((D_MODEL, IN_WIDTH)),
        _resident((NA_WIDTH, NA_WIDTH)),
        _resident((1, NA_WIDTH)),
        _resident((1, NA_WIDTH)),
    ]
    args = [x, mod, g1, w_in, head_mean, qg, kg]
    aliases = {}
    if cache is None:
        kv_spec = tok(NA_WIDTH)
        kv_shape = jax.ShapeDtypeStruct((B, L, NA_WIDTH), BF16)
    else:
        layer, seq, k_cache, v_cache = cache
        assert B == 1 and tm % seq == 0
        per_tile = tm // seq
        kv_spec = pl.BlockSpec((per_tile, 1, seq, NA_WIDTH), lambda b, i: (i, layer, 0, 0))
        kv_shape = jax.ShapeDtypeStruct((L // seq, DEPTH, seq, NA_WIDTH), F32)
        if k_cache is not None:
            in_specs += [pl.BlockSpec(memory_space=pl.ANY)] * 2
            aliases = {len(args): 3, len(args) + 1: 4}
            args += [k_cache, v_cache]
    return pl.pallas_call(
        _inproj_kernel,
        grid=(B, L // tm),
        in_specs=in_specs,
        out_specs=[tok(POOL_WIDTH), tok(3 * HY_WIDTH), tok(NA_WIDTH), kv_spec, kv_spec],
        out_shape=[
            jax.ShapeDtypeStruct((B, L, POOL_WIDTH), F32),
            jax.ShapeDtypeStruct((B, L, 3 * HY_WIDTH), F32),
            jax.ShapeDtypeStruct((B, L, NA_WIDTH), BF16),
            kv_shape,
            kv_shape,
        ],
        input_output_aliases=aliases,
        compiler_params=_params("parallel", "parallel"),
        name="in_proj",
    )(*args)


POOL_PAD = 8


def _pool_kernel(u_ref, w_ref, s_ref, o_ref, pad_ref):
    nseq, L = u_ref.shape[0], u_ref.shape[1]
    n = L + 2 * POOL_PAD

    def sh(v, s):
        return pltpu.roll(v, s % n, axis=0)

    lane_p = lax.broadcasted_iota(jnp.int32, (n, POOL_WIDTH), 1)
    t = lax.broadcasted_iota(jnp.int32, (L, POOL_WIDTH), 0)
    lane = lax.broadcasted_iota(jnp.int32, (L, POOL_WIDTH), 1)
    half = jnp.where(lane < 64, 1, jnp.where(lane < 128, 2, jnp.where(lane < 192, 4, 8)))
    lo = jnp.maximum(t - half, 0)
    hi = jnp.minimum(t + half - 1, L - 1)
    inv_cnt = 1.0 / (hi - lo + 1).astype(F32)

    for b in range(nseq):
        u = u_ref[b]
        pad_ref[b, 0:POOL_PAD, :] = jnp.zeros((POOL_PAD, POOL_WIDTH), F32)
        pad_ref[b, POOL_PAD + L:n, :] = jnp.zeros((POOL_PAD, POOL_WIDTH), F32)
        pad_ref[b, POOL_PAD:POOL_PAD + L, :] = u
        a = pad_ref[b]
        p2 = a + sh(a, 1)
        p4 = sh(p2, 1) + sh(p2, -1)
        p8 = sh(p4, 2) + sh(p4, -2)
        p16 = sh(p8, 4) + sh(p8, -4)
        win = jnp.where(lane_p < 64, p2, jnp.where(lane_p < 128, p4, jnp.where(lane_p < 192, p8, p16)))
        pooled = win[POOL_PAD:POOL_PAD + L, :] * inv_cnt - u
        o_ref[b] = (_dot(pooled.astype(BF16), w_ref[...]) * s_ref[...]).astype(o_ref.dtype)


def _pool_mixer(u_pool, w_bd, scale, per_step):
    B, L, _ = u_pool.shape
    blk = pl.BlockSpec((per_step, L, POOL_WIDTH), lambda b: (b, 0, 0))
    return pl.pallas_call(
        _pool_kernel,
        grid=(B // per_step,),
        in_specs=[blk, _resident((POOL_WIDTH, POOL_WIDTH)), _resident((1, POOL_WIDTH))],
        out_specs=blk,
        out_shape=jax.ShapeDtypeStruct((B, L, POOL_WIDTH), BF16),
        scratch_shapes=[pltpu.VMEM((per_step, L + 2 * POOL_PAD, POOL_WIDTH), F32)],
        compiler_params=_params("parallel"),
        name="pool_mixer",
    )(u_pool, w_bd, scale)


@functools.lru_cache(maxsize=None)
def _hyena_position_consts(L):
    p = np.abs(np.arange(2 * L, dtype=np.float64) - L)
    p[0] = 0.0
    t = p / (L - 1)
    w = 2.0 * math.pi * p / L
    f = np.linspace(1e-4, HY_BANDS - 1, HY_BANDS)
    z = np.zeros((HY_EMB_PAD, 2 * L), np.float64)
    z[0] = t
    z[1:1 + HY_BANDS] = np.cos(f[:, None] * w[None, :])
    z[1 + HY_BANDS:HY_EMB] = -np.sin(f[:, None] * w[None, :])
    deltas = np.abs(np.linspace(math.log(HY_TARGET) / HY_SLOW, math.log(HY_TARGET) / HY_FAST, HY_WIDTH))
    decay = np.exp(-t[:, None] * deltas[None, :])
    return z.astype(np.float32), decay.astype(np.float32)


FILTER_BLOCK = 512


def _filter_kernel(z_ref, dec_ref, w1_ref, b1_ref, f1_ref, w2_ref, b2_ref, f2_ref, w3_ref, o_ref, sum_ref):
    half, j = pl.program_id(0), pl.program_id(1)
    h = jnp.sin(f1_ref[...] * (_dot_f32(w1_ref[...], z_ref[...]) + b1_ref[...]))
    h = jnp.sin(f2_ref[...] * (_dot_f32(w2_ref[...], h) + b2_ref[...]))
    k = lax.dot_general(h, w3_ref[0], (((0,), (0,)), ((), ())), preferred_element_type=F32,
                        precision=lax.Precision.HIGHEST)
    unused = jnp.logical_and(lax.broadcasted_iota(jnp.int32, dec_ref.shape, 0) == 0,
                             jnp.logical_and(half == 0, j == 0))
    for o in range(2):
        ko = k[:, o * HY_WIDTH:(o + 1) * HY_WIDTH] * dec_ref[...]
        part = jnp.sum(jnp.abs(ko), axis=0, keepdims=True)

        @pl.when(j == 0)
        def _():
            sum_ref[0, o] = part

        @pl.when(j > 0)
        def _():
            sum_ref[0, o] = sum_ref[0, o] + part

        o_ref[o] = jnp.where(unused, 0.0, ko)


def _hyena_filters(L, w1, b1, f1, w2, b2, f2, w3):
    z, decay = _hyena_position_consts(L)
    fb = min(L, FILTER_BLOCK)
    nblk = L // fb
    return pl.pallas_call(
        _filter_kernel,
        grid=(2, nblk),
        in_specs=[
            pl.BlockSpec((HY_EMB_PAD, fb), lambda hf, j: (0, hf * nblk + j)),
            pl.BlockSpec((fb, HY_WIDTH), lambda hf, j: (hf * nblk + j, 0)),
            _resident((HY_FFN, HY_EMB_PAD)), _resident((HY_FFN, 1)), _resident((HY_FFN, 1)),
            _resident((HY_FFN, HY_FFN)), _resident((HY_FFN, 1)), _resident((HY_FFN, 1)),
            pl.BlockSpec((1, HY_FFN, 2 * HY_WIDTH), lambda hf, j: (1 - hf, 0, 0)),
        ],
        out_specs=[
            pl.BlockSpec((2, fb, HY_WIDTH), lambda hf, j: (0, hf * nblk + j, 0)),
            pl.BlockSpec((1, 2, 1, HY_WIDTH), lambda hf, j: (hf, 0, 0, 0)),
        ],
        out_shape=[
            jax.ShapeDtypeStruct((2, 2 * L, HY_WIDTH), F32),
            jax.ShapeDtypeStruct((2, 2, 1, HY_WIDTH), F32),
        ],
        compiler_params=_params("parallel", "arbitrary"),
        name="hyena_filters",
    )(jnp.asarray(z), jnp.asarray(decay), w1, b1, f1, w2, b2, f2, w3)


@functools.lru_cache(maxsize=None)
def _dft_consts():
    n = np.arange(DFT_BLOCK, dtype=np.float64)
    j = np.arange(DFT_BLOCK, dtype=np.float64)
    ang = 2.0 * math.pi * np.outer(j, n) / DFT_N
    fwd = np.concatenate([np.cos(ang), -np.sin(ang)], axis=0)
    fwd[DFT_BLOCK] = np.cos(math.pi * n)
    wgt = np.where(j == 0, 1.0, 2.0)[None, :]
    inv = np.concatenate([wgt * np.cos(ang.T), -2.0 * np.sin(ang.T)], axis=1) / DFT_N
    inv[:, DFT_BLOCK] = np.cos(math.pi * n) / DFT_N
    sign = np.tile(np.where(np.arange(DFT_BLOCK) % 2 == 0, 1.0, -1.0), 2)[:, None]
    col0 = fwd[:, 0:1].copy()
    return fwd, inv, sign.astype(np.float32), col0.astype(np.float32)


def _split2(a64):
    hi = _np_bf16(a64)
    return hi, _np_bf16(a64 - hi.astype(np.float64))


def _np_bf16(a64):
    return a64.astype(np.float32).astype(BF16)


def _spectra_kernel(a_ref, sum_ref, fh_ref, fl_ref, sign_ref, col0_ref, g_ref, ny_ref, prev_ref):
    @pl.when(pl.program_id(1) == 0)
    def _():
        prev_ref[...] = jnp.zeros(prev_ref.shape, F32)

    inv_norm = 1.0 / (sum_ref[0, 0] + 1e-6)
    row = lax.broadcasted_iota(jnp.int32, (DFT_N, HY_WIDTH), 0)
    for t in range(g_ref.shape[1]):
        a = a_ref[0, t * DFT_BLOCK:(t + 1) * DFT_BLOCK, :] * inv_norm
        a_hi = a.astype(BF16)
        a_lo = (a - a_hi.astype(F32)).astype(BF16)
        fa = _dot(fh_ref[...], a_hi) + (_dot(fh_ref[...], a_lo) + _dot(fl_ref[...], a_hi))
        g = fa + sign_ref[...] * prev_ref[...]
        ny_ref[0, t] = g[DFT_BLOCK:DFT_BLOCK + 1, :]
        g_ref[0, t] = jnp.where(row == DFT_BLOCK, 0.0, g)
        prev_ref[...] = fa - col0_ref[...] * a[0:1, :]


SPECTRA_PER_STEP = 4


def _hyena_spectra(gg, sums):
    nb = gg.shape[1] // (2 * DFT_BLOCK)
    per = min(nb, SPECTRA_PER_STEP)
    steps_per_half = nb // per
    fwd, _, sign, col0 = _dft_consts()
    fh, fl = _split2(fwd)
    return pl.pallas_call(
        _spectra_kernel,
        grid=(2, 2 * steps_per_half),
        in_specs=[
            pl.BlockSpec((1, per * DFT_BLOCK, HY_WIDTH), lambda o, s: (o, s, 0)),
            pl.BlockSpec((1, 1, 1, HY_WIDTH), lambda o, s: (s // steps_per_half, o, 0, 0)),
            _resident((DFT_N, DFT_BLOCK)), _resident((DFT_N, DFT_BLOCK)),
            _resident((DFT_N, 1)), _resident((DFT_N, 1)),
        ],
        out_specs=[
            pl.BlockSpec((1, per, DFT_N, HY_WIDTH), lambda o, s: (o, s, 0, 0)),
            pl.BlockSpec((1, per, 1, HY_WIDTH), lambda o, s: (o, s, 0, 0)),
        ],
        out_shape=[
            jax.ShapeDtypeStruct((2, 2 * nb, DFT_N, HY_WIDTH), F32),
            jax.ShapeDtypeStruct((2, 2 * nb, 1, HY_WIDTH), F32),
        ],
        scratch_shapes=[pltpu.VMEM((DFT_N, HY_WIDTH), F32)],
        compiler_params=_params("parallel", "arbitrary"),
        name="hyena_spectra",
    )(gg, sums, fh, fl, jnp.asarray(sign), jnp.asarray(col0))


SPEC_ROWS = 32
NY_ROWS = 16


def _hyena_kernel(u_ref, cw_ref, cb_ref, hb_ref, g_ref, ny_ref, fwd_ref, inv_ref, o_ref,
                  z_ref, gate_ref, zf_ref, y_ref):
    nseq, L = u_ref.shape[0], u_ref.shape[1]
    nb = L // DFT_BLOCK
    t = lax.broadcasted_iota(jnp.int32, (L, HY_WIDTH), 0)
    ny_row = lax.broadcasted_iota(jnp.int32, (NY_ROWS, HY_WIDTH), 0) == 0

    for b in range(nseq):
        for part in range(3):
            cols = slice(part * HY_WIDTH, (part + 1) * HY_WIDTH)
            u = u_ref[b, :, cols]
            prev = jnp.where(t == 0, 0.0, pltpu.roll(u, 1, axis=0))
            nxt = jnp.where(t == L - 1, 0.0, pltpu.roll(u, L - 1, axis=0))
            y = cb_ref[:, cols] + prev * cw_ref[0:1, cols] + u * cw_ref[1:2, cols] + nxt * cw_ref[2:3, cols]
            if part < 2:
                gate_ref[part] = y
            else:
                z_ref[...] = y

        for o in range(2):
            for j in range(nb):
                blk = slice(j * DFT_BLOCK, (j + 1) * DFT_BLOCK)
                zf_ref[j] = _dot(fwd_ref[...], z_ref[blk, :].astype(BF16))
            bias = hb_ref[o]

            def out_block(i, carry, o=o, bias=bias):
                def spec_rows(c, carry2):
                    r0 = pl.multiple_of(c * SPEC_ROWS, SPEC_ROWS)
                    re = pl.ds(r0, SPEC_ROWS)
                    im = pl.ds(DFT_BLOCK + r0, SPEC_ROWS)
                    top = jnp.zeros((SPEC_ROWS, HY_WIDTH), F32)
                    bot = jnp.zeros((SPEC_ROWS, HY_WIDTH), F32)
                    for j in range(nb):
                        d = i - j + nb
                        gr, gi = g_ref[o, d, re, :], g_ref[o, d, im, :]
                        zr, zi = zf_ref[j, re, :], zf_ref[j, im, :]
                        top = top + (gr * zr - gi * zi)
                        bot = bot + (gr * zi + gi * zr)
                    y_ref[re, :] = top.astype(BF16)
                    y_ref[im, :] = bot.astype(BF16)
                    return carry2

                lax.fori_loop(0, DFT_BLOCK // SPEC_ROWS, spec_rows, 0, unroll=(nb == 1))
                ny = jnp.zeros((1, HY_WIDTH), F32)
                for j in range(nb):
                    ny = ny + ny_ref[o, i - j + nb] * zf_ref[j, DFT_BLOCK:DFT_BLOCK + 1, :]
                head = y_ref[DFT_BLOCK:DFT_BLOCK + NY_ROWS, :].astype(F32)
                y_ref[DFT_BLOCK:DFT_BLOCK + NY_ROWS, :] = jnp.where(ny_row, ny, head).astype(BF16)
                conv = _dot(inv_ref[...], y_ref[...])
                rows = pl.ds(pl.multiple_of(i * DFT_BLOCK, DFT_BLOCK), DFT_BLOCK)
                z_ref[rows, :] = gate_ref[o, rows, :] * (conv + bias * z_ref[rows, :])
                return carry

            lax.fori_loop(0, nb, out_block, 0, unroll=(nb == 1))
        o_ref[b] = z_ref[...].astype(o_ref.dtype)


def _hyena_mixer(u_hy, conv_w, conv_b, hy_bias, spec, nyq, per_step):
    B, L, _ = u_hy.shape
    nd = spec.shape[1]
    fwd, inv, _, _ = _dft_consts()
    return pl.pallas_call(
        _hyena_kernel,
        grid=(B // per_step,),
        in_specs=[
            pl.BlockSpec((per_step, L, 3 * HY_WIDTH), lambda b: (b, 0, 0)),
            _resident((3, 3 * HY_WIDTH)),
            _resident((1, 3 * HY_WIDTH)),
            _resident((2, 1, HY_WIDTH)),
            _resident((2, nd, DFT_N, HY_WIDTH)),
            _resident((2, nd, 1, HY_WIDTH)),
            _resident((DFT_N, DFT_BLOCK)),
            _resident((DFT_BLOCK, DFT_N)),
        ],
        out_specs=pl.BlockSpec((per_step, L, HY_WIDTH), lambda b: (b, 0, 0)),
        out_shape=jax.ShapeDtypeStruct((B, L, HY_WIDTH), BF16),
        scratch_shapes=[
            pltpu.VMEM((L, HY_WIDTH), F32),
            pltpu.VMEM((2, L, HY_WIDTH), F32),
            pltpu.VMEM((L // DFT_BLOCK, DFT_N, HY_WIDTH), F32),
            pltpu.VMEM((DFT_N, HY_WIDTH), BF16),
        ],
        compiler_params=_params("parallel"),
        name="hyena_mixer",
    )(u_hy, conv_w, conv_b, hy_bias.reshape(2, 1, HY_WIDTH), spec, nyq,
      jnp.asarray(fwd, F32).astype(BF16), jnp.asarray(inv, F32).astype(BF16))


HEAD_PAIR = 2 * NA_HEAD_DIM
SOFTMAX_ROWS = 64
PV_CHUNK = 256


def _paired_heads(q, keys, values, bias, s_ref, p_ref):
    M = q.shape[0]
    lane = lax.broadcasted_iota(jnp.int32, (M, HEAD_PAIR), 1)

    def own_lanes(ln, e):
        return (ln < NA_HEAD_DIM) if e == 0 else (ln >= NA_HEAD_DIM)

    def scores(h):
        cols = slice((h // 2) * HEAD_PAIR, (h // 2 + 1) * HEAD_PAIR)
        qh = jnp.where(own_lanes(lane, h % 2), q[:, cols], jnp.zeros((M, HEAD_PAIR), BF16))
        off, m = 0, None
        for i, k in enumerate(keys):
            s = _dot_nt(qh, k[:, cols])
            if bias is not None and i == 0:
                s = s + bias(h)
            s_ref[h % 2, :, off:off + k.shape[0]] = s
            mi = jnp.max(s, axis=-1, keepdims=True)
            m = mi if m is None else jnp.maximum(m, mi)
            off += k.shape[0]
        return m

    def attend(h, m):
        cols = slice((h // 2) * HEAD_PAIR, (h // 2 + 1) * HEAD_PAIR)
        acc, off = None, 0
        for v in values:
            vl = lax.broadcasted_iota(jnp.int32, (v.shape[0], HEAD_PAIR), 1)
            vh = jnp.where(own_lanes(vl, h % 2), v[:, cols], jnp.ones((v.shape[0], HEAD_PAIR), BF16))
            for c0 in range(0, v.shape[0], PV_CHUNK):
                kc = slice(off + c0, off + c0 + PV_CHUNK)
                for r0 in range(0, M, SOFTMAX_ROWS):
                    rows = slice(r0, r0 + SOFTMAX_ROWS)
                    p_ref[h % 2, rows, kc] = jnp.exp(s_ref[h % 2, rows, kc] - m[rows]).astype(BF16)
                part = _dot(p_ref[h % 2, :, kc], vh[c0:c0 + PV_CHUNK])
                acc = part if acc is None else acc + part
            off += v.shape[0]
        return acc / pltpu.roll(acc, NA_HEAD_DIM, axis=1)

    outs, halves = [], []
    m_next = scores(0)
    for h in range(NA_HEADS):
        m = m_next
        if h + 1 < NA_HEADS:
            m_next = scores(h + 1)
        halves.append(attend(h, m))
        if h % 2 == 1:
            outs.append(jnp.where(lane < NA_HEAD_DIM, halves[0], halves[1]))
            halves = []
    return jnp.concatenate(outs, axis=-1)


def _ctx_attn_kernel(q_ref, k_ref, v_ref, o_ref, s_ref, p_ref):
    nseq, L = q_ref.shape[0], q_ref.shape[1]
    even = lax.broadcasted_iota(jnp.int32, (L, HEAD_PAIR), 1) < NA_HEAD_DIM
    zero = jnp.zeros((L, HEAD_PAIR), BF16)
    units = [(b, p) for b in range(nseq) for p in range(NA_HEADS // 2)]

    def scores(n):
        b, p = units[n]
        cols = slice(p * HEAD_PAIR, (p + 1) * HEAD_PAIR)
        qp = q_ref[b, :, cols]
        q2 = jnp.concatenate([jnp.where(even, qp, zero), jnp.where(even, zero, qp)], axis=0)
        s_ref[n % 2] = _dot_nt(q2, k_ref[b, 0, :, cols].astype(BF16))

    def attend(n):
        b, p = units[n]
        cols = slice(p * HEAD_PAIR, (p + 1) * HEAD_PAIR)
        for r0 in range(0, 2 * L, SOFTMAX_ROWS):
            rows = slice(r0, r0 + SOFTMAX_ROWS)
            s = s_ref[n % 2, rows, :]
            e = jnp.exp(s - jnp.max(s, axis=-1, keepdims=True))
            p_ref[n % 2, rows, :] = (e * (1.0 / jnp.sum(e, axis=-1, keepdims=True))).astype(BF16)
        acc = _dot(p_ref[n % 2], v_ref[b, 0, :, cols].astype(BF16))
        o_ref[b, :, cols] = jnp.where(even, acc[:L], acc[L:]).astype(o_ref.dtype)

    scores(0)
    for n in range(len(units)):
        if n + 1 < len(units):
            scores(n + 1)
        attend(n)


def _context_attention(q, k_cache, v_cache, layer, per_step):
    B, L, _ = q.shape
    spec = pl.BlockSpec((per_step, L, NA_WIDTH), lambda b: (b, 0, 0))
    kv = pl.BlockSpec((per_step, 1, L, NA_WIDTH), lambda b: (b, layer, 0, 0))
    return pl.pallas_call(
        _ctx_attn_kernel,
        grid=(B // per_step,),
        in_specs=[spec, kv, kv],
        out_specs=spec,
        out_shape=jax.ShapeDtypeStruct((B, L, NA_WIDTH), BF16),
        scratch_shapes=[pltpu.VMEM((2, 2 * L, L), F32), pltpu.VMEM((2, 2 * L, L), BF16)],
        compiler_params=_params("parallel"),
        name="context_attention",
    )(q, k_cache, v_cache)


N_DR = 2 * NA_KH - 1
N_DC = 2 * NA_KW - 1
NA_RB = 4
NA_WIN = NA_RB + NA_KH
WIN_KEYS = NA_WIN * GRID_W


def _bias_kernel(rb_ref, o_ref):
    h = pl.program_id(0)
    qc = lax.broadcasted_iota(jnp.int32, (GRID_W, GRID_W), 0)
    kc = lax.broadcasted_iota(jnp.int32, (GRID_W, GRID_W), 1)
    dc = jnp.clip(kc - qc, -(NA_KW - 1), NA_KW - 1) + (NA_KW - 1)
    col_start = jnp.clip(qc - NA_KW // 2, 0, GRID_W - NA_KW)
    visible = (kc >= col_start) & (kc < col_start + NA_KW)
    blocks = []
    for dr in range(N_DR):
        acc = jnp.zeros((GRID_W, GRID_W), F32)
        for d in range(N_DC):
            acc = jnp.where(dc == d, rb_ref[h * N_DR + dr, d], acc)
        blocks.append(jnp.where(visible, acc, MASK_VALUE))
    hidden = jnp.full((GRID_W, GRID_W), MASK_VALUE, F32)
    for variant, (first, base) in enumerate(((0, NA_KH - 1), (None, NA_KH // 2 - 1), (NA_RB, -1))):
        for a in range(NA_RB):
            lo = a if first is None else first
            row = [blocks[i - a + base] if lo <= i < lo + NA_KH else hidden for i in range(NA_WIN)]
            o_ref[variant, 0, a * GRID_W:(a + 1) * GRID_W, :] = jnp.concatenate(row, axis=-1)


def _bias_table(rel_bias_l):
    return pl.pallas_call(
        _bias_kernel,
        grid=(NA_HEADS,),
        in_specs=[pl.BlockSpec(memory_space=pltpu.SMEM)],
        out_specs=pl.BlockSpec((3, 1, NA_RB * GRID_W, WIN_KEYS), lambda h: (0, h, 0, 0)),
        out_shape=jax.ShapeDtypeStruct((3, NA_HEADS, NA_RB * GRID_W, WIN_KEYS), F32),
        compiler_params=_params("parallel"),
        name="na_bias_table",
    )(rel_bias_l.reshape(NA_HEADS * N_DR, N_DC))


def _na_kernel(q_ref, k_ref, v_ref, kc_ref, vc_ref, tab_ref, o_ref, s_ref, p_ref):
    rows = k_ref.shape[1] // GRID_W
    rb = pl.program_id(1)
    first_row = jnp.clip(rb * NA_RB - NA_KH // 2, 0, rows - NA_WIN)
    start = pl.multiple_of(first_row * GRID_W, NA_RB * GRID_W)
    keys = [k_ref[0, pl.ds(start, WIN_KEYS), :], kc_ref[0, 0].astype(BF16)]
    values = [v_ref[0, pl.ds(start, WIN_KEYS), :], vc_ref[0, 0].astype(BF16)]
    out = _paired_heads(q_ref[0], keys, values, lambda h: tab_ref[0, h], s_ref, p_ref)
    o_ref[0] = out.astype(o_ref.dtype)


def _neighbourhood_attention(q, k, v, cache_k, cache_v, layer, table):
    B, L, _ = q.shape
    rows = L // GRID_W
    assert rows % NA_RB == 0 and rows >= NA_WIN + NA_RB
    nrb = rows // NA_RB
    Lc = cache_k.shape[2]
    seq = pl.BlockSpec((1, L, NA_WIDTH), lambda b, r: (b, 0, 0))
    blk = pl.BlockSpec((1, NA_RB * GRID_W, NA_WIDTH), lambda b, r: (b, r, 0))
    ctx = pl.BlockSpec((1, 1, Lc, NA_WIDTH), lambda b, r: (b, layer, 0, 0))
    placement = lambda r: jnp.where(r == 0, 0, jnp.where(r == nrb - 1, 2, 1))
    tab = pl.BlockSpec((1, NA_HEADS, NA_RB * GRID_W, WIN_KEYS), lambda b, r: (placement(r), 0, 0, 0))
    return pl.pallas_call(
        _na_kernel,
        grid=(B, nrb),
        in_specs=[blk, seq, seq, ctx, ctx, tab],
        out_specs=blk,
        out_shape=jax.ShapeDtypeStruct((B, L, NA_WIDTH), BF16),
        scratch_shapes=[pltpu.VMEM((2, NA_RB * GRID_W, WIN_KEYS + Lc), F32),
                        pltpu.VMEM((2, NA_RB * GRID_W, WIN_KEYS + Lc), BF16)],
        compiler_params=_params("parallel", "arbitrary"),
        name="neighbourhood_attention",
    )(q, k, v, cache_k, cache_v, table)


FF_CHUNK = 1024


def _out_mlp_kernel(x_ref, yp_ref, yh_ref, ya_ref, mod_ref, g_ref, wo_ref, wu_ref, wd_ref, o_ref):
    y = (_dot(yp_ref[0], wo_ref[0:POOL_WIDTH, :])
         + _dot(yh_ref[0], wo_ref[POOL_WIDTH:POOL_WIDTH + HY_WIDTH, :])
         + _dot(ya_ref[0], wo_ref[POOL_WIDTH + HY_WIDTH:D_MODEL, :]))
    x = x_ref[0] + mod_ref[0, 2:3, :] * y
    h = _modulated_norm(x, g_ref[...], mod_ref[0, 3:4, :], mod_ref[0, 4:5, :]).astype(BF16)
    acc = jnp.zeros(x.shape, F32)
    for c in range(D_FF // FF_CHUNK):
        cols = slice(c * FF_CHUNK, (c + 1) * FF_CHUNK)
        a = jnp.maximum(_dot(h, wu_ref[:, cols]), 0.0)
        acc = acc + _dot((a * a).astype(BF16), wd_ref[cols, :])
    o_ref[0] = x + mod_ref[0, 5:6, :] * acc


def _out_mlp(x, y_pool, y_hy, y_na, mod, g2, w_out, w_up, w_down):
    B, L, _ = x.shape
    tm = TOKEN_TILE
    tok = lambda w: pl.BlockSpec((1, tm, w), lambda b, i: (b, i, 0))
    return pl.pallas_call(
        _out_mlp_kernel,
        grid=(B, L // tm),
        in_specs=[
            tok(D_MODEL), tok(POOL_WIDTH), tok(HY_WIDTH), tok(NA_WIDTH),
            pl.BlockSpec((1, N_MOD, D_MODEL), lambda b, i: (b, 0, 0)),
            _resident((1, D_MODEL)),
            _resident((D_MODEL, D_MODEL)),
            _resident((D_MODEL, D_FF)),
            _resident((D_FF, D_MODEL)),
        ],
        out_specs=tok(D_MODEL),
        out_shape=jax.ShapeDtypeStruct((B, L, D_MODEL), F32),
        compiler_params=_params("parallel", "parallel"),
        name="out_mlp",
    )(x, y_pool, y_hy, y_na, mod, g2, w_out, w_up, w_down)


def _block_diag(blocks):
    g, c, d = blocks.shape
    out = jnp.zeros((g * c, g * d), blocks.dtype)
    for i in range(g):
        out = out.at[i * c:(i + 1) * c, i * d:(i + 1) * d].set(blocks[i])
    return out


def kernel(x_prompt, x_sample, cache_k, cache_v, c, c_ctx, norm1_g, norm2_g, w_mod, b_mod, w_in,
           pool_w, pool_scale, hy_conv_w, hy_conv_b, hy_f1_w, hy_f1_b, hy_f1_freq, hy_f2_w, hy_f2_b,
           hy_f2_freq, hy_f3_w, hy_bias, q_norm_g, k_norm_g, rel_bias, w_out, w_up, w_down):
    n_prompt, seq, _ = x_prompt.shape
    n_dec, dec_seq, _ = x_sample.shape

    conds = jnp.zeros((MOD_ROWS, D_MODEL), F32).at[0].set(c_ctx).at[1:1 + n_dec].set(c)
    mod = _adaln(conds, w_mod, b_mod).reshape(DEPTH, MOD_ROWS, N_MOD, D_MODEL)

    head_mean = _block_diag(jnp.full((NA_HEADS, NA_HEAD_DIM, NA_HEAD_DIM), 1.0 / NA_HEAD_DIM, BF16))
    cache_k = cache_k.reshape(n_dec, DEPTH, -1, NA_WIDTH)
    cache_v = cache_v.reshape(n_dec, DEPTH, -1, NA_WIDTH)

    xp = x_prompt.reshape(1, n_prompt * seq, D_MODEL)
    xs = x_sample
    new_k = new_v = None
    for l in range(DEPTH):
        g1 = norm1_g[l].reshape(1, D_MODEL)
        g2 = norm2_g[l].reshape(1, D_MODEL)
        w_in_l = w_in[l].astype(BF16)
        w_out_l = w_out[l].astype(BF16)
        w_up_l = w_up[l].astype(BF16)
        w_down_l = w_down[l].astype(BF16)
        pool_bd = _block_diag(pool_w[l]).astype(BF16)
        pool_s = pool_scale[l].reshape(1, POOL_WIDTH)
        qg = jnp.tile(q_norm_g[l], NA_HEADS).reshape(1, NA_WIDTH)
        kg = jnp.tile(k_norm_g[l], NA_HEADS).reshape(1, NA_WIDTH)
        f1_w = jnp.zeros((HY_FFN, HY_EMB_PAD), F32).at[:, :HY_EMB].set(hy_f1_w[l].T)
        filt_args = (f1_w, hy_f1_b[l].reshape(HY_FFN, 1), hy_f1_freq[l].reshape(HY_FFN, 1),
                     hy_f2_w[l].T, hy_f2_b[l].reshape(HY_FFN, 1), hy_f2_freq[l].reshape(HY_FFN, 1),
                     hy_f3_w[l].reshape(HY_FFN, 2, 2 * HY_WIDTH).transpose(1, 0, 2))
        conv_b = hy_conv_b[l].reshape(1, 3 * HY_WIDTH)

        mod_ctx = mod[l, 0:1]
        u_pool, u_hy, q, new_k, new_v = _in_proj(xp, mod_ctx, g1, w_in_l, head_mean, qg, kg,
                                                 cache=(l, seq, new_k, new_v))
        per_seq = lambda a: a.reshape(n_prompt, seq, a.shape[-1])
        spec, nyq = _hyena_spectra(*_hyena_filters(seq, *filt_args))
        y_pool = _pool_mixer(per_seq(u_pool), pool_bd, pool_s, CTX_PER_STEP)
        y_hy = _hyena_mixer(per_seq(u_hy), hy_conv_w[l], conv_b, hy_bias[l], spec, nyq, CTX_PER_STEP)
        y_na = _context_attention(per_seq(q), new_k, new_v, l, CTX_PER_STEP)
        flat = lambda a: a.reshape(1, n_prompt * seq, a.shape[-1])
        xp = _out_mlp(xp, flat(y_pool), flat(y_hy), flat(y_na), mod_ctx, g2, w_out_l, w_up_l, w_down_l)

        mod_lat = mod[l, 1:1 + n_dec]
        u_pool, u_hy, q, k, v = _in_proj(xs, mod_lat, g1, w_in_l, head_mean, qg, kg)
        spec, nyq = _hyena_spectra(*_hyena_filters(dec_seq, *filt_args))
        y_pool = _pool_mixer(u_pool, pool_bd, pool_s, 1)
        y_hy = _hyena_mixer(u_hy, hy_conv_w[l], conv_b, hy_bias[l], spec, nyq, 1)
        y_na = _neighbourhood_attention(q, k, v, cache_k, cache_v, l, _bias_table(rel_bias[l]))
        xs = _out_mlp(xs, y_pool, y_hy, y_na, mod_lat, g2, w_out_l, w_up_l, w_down_l)

    shape_kv = (n_prompt, DEPTH, seq, NA_HEADS, NA_HEAD_DIM)
    return (xp.reshape(n_prompt, seq, D_MODEL), xs, new_k.reshape(shape_kv), new_v.reshape(shape_kv))
```

```python
import functools
import math

import numpy as np
import jax
import jax.numpy as jnp
from jax import lax
from jax.experimental import pallas as pl
from jax.experimental.pallas import tpu as pltpu

F32 = jnp.float32
BF16 = jnp.bfloat16

D_MODEL = 1024
DEPTH = 2
GRID_W = 64
POOL_WIDTH = 256
POOL_GROUPS = 4
POOL_GROUP_DIM = 64
HY_WIDTH = 256
HY_BANDS = 16
HY_EMB = 1 + 2 * HY_BANDS
HY_EMB_PAD = 128
HY_FFN = 64
HY_TARGET = 1e-2
HY_FAST = 0.3
HY_SLOW = 1.5
NA_HEAD_DIM = 64
NA_WIDTH = 512
NA_HEADS = 8
NA_KH = 8
NA_KW = 16
N_MOD = 6
D_FF = 4096
NORM_EPS = 1e-6
ATT_SCALE = NA_HEAD_DIM ** -0.5

C_POOL = 0
C_HY = POOL_WIDTH
C_Q = C_HY + 3 * HY_WIDTH
C_K = C_Q + NA_WIDTH
C_V = C_K + NA_WIDTH
IN_WIDTH = C_V + NA_WIDTH

DFT_BLOCK = 256
DFT_N = 2 * DFT_BLOCK
MASK_VALUE = -1e30
VMEM_LIMIT = 56 * 1024 * 1024
TOKEN_TILE = 512
MOD_ROWS = 8
CTX_PER_STEP = 4


def _params(*sem):
    return pltpu.CompilerParams(dimension_semantics=sem, vmem_limit_bytes=VMEM_LIMIT)


def _resident(shape):
    return pl.BlockSpec(shape, lambda *_: (0,) * len(shape), pipeline_mode=pl.Buffered(1))


def _layer(shape, l):
    return pl.BlockSpec((None,) + tuple(shape), lambda *_: (l,) + (0,) * len(shape),
                        pipeline_mode=pl.Buffered(1))


def _mod_spec(l, row0):
    return pl.BlockSpec((None, 1, N_MOD, D_MODEL), lambda b, i: (l, row0 + b, 0, 0))


def _dot(a, b):
    return jnp.dot(a, b, preferred_element_type=F32)


def _dot_nt(a, b):
    return lax.dot_general(a, b, (((1,), (1,)), ((), ())), preferred_element_type=F32)


def _dot_f32(a, b):
    return jnp.dot(a, b, preferred_element_type=F32, precision=lax.Precision.HIGHEST)


def _adaln_kernel(c_ref, w_ref, b_ref, o_ref):
    c = c_ref[...]
    s = c * (1.0 / (1.0 + jnp.exp(-c)))
    o_ref[0] = _dot(s.astype(BF16), w_ref[0].astype(BF16)) + b_ref[0]


def _adaln(conds, w_mod, b_mod):
    tn = 1536
    n = N_MOD * D_MODEL
    return pl.pallas_call(
        _adaln_kernel,
        grid=(DEPTH, n // tn),
        in_specs=[
            pl.BlockSpec((MOD_ROWS, D_MODEL), lambda l, j: (0, 0)),
            pl.BlockSpec((1, D_MODEL, tn), lambda l, j: (l, 0, j)),
            pl.BlockSpec((1, 1, tn), lambda l, j: (l, 0, j)),
        ],
        out_specs=pl.BlockSpec((1, MOD_ROWS, tn), lambda l, j: (l, 0, j)),
        out_shape=jax.ShapeDtypeStruct((DEPTH, MOD_ROWS, n), F32),
        compiler_params=_params("parallel", "parallel"),
        name="adaln",
    )(conds, w_mod, b_mod.reshape(DEPTH, 1, n))


def _modulated_norm(x, g, shift, scale):
    ms = jnp.mean(x * x, axis=-1, keepdims=True)
    return (x * lax.rsqrt(ms + NORM_EPS) * g) * (1.0 + scale) + shift


def _inproj_kernel(x_ref, mod_ref, g_ref, w_ref, hm_ref, qg_ref, kg_ref, *rest):
    up_ref, uh_ref, q_ref, k_ref, v_ref = rest[-5:]
    h = _modulated_norm(x_ref[0], g_ref[...], mod_ref[0, 0:1, :], mod_ref[0, 1:2, :]).astype(BF16)
    up_ref[0] = _dot(h, w_ref[:, C_POOL:C_HY])
    uh_ref[0] = _dot(h, w_ref[:, C_HY:C_Q])

    def head_norm(u, g):
        ms = _dot((u * u).astype(BF16), hm_ref[...])
        return u * lax.rsqrt(ms + NORM_EPS) * g

    q = head_norm(_dot(h, w_ref[:, C_Q:C_K]), qg_ref[...])
    q_ref[0] = (q * ATT_SCALE).astype(q_ref.dtype)
    k = head_norm(_dot(h, w_ref[:, C_K:C_V]), kg_ref[...])
    v = _dot(h, w_ref[:, C_V:IN_WIDTH])
    k_ref[...] = k.astype(k_ref.dtype).reshape(k_ref.shape)
    v_ref[...] = v.astype(v_ref.dtype).reshape(v_ref.shape)


def _in_proj(l, x, mod, mod_row0, g1, w_in, head_mean, qg, kg, cache=None):
    B, L, _ = x.shape
    tm = TOKEN_TILE
    tok = lambda w: pl.BlockSpec((1, tm, w), lambda b, i: (b, i, 0))
    in_specs = [
        tok(D_MODEL),
        _mod_spec(l, mod_row0),
        _layer((1, D_MODEL), l),
        _layer((D_MODEL, IN_WIDTH), l),
        _resident((NA_WIDTH, NA_WIDTH)),
        _layer((1, NA_WIDTH), l),
        _layer((1, NA_WIDTH), l),
    ]
    args = [x, mod, g1, w_in, head_mean, qg, kg]
    aliases = {}
    if cache is None:
        kv_spec = tok(NA_WIDTH)
        kv_shape = jax.ShapeDtypeStruct((B, L, NA_WIDTH), BF16)
    else:
        seq, k_cache, v_cache = cache
        assert B == 1 and tm % seq == 0
        per_tile = tm // seq
        kv_spec = pl.BlockSpec((per_tile, 1, seq, NA_WIDTH), lambda b, i: (i, l, 0, 0))
        kv_shape = jax.ShapeDtypeStruct((L // seq, DEPTH, seq, NA_WIDTH), F32)
        if k_cache is not None:
            in_specs += [pl.BlockSpec(memory_space=pl.ANY)] * 2
            aliases = {len(args): 3, len(args) + 1: 4}
            args += [k_cache, v_cache]
    return pl.pallas_call(
        _inproj_kernel,
        grid=(B, L // tm),
        in_specs=in_specs,
        out_specs=[tok(POOL_WIDTH), tok(3 * HY_WIDTH), tok(NA_WIDTH), kv_spec, kv_spec],
        out_shape=[
            jax.ShapeDtypeStruct((B, L, POOL_WIDTH), F32),
            jax.ShapeDtypeStruct((B, L, 3 * HY_WIDTH), F32),
            jax.ShapeDtypeStruct((B, L, NA_WIDTH), BF16),
            kv_shape,
            kv_shape,
        ],
        input_output_aliases=aliases,
        compiler_params=_params("parallel", "parallel"),
        name="in_proj",
    )(*args)


POOL_PAD = 8


def _pool_kernel(u_ref, w_ref, s_ref, o_ref, pad_ref):
    nseq, L = u_ref.shape[0], u_ref.shape[1]
    n = L + 2 * POOL_PAD

    def sh(v, s):
        return pltpu.roll(v, s % n, axis=0)

    lane_p = lax.broadcasted_iota(jnp.int32, (n, POOL_WIDTH), 1)
    t = lax.broadcasted_iota(jnp.int32, (L, POOL_WIDTH), 0)
    lane = lax.broadcasted_iota(jnp.int32, (L, POOL_WIDTH), 1)
    half = jnp.where(lane < 64, 1, jnp.where(lane < 128, 2, jnp.where(lane < 192, 4, 8)))
    lo = jnp.maximum(t - half, 0)
    hi = jnp.minimum(t + half - 1, L - 1)
    inv_cnt = 1.0 / (hi - lo + 1).astype(F32)

    for b in range(nseq):
        u = u_ref[b]
        pad_ref[b, 0:POOL_PAD, :] = jnp.zeros((POOL_PAD, POOL_WIDTH), F32)
        pad_ref[b, POOL_PAD + L:n, :] = jnp.zeros((POOL_PAD, POOL_WIDTH), F32)
        pad_ref[b, POOL_PAD:POOL_PAD + L, :] = u
        a = pad_ref[b]
        p2 = a + sh(a, 1)
        p4 = sh(p2, 1) + sh(p2, -1)
        p8 = sh(p4, 2) + sh(p4, -2)
        p16 = sh(p8, 4) + sh(p8, -4)
        win = jnp.where(lane_p < 64, p2, jnp.where(lane_p < 128, p4, jnp.where(lane_p < 192, p8, p16)))
        pooled = win[POOL_PAD:POOL_PAD + L, :] * inv_cnt - u
        o_ref[b] = (_dot(pooled.astype(BF16), w_ref[...]) * s_ref[...]).astype(o_ref.dtype)


def _pool_mixer(l, u_pool, w_bd, scale, per_step):
    B, L, _ = u_pool.shape
    blk = pl.BlockSpec((per_step, L, POOL_WIDTH), lambda b: (b, 0, 0))
    return pl.pallas_call(
        _pool_kernel,
        grid=(B // per_step,),
        in_specs=[blk, _layer((POOL_WIDTH, POOL_WIDTH), l), _layer((1, POOL_WIDTH), l)],
        out_specs=blk,
        out_shape=jax.ShapeDtypeStruct((B, L, POOL_WIDTH), BF16),
        scratch_shapes=[pltpu.VMEM((per_step, L + 2 * POOL_PAD, POOL_WIDTH), F32)],
        compiler_params=_params("parallel"),
        name="pool_mixer",
    )(u_pool, w_bd, scale)


@functools.lru_cache(maxsize=None)
def _hyena_position_consts(L):
    p = np.abs(np.arange(2 * L, dtype=np.float64) - L)
    p[0] = 0.0
    t = p / (L - 1)
    w = 2.0 * math.pi * p / L
    f = np.linspace(1e-4, HY_BANDS - 1, HY_BANDS)
    z = np.zeros((HY_EMB_PAD, 2 * L), np.float64)
    z[0] = t
    z[1:1 + HY_BANDS] = np.cos(f[:, None] * w[None, :])
    z[1 + HY_BANDS:HY_EMB] = -np.sin(f[:, None] * w[None, :])
    deltas = np.abs(np.linspace(math.log(HY_TARGET) / HY_SLOW, math.log(HY_TARGET) / HY_FAST, HY_WIDTH))
    decay = np.exp(-t[:, None] * deltas[None, :])
    return z.astype(np.float32), decay.astype(np.float32)


FILTER_BLOCK = 512


def _filter_kernel(z_ref, dec_ref, w1_ref, b1_ref, f1_ref, w2_ref, b2_ref, f2_ref, w3_ref, o_ref, sum_ref):
    half, j = pl.program_id(0), pl.program_id(1)
    h = jnp.sin(f1_ref[...] * (_dot_f32(w1_ref[...], z_ref[...]) + b1_ref[...]))
    h = jnp.sin(f2_ref[...] * (_dot_f32(w2_ref[...], h) + b2_ref[...]))
    k = lax.dot_general(h, w3_ref[0], (((0,), (0,)), ((), ())), preferred_element_type=F32,
                        precision=lax.Precision.HIGHEST)
    unused = jnp.logical_and(lax.broadcasted_iota(jnp.int32, dec_ref.shape, 0) == 0,
                             jnp.logical_and(half == 0, j == 0))
    for o in range(2):
        ko = k[:, o * HY_WIDTH:(o + 1) * HY_WIDTH] * dec_ref[...]
        part = jnp.sum(jnp.abs(ko), axis=0, keepdims=True)

        @pl.when(j == 0)
        def _():
            sum_ref[0, o] = part

        @pl.when(j > 0)
        def _():
            sum_ref[0, o] = sum_ref[0, o] + part

        o_ref[o] = jnp.where(unused, 0.0, ko)


def _hyena_filters(l, L, w1, b1, f1, w2, b2, f2, w3):
    z, decay = _hyena_position_consts(L)
    fb = min(L, FILTER_BLOCK)
    nblk = L // fb
    return pl.pallas_call(
        _filter_kernel,
        grid=(2, nblk),
        in_specs=[
            pl.BlockSpec((HY_EMB_PAD, fb), lambda hf, j: (0, hf * nblk + j)),
            pl.BlockSpec((fb, HY_WIDTH), lambda hf, j: (hf * nblk + j, 0)),
            _layer((HY_FFN, HY_EMB_PAD), l), _layer((HY_FFN, 1), l), _layer((HY_FFN, 1), l),
            _layer((HY_FFN, HY_FFN), l), _layer((HY_FFN, 1), l), _layer((HY_FFN, 1), l),
            pl.BlockSpec((None, 1, HY_FFN, 2 * HY_WIDTH), lambda hf, j: (l, 1 - hf, 0, 0)),
        ],
        out_specs=[
            pl.BlockSpec((2, fb, HY_WIDTH), lambda hf, j: (0, hf * nblk + j, 0)),
            pl.BlockSpec((1, 2, 1, HY_WIDTH), lambda hf, j: (hf, 0, 0, 0)),
        ],
        out_shape=[
            jax.ShapeDtypeStruct((2, 2 * L, HY_WIDTH), F32),
            jax.ShapeDtypeStruct((2, 2, 1, HY_WIDTH), F32),
        ],
        compiler_params=_params("parallel", "arbitrary"),
        name="hyena_filters",
    )(jnp.asarray(z), jnp.asarray(decay), w1, b1, f1, w2, b2, f2, w3)


@functools.lru_cache(maxsize=None)
def _dft_consts():
    n = np.arange(DFT_BLOCK, dtype=np.float64)
    j = np.arange(DFT_BLOCK, dtype=np.float64)
    ang = 2.0 * math.pi * np.outer(j, n) / DFT_N
    fwd = np.concatenate([np.cos(ang), -np.sin(ang)], axis=0)
    fwd[DFT_BLOCK] = np.cos(math.pi * n)
    wgt = np.where(j == 0, 1.0, 2.0)[None, :]
    inv = np.concatenate([wgt * np.cos(ang.T), -2.0 * np.sin(ang.T)], axis=1) / DFT_N
    inv[:, DFT_BLOCK] = np.cos(math.pi * n) / DFT_N
    sign = np.tile(np.where(np.arange(DFT_BLOCK) % 2 == 0, 1.0, -1.0), 2)[:, None]
    col0 = fwd[:, 0:1].copy()
    return fwd, inv, sign.astype(np.float32), col0.astype(np.float32)


def _split2(a64):
    hi = _np_bf16(a64)
    return hi, _np_bf16(a64 - hi.astype(np.float64))


def _np_bf16(a64):
    return a64.astype(np.float32).astype(BF16)


def _spectra_kernel(a_ref, sum_ref, fh_ref, fl_ref, sign_ref, col0_ref, g_ref, ny_ref, prev_ref):
    @pl.when(pl.program_id(1) == 0)
    def _():
        prev_ref[...] = jnp.zeros(prev_ref.shape, F32)

    inv_norm = 1.0 / (sum_ref[0, 0] + 1e-6)
    row = lax.broadcasted_iota(jnp.int32, (DFT_N, HY_WIDTH), 0)
    for t in range(g_ref.shape[1]):
        a = a_ref[0, t * DFT_BLOCK:(t + 1) * DFT_BLOCK, :] * inv_norm
        a_hi = a.astype(BF16)
        a_lo = (a - a_hi.astype(F32)).astype(BF16)
        fa = _dot(fh_ref[...], a_hi) + (_dot(fh_ref[...], a_lo) + _dot(fl_ref[...], a_hi))
        g = fa + sign_ref[...] * prev_ref[...]
        ny_ref[0, t] = g[DFT_BLOCK:DFT_BLOCK + 1, :]
        g_ref[0, t] = jnp.where(row == DFT_BLOCK, 0.0, g)
        prev_ref[...] = fa - col0_ref[...] * a[0:1, :]


SPECTRA_PER_STEP = 4


def _hyena_spectra(gg, sums):
    nb = gg.shape[1] // (2 * DFT_BLOCK)
    per = min(nb, SPECTRA_PER_STEP)
    steps_per_half = nb // per
    fwd, _, sign, col0 = _dft_consts()
    fh, fl = _split2(fwd)
    return pl.pallas_call(
        _spectra_kernel,
        grid=(2, 2 * steps_per_half),
        in_specs=[
            pl.BlockSpec((1, per * DFT_BLOCK, HY_WIDTH), lambda o, s: (o, s, 0)),
            pl.BlockSpec((1, 1, 1, HY_WIDTH), lambda o, s: (s // steps_per_half, o, 0, 0)),
            _resident((DFT_N, DFT_BLOCK)), _resident((DFT_N, DFT_BLOCK)),
            _resident((DFT_N, 1)), _resident((DFT_N, 1)),
        ],
        out_specs=[
            pl.BlockSpec((1, per, DFT_N, HY_WIDTH), lambda o, s: (o, s, 0, 0)),
            pl.BlockSpec((1, per, 1, HY_WIDTH), lambda o, s: (o, s, 0, 0)),
        ],
        out_shape=[
            jax.ShapeDtypeStruct((2, 2 * nb, DFT_N, HY_WIDTH), F32),
            jax.ShapeDtypeStruct((2, 2 * nb, 1, HY_WIDTH), F32),
        ],
        scratch_shapes=[pltpu.VMEM((DFT_N, HY_WIDTH), F32)],
        compiler_params=_params("parallel", "arbitrary"),
        name="hyena_spectra",
    )(gg, sums, fh, fl, jnp.asarray(sign), jnp.asarray(col0))


SPEC_ROWS = 32
NY_ROWS = 16


def _hyena_kernel(u_ref, cw_ref, cb_ref, hb_ref, g_ref, ny_ref, fwd_ref, inv_ref, o_ref,
                  z_ref, gate_ref, zf_ref, y_ref):
    nseq, L = u_ref.shape[0], u_ref.shape[1]
    nb = L // DFT_BLOCK
    t = lax.broadcasted_iota(jnp.int32, (L, HY_WIDTH), 0)
    ny_row = lax.broadcasted_iota(jnp.int32, (NY_ROWS, HY_WIDTH), 0) == 0

    for b in range(nseq):
        for part in range(3):
            cols = slice(part * HY_WIDTH, (part + 1) * HY_WIDTH)
            u = u_ref[b, :, cols]
            prev = jnp.where(t == 0, 0.0, pltpu.roll(u, 1, axis=0))
            nxt = jnp.where(t == L - 1, 0.0, pltpu.roll(u, L - 1, axis=0))
            y = cb_ref[:, cols] + prev * cw_ref[0:1, cols] + u * cw_ref[1:2, cols] + nxt * cw_ref[2:3, cols]
            if part < 2:
                gate_ref[part] = y
            else:
                z_ref[...] = y

        for o in range(2):
            for j in range(nb):
                blk = slice(j * DFT_BLOCK, (j + 1) * DFT_BLOCK)
                zf_ref[j] = _dot(fwd_ref[...], z_ref[blk, :].astype(BF16))
            bias = hb_ref[o]

            def out_block(i, carry, o=o, bias=bias):
                def spec_rows(c, carry2):
                    r0 = pl.multiple_of(c * SPEC_ROWS, SPEC_ROWS)
                    re = pl.ds(r0, SPEC_ROWS)
                    im = pl.ds(DFT_BLOCK + r0, SPEC_ROWS)
                    top = jnp.zeros((SPEC_ROWS, HY_WIDTH), F32)
                    bot = jnp.zeros((SPEC_ROWS, HY_WIDTH), F32)
                    for j in range(nb):
                        d = i - j + nb
                        gr, gi = g_ref[o, d, re, :], g_ref[o, d, im, :]
                        zr, zi = zf_ref[j, re, :], zf_ref[j, im, :]
                        top = top + (gr * zr - gi * zi)
                        bot = bot + (gr * zi + gi * zr)
                    y_ref[re, :] = top.astype(BF16)
                    y_ref[im, :] = bot.astype(BF16)
                    return carry2

                lax.fori_loop(0, DFT_BLOCK // SPEC_ROWS, spec_rows, 0, unroll=(nb == 1))
                ny = jnp.zeros((1, HY_WIDTH), F32)
                for j in range(nb):
                    ny = ny + ny_ref[o, i - j + nb] * zf_ref[j, DFT_BLOCK:DFT_BLOCK + 1, :]
                head = y_ref[DFT_BLOCK:DFT_BLOCK + NY_ROWS, :].astype(F32)
                y_ref[DFT_BLOCK:DFT_BLOCK + NY_ROWS, :] = jnp.where(ny_row, ny, head).astype(BF16)
                conv = _dot(inv_ref[...], y_ref[...])
                rows = pl.ds(pl.multiple_of(i * DFT_BLOCK, DFT_BLOCK), DFT_BLOCK)
                z_ref[rows, :] = gate_ref[o, rows, :] * (conv + bias * z_ref[rows, :])
                return carry

            lax.fori_loop(0, nb, out_block, 0, unroll=(nb == 1))
        o_ref[b] = z_ref[...].astype(o_ref.dtype)


def _hyena_mixer(l, u_hy, conv_w, conv_b, hy_bias, spec, nyq, per_step):
    B, L, _ = u_hy.shape
    nd = spec.shape[1]
    fwd, inv, _, _ = _dft_consts()
    return pl.pallas_call(
        _hyena_kernel,
        grid=(B // per_step,),
        in_specs=[
            pl.BlockSpec((per_step, L, 3 * HY_WIDTH), lambda b: (b, 0, 0)),
            _layer((3, 3 * HY_WIDTH), l),
            _layer((1, 3 * HY_WIDTH), l),
            _layer((2, 1, HY_WIDTH), l),
            _resident((2, nd, DFT_N, HY_WIDTH)),
            _resident((2, nd, 1, HY_WIDTH)),
            _resident((DFT_N, DFT_BLOCK)),
            _resident((DFT_BLOCK, DFT_N)),
        ],
        out_specs=pl.BlockSpec((per_step, L, HY_WIDTH), lambda b: (b, 0, 0)),
        out_shape=jax.ShapeDtypeStruct((B, L, HY_WIDTH), BF16),
        scratch_shapes=[
            pltpu.VMEM((L, HY_WIDTH), F32),
            pltpu.VMEM((2, L, HY_WIDTH), F32),
            pltpu.VMEM((L // DFT_BLOCK, DFT_N, HY_WIDTH), F32),
            pltpu.VMEM((DFT_N, HY_WIDTH), BF16),
        ],
        compiler_params=_params("parallel"),
        name="hyena_mixer",
    )(u_hy, conv_w, conv_b, hy_bias, spec, nyq,
      jnp.asarray(fwd, F32).astype(BF16), jnp.asarray(inv, F32).astype(BF16))


HEAD_PAIR = 2 * NA_HEAD_DIM
SOFTMAX_ROWS = 64
PV_CHUNK = 256


def _paired_heads(q, keys, values, bias, s_ref, p_ref):
    M = q.shape[0]
    lane = lax.broadcasted_iota(jnp.int32, (M, HEAD_PAIR), 1)

    def own_lanes(ln, e):
        return (ln < NA_HEAD_DIM) if e == 0 else (ln >= NA_HEAD_DIM)

    def scores(h):
        cols = slice((h // 2) * HEAD_PAIR, (h // 2 + 1) * HEAD_PAIR)
        qh = jnp.where(own_lanes(lane, h % 2), q[:, cols], jnp.zeros((M, HEAD_PAIR), BF16))
        off, m = 0, None
        for i, k in enumerate(keys):
            s = _dot_nt(qh, k[:, cols])
            if bias is not None and i == 0:
                s = s + bias(h)
            s_ref[h % 2, :, off:off + k.shape[0]] = s
            mi = jnp.max(s, axis=-1, keepdims=True)
            m = mi if m is None else jnp.maximum(m, mi)
            off += k.shape[0]
        return m

    def attend(h, m):
        cols = slice((h // 2) * HEAD_PAIR, (h // 2 + 1) * HEAD_PAIR)
        acc, off = None, 0
        for v in values:
            vl = lax.broadcasted_iota(jnp.int32, (v.shape[0], HEAD_PAIR), 1)
            vh = jnp.where(own_lanes(vl, h % 2), v[:, cols], jnp.ones((v.shape[0], HEAD_PAIR), BF16))
            for c0 in range(0, v.shape[0], PV_CHUNK):
                kc = slice(off + c0, off + c0 + PV_CHUNK)
                for r0 in range(0, M, SOFTMAX_ROWS):
                    rows = slice(r0, r0 + SOFTMAX_ROWS)
                    p_ref[h % 2, rows, kc] = jnp.exp(s_ref[h % 2, rows, kc] - m[rows]).astype(BF16)
                part = _dot(p_ref[h % 2, :, kc], vh[c0:c0 + PV_CHUNK])
                acc = part if acc is None else acc + part
            off += v.shape[0]
        return acc / pltpu.roll(acc, NA_HEAD_DIM, axis=1)

    outs, halves = [], []
    m_next = scores(0)
    for h in range(NA_HEADS):
        m = m_next
        if h + 1 < NA_HEADS:
            m_next = scores(h + 1)
        halves.append(attend(h, m))
        if h % 2 == 1:
            outs.append(jnp.where(lane < NA_HEAD_DIM, halves[0], halves[1]))
            halves = []
    return jnp.concatenate(outs, axis=-1)


def _ctx_attn_kernel(q_ref, k_ref, v_ref, o_ref, s_ref, p_ref):
    nseq, L = q_ref.shape[0], q_ref.shape[1]
    even = lax.broadcasted_iota(jnp.int32, (L, HEAD_PAIR), 1) < NA_HEAD_DIM
    zero = jnp.zeros((L, HEAD_PAIR), BF16)
    units = [(b, p) for b in range(nseq) for p in range(NA_HEADS // 2)]

    def scores(n):
        b, p = units[n]
        cols = slice(p * HEAD_PAIR, (p + 1) * HEAD_PAIR)
        qp = q_ref[b, :, cols]
        q2 = jnp.concatenate([jnp.where(even, qp, zero), jnp.where(even, zero, qp)], axis=0)
        s_ref[n % 2] = _dot_nt(q2, k_ref[b, 0, :, cols].astype(BF16))

    def attend(n):
        b, p = units[n]
        cols = slice(p * HEAD_PAIR, (p + 1) * HEAD_PAIR)
        for r0 in range(0, 2 * L, SOFTMAX_ROWS):
            rows = slice(r0, r0 + SOFTMAX_ROWS)
            s = s_ref[n % 2, rows, :]
            e = jnp.exp(s - jnp.max(s, axis=-1, keepdims=True))
            p_ref[n % 2, rows, :] = (e * (1.0 / jnp.sum(e, axis=-1, keepdims=True))).astype(BF16)
        acc = _dot(p_ref[n % 2], v_ref[b, 0, :, cols].astype(BF16))
        o_ref[b, :, cols] = jnp.where(even, acc[:L], acc[L:]).astype(o_ref.dtype)

    scores(0)
    for n in range(len(units)):
        if n + 1 < len(units):
            scores(n + 1)
        attend(n)


def _context_attention(q, k_cache, v_cache, layer, per_step):
    B, L, _ = q.shape
    spec = pl.BlockSpec((per_step, L, NA_WIDTH), lambda b: (b, 0, 0))
    kv = pl.BlockSpec((per_step, 1, L, NA_WIDTH), lambda b: (b, layer, 0, 0))
    return pl.pallas_call(
        _ctx_attn_kernel,
        grid=(B // per_step,),
        in_specs=[spec, kv, kv],
        out_specs=spec,
        out_shape=jax.ShapeDtypeStruct((B, L, NA_WIDTH), BF16),
        scratch_shapes=[pltpu.VMEM((2, 2 * L, L), F32), pltpu.VMEM((2, 2 * L, L), BF16)],
        compiler_params=_params("parallel"),
        name="context_attention",
    )(q, k_cache, v_cache)


N_DR = 2 * NA_KH - 1
N_DC = 2 * NA_KW - 1
NA_RB = 4
NA_WIN = NA_RB + NA_KH
WIN_KEYS = NA_WIN * GRID_W


def _bias_kernel(rb_ref, o_ref):
    h = pl.program_id(0)
    qc = lax.broadcasted_iota(jnp.int32, (GRID_W, GRID_W), 0)
    kc = lax.broadcasted_iota(jnp.int32, (GRID_W, GRID_W), 1)
    dc = jnp.clip(kc - qc, -(NA_KW - 1), NA_KW - 1) + (NA_KW - 1)
    col_start = jnp.clip(qc - NA_KW // 2, 0, GRID_W - NA_KW)
    visible = (kc >= col_start) & (kc < col_start + NA_KW)
    blocks = []
    for dr in range(N_DR):
        acc = jnp.zeros((GRID_W, GRID_W), F32)
        for d in range(N_DC):
            acc = jnp.where(dc == d, rb_ref[h * N_DR + dr, d], acc)
        blocks.append(jnp.where(visible, acc, MASK_VALUE))
    hidden = jnp.full((GRID_W, GRID_W), MASK_VALUE, F32)
    for variant, (first, base) in enumerate(((0, NA_KH - 1), (None, NA_KH // 2 - 1), (NA_RB, -1))):
        for a in range(NA_RB):
            lo = a if first is None else first
            row = [blocks[i - a + base] if lo <= i < lo + NA_KH else hidden for i in range(NA_WIN)]
            o_ref[variant, 0, a * GRID_W:(a + 1) * GRID_W, :] = jnp.concatenate(row, axis=-1)


def _bias_table(rel_bias_l):
    return pl.pallas_call(
        _bias_kernel,
        grid=(NA_HEADS,),
        in_specs=[pl.BlockSpec(memory_space=pltpu.SMEM)],
        out_specs=pl.BlockSpec((3, 1, NA_RB * GRID_W, WIN_KEYS), lambda h: (0, h, 0, 0)),
        out_shape=jax.ShapeDtypeStruct((3, NA_HEADS, NA_RB * GRID_W, WIN_KEYS), F32),
        compiler_params=_params("parallel"),
        name="na_bias_table",
    )(rel_bias_l.reshape(NA_HEADS * N_DR, N_DC))


def _na_kernel(q_ref, k_ref, v_ref, kc_ref, vc_ref, tab_ref, o_ref, s_ref, p_ref):
    rows = k_ref.shape[1] // GRID_W
    rb = pl.program_id(1)
    first_row = jnp.clip(rb * NA_RB - NA_KH // 2, 0, rows - NA_WIN)
    start = pl.multiple_of(first_row * GRID_W, NA_RB * GRID_W)
    keys = [k_ref[0, pl.ds(start, WIN_KEYS), :], kc_ref[0, 0].astype(BF16)]
    values = [v_ref[0, pl.ds(start, WIN_KEYS), :], vc_ref[0, 0].astype(BF16)]
    out = _paired_heads(q_ref[0], keys, values, lambda h: tab_ref[0, h], s_ref, p_ref)
    o_ref[0] = out.astype(o_ref.dtype)


def _neighbourhood_attention(q, k, v, cache_k, cache_v, layer, table):
    B, L, _ = q.shape
    rows = L // GRID_W
    assert rows % NA_RB == 0 and rows >= NA_WIN + NA_RB
    nrb = rows // NA_RB
    Lc = cache_k.shape[2]
    seq = pl.BlockSpec((1, L, NA_WIDTH), lambda b, r: (b, 0, 0))
    blk = pl.BlockSpec((1, NA_RB * GRID_W, NA_WIDTH), lambda b, r: (b, r, 0))
    ctx = pl.BlockSpec((1, 1, Lc, NA_WIDTH), lambda b, r: (b, layer, 0, 0))
    placement = lambda r: jnp.where(r == 0, 0, jnp.where(r == nrb - 1, 2, 1))
    tab = pl.BlockSpec((1, NA_HEADS, NA_RB * GRID_W, WIN_KEYS), lambda b, r: (placement(r), 0, 0, 0))
    return pl.pallas_call(
        _na_kernel,
        grid=(B, nrb),
        in_specs=[blk, seq, seq, ctx, ctx, tab],
        out_specs=blk,
        out_shape=jax.ShapeDtypeStruct((B, L, NA_WIDTH), BF16),
        scratch_shapes=[pltpu.VMEM((2, NA_RB * GRID_W, WIN_KEYS + Lc), F32),
                        pltpu.VMEM((2, NA_RB * GRID_W, WIN_KEYS + Lc), BF16)],
        compiler_params=_params("parallel", "arbitrary"),
        name="neighbourhood_attention",
    )(q, k, v, cache_k, cache_v, table)


FF_CHUNK = 1024


def _out_mlp_kernel(x_ref, yp_ref, yh_ref, ya_ref, mod_ref, g_ref, wo_ref, wu_ref, wd_ref, o_ref):
    y = (_dot(yp_ref[0], wo_ref[0:POOL_WIDTH, :])
         + _dot(yh_ref[0], wo_ref[POOL_WIDTH:POOL_WIDTH + HY_WIDTH, :])
         + _dot(ya_ref[0], wo_ref[POOL_WIDTH + HY_WIDTH:D_MODEL, :]))
    x = x_ref[0] + mod_ref[0, 2:3, :] * y
    h = _modulated_norm(x, g_ref[...], mod_ref[0, 3:4, :], mod_ref[0, 4:5, :]).astype(BF16)
    acc = jnp.zeros(x.shape, F32)
    for c in range(D_FF // FF_CHUNK):
        cols = slice(c * FF_CHUNK, (c + 1) * FF_CHUNK)
        a = jnp.maximum(_dot(h, wu_ref[:, cols]), 0.0)
        acc = acc + _dot((a * a).astype(BF16), wd_ref[cols, :])
    o_ref[0] = x + mod_ref[0, 5:6, :] * acc


def _out_mlp(l, x, y_pool, y_hy, y_na, mod, mod_row0, g2, w_out, w_up, w_down):
    B, L, _ = x.shape
    tm = TOKEN_TILE
    tok = lambda w: pl.BlockSpec((1, tm, w), lambda b, i: (b, i, 0))
    return pl.pallas_call(
        _out_mlp_kernel,
        grid=(B, L // tm),
        in_specs=[
            tok(D_MODEL), tok(POOL_WIDTH), tok(HY_WIDTH), tok(NA_WIDTH),
            _mod_spec(l, mod_row0),
            _layer((1, D_MODEL), l),
            _layer((D_MODEL, D_MODEL), l),
            _layer((D_MODEL, D_FF), l),
            _layer((D_FF, D_MODEL), l),
        ],
        out_specs=tok(D_MODEL),
        out_shape=jax.ShapeDtypeStruct((B, L, D_MODEL), F32),
        compiler_params=_params("parallel", "parallel"),
        name="out_mlp",
    )(x, y_pool, y_hy, y_na, mod, g2, w_out, w_up, w_down)


def _block_diag(blocks):
    *lead, g, c, d = blocks.shape
    eye = jnp.eye(g, dtype=blocks.dtype)
    out = blocks[..., :, :, None, :] * eye[:, None, :, None]
    return out.reshape(*lead, g * c, g * d)


def kernel(x_prompt, x_sample, cache_k, cache_v, c, c_ctx, norm1_g, norm2_g, w_mod, b_mod, w_in,
           pool_w, pool_scale, hy_conv_w, hy_conv_b, hy_f1_w, hy_f1_b, hy_f1_freq, hy_f2_w, hy_f2_b,
           hy_f2_freq, hy_f3_w, hy_bias, q_norm_g, k_norm_g, rel_bias, w_out, w_up, w_down):
    n_prompt, seq, _ = x_prompt.shape
    n_dec, dec_seq, _ = x_sample.shape

    conds = jnp.zeros((MOD_ROWS, D_MODEL), F32).at[0].set(c_ctx).at[1:1 + n_dec].set(c)
    mod = _adaln(conds, w_mod, b_mod).reshape(DEPTH, MOD_ROWS, N_MOD, D_MODEL)

    head_mean = _block_diag(jnp.full((NA_HEADS, NA_HEAD_DIM, NA_HEAD_DIM), 1.0 / NA_HEAD_DIM, BF16))
    cache_k = cache_k.reshape(n_dec, DEPTH, -1, NA_WIDTH)
    cache_v = cache_v.reshape(n_dec, DEPTH, -1, NA_WIDTH)

    g1 = norm1_g.reshape(DEPTH, 1, D_MODEL)
    g2 = norm2_g.reshape(DEPTH, 1, D_MODEL)
    w_in_b, w_out_b, w_up_b, w_down_b = (w.astype(BF16) for w in (w_in, w_out, w_up, w_down))
    pool_bd = _block_diag(pool_w).astype(BF16)
    pool_s = pool_scale.reshape(DEPTH, 1, POOL_WIDTH)
    qg = jnp.tile(q_norm_g, (1, NA_HEADS)).reshape(DEPTH, 1, NA_WIDTH)
    kg = jnp.tile(k_norm_g, (1, NA_HEADS)).reshape(DEPTH, 1, NA_WIDTH)
    col = lambda a: a.reshape(DEPTH, HY_FFN, 1)
    f1_w = jnp.zeros((DEPTH, HY_FFN, HY_EMB_PAD), F32).at[:, :, :HY_EMB].set(hy_f1_w.transpose(0, 2, 1))
    f3_w = hy_f3_w.reshape(DEPTH, HY_FFN, 2, 2 * HY_WIDTH).transpose(0, 2, 1, 3)
    filt_args = (f1_w, col(hy_f1_b), col(hy_f1_freq), hy_f2_w.transpose(0, 2, 1), col(hy_f2_b),
                 col(hy_f2_freq), f3_w)
    conv_b = hy_conv_b.reshape(DEPTH, 1, 3 * HY_WIDTH)
    hy_b = hy_bias.reshape(DEPTH, 2, 1, HY_WIDTH)

    xp = x_prompt.reshape(1, n_prompt * seq, D_MODEL)
    xs = x_sample
    new_k = new_v = None
    per_seq = lambda a: a.reshape(n_prompt, seq, a.shape[-1])
    flat = lambda a: a.reshape(1, n_prompt * seq, a.shape[-1])
    for l in range(DEPTH):
        u_pool, u_hy, q, new_k, new_v = _in_proj(l, xp, mod, 0, g1, w_in_b, head_mean, qg, kg,
                                                 cache=(seq, new_k, new_v))
        spec, nyq = _hyena_spectra(*_hyena_filters(l, seq, *filt_args))
        y_pool = _pool_mixer(l, per_seq(u_pool), pool_bd, pool_s, CTX_PER_STEP)
        y_hy = _hyena_mixer(l, per_seq(u_hy), hy_conv_w, conv_b, hy_b, spec, nyq, CTX_PER_STEP)
        y_na = _context_attention(per_seq(q), new_k, new_v, l, CTX_PER_STEP)
        xp = _out_mlp(l, xp, flat(y_pool), flat(y_hy), flat(y_na), mod, 0, g2, w_out_b, w_up_b, w_down_b)

        u_pool, u_hy, q, k, v = _in_proj(l, xs, mod, 1, g1, w_in_b, head_mean, qg, kg)
        spec, nyq = _hyena_spectra(*_hyena_filters(l, dec_seq, *filt_args))
        y_pool = _pool_mixer(l, u_pool, pool_bd, pool_s, 1)
        y_hy = _hyena_mixer(l, u_hy, hy_conv_w, conv_b, hy_b, spec, nyq, 1)
        y_na = _neighbourhood_attention(q, k, v, cache_k, cache_v, l, _bias_table(rel_bias[l]))
        xs = _out_mlp(l, xs, y_pool, y_hy, y_na, mod, 1, g2, w_out_b, w_up_b, w_down_b)

    shape_kv = (n_prompt, DEPTH, seq, NA_HEADS, NA_HEAD_DIM)
    return (xp.reshape(n_prompt, seq, D_MODEL), xs, new_k.reshape(shape_kv), new_v.reshape(shape_kv))
```

```python
import functools
import math

import numpy as np
import jax
import jax.numpy as jnp
from jax import lax
from jax.experimental import pallas as pl
from jax.experimental.pallas import tpu as pltpu

F32 = jnp.float32
BF16 = jnp.bfloat16

D_MODEL = 1024
DEPTH = 2
GRID_W = 64
POOL_WIDTH = 256
POOL_GROUPS = 4
POOL_GROUP_DIM = 64
HY_WIDTH = 256
HY_BANDS = 16
HY_EMB = 1 + 2 * HY_BANDS
HY_EMB_PAD = 128
HY_FFN = 64
HY_TARGET = 1e-2
HY_FAST = 0.3
HY_SLOW = 1.5
NA_HEAD_DIM = 64
NA_WIDTH = 512
NA_HEADS = 8
NA_KH = 8
NA_KW = 16
N_MOD = 6
D_FF = 4096
NORM_EPS = 1e-6
ATT_SCALE = NA_HEAD_DIM ** -0.5

C_POOL = 0
C_HY = POOL_WIDTH
C_Q = C_HY + 3 * HY_WIDTH
C_K = C_Q + NA_WIDTH
C_V = C_K + NA_WIDTH
IN_WIDTH = C_V + NA_WIDTH

DFT_BLOCK = 256
DFT_N = 2 * DFT_BLOCK
MASK_VALUE = -1e30
VMEM_LIMIT = 56 * 1024 * 1024
TOKEN_TILE = 512
MOD_ROWS = 8
CTX_PER_STEP = 4


def _params(*sem):
    return pltpu.CompilerParams(dimension_semantics=sem, vmem_limit_bytes=VMEM_LIMIT)


def _resident(shape):
    return pl.BlockSpec(shape, lambda *_: (0,) * len(shape), pipeline_mode=pl.Buffered(1))


def _layer(shape, l):
    return pl.BlockSpec((None,) + tuple(shape), lambda *_: (l,) + (0,) * len(shape),
                        pipeline_mode=pl.Buffered(1))


def _mod_spec(l, row0):
    return pl.BlockSpec((None, 1, N_MOD, D_MODEL), lambda b, i: (l, row0 + b, 0, 0))


def _dot(a, b):
    return jnp.dot(a, b, preferred_element_type=F32)


def _dot_nt(a, b):
    return lax.dot_general(a, b, (((1,), (1,)), ((), ())), preferred_element_type=F32)


def _dot_f32(a, b):
    return jnp.dot(a, b, preferred_element_type=F32, precision=lax.Precision.HIGHEST)


def _adaln_kernel(c_ref, w_ref, b_ref, o_ref):
    c = c_ref[...]
    s = c * (1.0 / (1.0 + jnp.exp(-c)))
    o_ref[0] = _dot(s.astype(BF16), w_ref[0].astype(BF16)) + b_ref[0]


def _adaln(conds, w_mod, b_mod):
    tn = 1536
    n = N_MOD * D_MODEL
    return pl.pallas_call(
        _adaln_kernel,
        grid=(DEPTH, n // tn),
        in_specs=[
            pl.BlockSpec((MOD_ROWS, D_MODEL), lambda l, j: (0, 0)),
            pl.BlockSpec((1, D_MODEL, tn), lambda l, j: (l, 0, j)),
            pl.BlockSpec((1, 1, tn), lambda l, j: (l, 0, j)),
        ],
        out_specs=pl.BlockSpec((1, MOD_ROWS, tn), lambda l, j: (l, 0, j)),
        out_shape=jax.ShapeDtypeStruct((DEPTH, MOD_ROWS, n), F32),
        compiler_params=_params("parallel", "parallel"),
        name="adaln",
    )(conds, w_mod, b_mod.reshape(DEPTH, 1, n))


def _modulated_norm(x, g, shift, scale):
    ms = jnp.mean(x * x, axis=-1, keepdims=True)
    return (x * lax.rsqrt(ms + NORM_EPS) * g) * (1.0 + scale) + shift


def _inproj_kernel(x_ref, mod_ref, g_ref, w_ref, qg_ref, kg_ref, *rest):
    up_ref, uh_ref, q_ref, k_ref, v_ref = rest[-5:]
    h = _modulated_norm(x_ref[0], g_ref[...], mod_ref[0, 0:1, :], mod_ref[0, 1:2, :]).astype(BF16)
    up_ref[0] = _dot(h, w_ref[:, C_POOL:C_HY])
    uh_ref[0] = _dot(h, w_ref[:, C_HY:C_Q])

    even = lax.broadcasted_iota(jnp.int32, (x_ref.shape[1], HEAD_PAIR), 1) < NA_HEAD_DIM

    def head_norm(u, g):
        outs = []
        for p in range(NA_HEADS // 2):
            cols = slice(p * HEAD_PAIR, (p + 1) * HEAD_PAIR)
            up = u[:, cols]
            sq = up * up
            ms_even = jnp.sum(jnp.where(even, sq, 0.0), axis=-1, keepdims=True) * (1.0 / NA_HEAD_DIM)
            ms_odd = jnp.sum(jnp.where(even, 0.0, sq), axis=-1, keepdims=True) * (1.0 / NA_HEAD_DIM)
            r = jnp.where(even, lax.rsqrt(ms_even + NORM_EPS), lax.rsqrt(ms_odd + NORM_EPS))
            outs.append(up * r * g[:, cols])
        return jnp.concatenate(outs, axis=-1)

    q = head_norm(_dot(h, w_ref[:, C_Q:C_K]), qg_ref[...])
    q_ref[0] = (q * ATT_SCALE).astype(q_ref.dtype)
    k = head_norm(_dot(h, w_ref[:, C_K:C_V]), kg_ref[...])
    v = _dot(h, w_ref[:, C_V:IN_WIDTH])
    k_ref[...] = k.astype(k_ref.dtype).reshape(k_ref.shape)
    v_ref[...] = v.astype(v_ref.dtype).reshape(v_ref.shape)


def _in_proj(l, x, mod, mod_row0, g1, w_in, qg, kg, cache=None):
    B, L, _ = x.shape
    tm = TOKEN_TILE
    tok = lambda w: pl.BlockSpec((1, tm, w), lambda b, i: (b, i, 0))
    in_specs = [
        tok(D_MODEL),
        _mod_spec(l, mod_row0),
        _layer((1, D_MODEL), l),
        _layer((D_MODEL, IN_WIDTH), l),
        _layer((1, NA_WIDTH), l),
        _layer((1, NA_WIDTH), l),
    ]
    args = [x, mod, g1, w_in, qg, kg]
    aliases = {}
    if cache is None:
        kv_spec = tok(NA_WIDTH)
        kv_shape = jax.ShapeDtypeStruct((B, L, NA_WIDTH), BF16)
    else:
        seq, k_cache, v_cache = cache
        assert B == 1 and tm % seq == 0
        per_tile = tm // seq
        kv_spec = pl.BlockSpec((per_tile, 1, seq, NA_WIDTH), lambda b, i: (i, l, 0, 0))
        kv_shape = jax.ShapeDtypeStruct((L // seq, DEPTH, seq, NA_WIDTH), F32)
        if k_cache is not None:
            in_specs += [pl.BlockSpec(memory_space=pl.ANY)] * 2
            aliases = {len(args): 3, len(args) + 1: 4}
            args += [k_cache, v_cache]
    return pl.pallas_call(
        _inproj_kernel,
        grid=(B, L // tm),
        in_specs=in_specs,
        out_specs=[tok(POOL_WIDTH), tok(3 * HY_WIDTH), tok(NA_WIDTH), kv_spec, kv_spec],
        out_shape=[
            jax.ShapeDtypeStruct((B, L, POOL_WIDTH), F32),
            jax.ShapeDtypeStruct((B, L, 3 * HY_WIDTH), F32),
            jax.ShapeDtypeStruct((B, L, NA_WIDTH), BF16),
            kv_shape,
            kv_shape,
        ],
        input_output_aliases=aliases,
        compiler_params=_params("parallel", "parallel"),
        name="in_proj",
    )(*args)


POOL_PAD = 8


def _pool_kernel(u_ref, w_ref, s_ref, o_ref, pad_ref):
    nseq, L = u_ref.shape[0], u_ref.shape[1]
    n = L + 2 * POOL_PAD

    def sh(v, s):
        return pltpu.roll(v, s % n, axis=0)

    lane_p = lax.broadcasted_iota(jnp.int32, (n, POOL_WIDTH), 1)
    t = lax.broadcasted_iota(jnp.int32, (L, POOL_WIDTH), 0)
    lane = lax.broadcasted_iota(jnp.int32, (L, POOL_WIDTH), 1)
    half = jnp.where(lane < 64, 1, jnp.where(lane < 128, 2, jnp.where(lane < 192, 4, 8)))
    lo = jnp.maximum(t - half, 0)
    hi = jnp.minimum(t + half - 1, L - 1)
    inv_cnt = 1.0 / (hi - lo + 1).astype(F32)

    for b in range(nseq):
        u = u_ref[b]
        pad_ref[b, 0:POOL_PAD, :] = jnp.zeros((POOL_PAD, POOL_WIDTH), F32)
        pad_ref[b, POOL_PAD + L:n, :] = jnp.zeros((POOL_PAD, POOL_WIDTH), F32)
        pad_ref[b, POOL_PAD:POOL_PAD + L, :] = u
        a = pad_ref[b]
        p2 = a + sh(a, 1)
        p4 = sh(p2, 1) + sh(p2, -1)
        p8 = sh(p4, 2) + sh(p4, -2)
        p16 = sh(p8, 4) + sh(p8, -4)
        win = jnp.where(lane_p < 64, p2, jnp.where(lane_p < 128, p4, jnp.where(lane_p < 192, p8, p16)))
        pooled = win[POOL_PAD:POOL_PAD + L, :] * inv_cnt - u
        o_ref[b] = (_dot(pooled.astype(BF16), w_ref[...]) * s_ref[...]).astype(o_ref.dtype)


def _pool_mixer(l, u_pool, w_bd, scale, per_step):
    B, L, _ = u_pool.shape
    blk = pl.BlockSpec((per_step, L, POOL_WIDTH), lambda b: (b, 0, 0))
    return pl.pallas_call(
        _pool_kernel,
        grid=(B // per_step,),
        in_specs=[blk, _layer((POOL_WIDTH, POOL_WIDTH), l), _layer((1, POOL_WIDTH), l)],
        out_specs=blk,
        out_shape=jax.ShapeDtypeStruct((B, L, POOL_WIDTH), BF16),
        scratch_shapes=[pltpu.VMEM((per_step, L + 2 * POOL_PAD, POOL_WIDTH), F32)],
        compiler_params=_params("parallel"),
        name="pool_mixer",
    )(u_pool, w_bd, scale)


@functools.lru_cache(maxsize=None)
def _hyena_position_consts(L):
    p = np.abs(np.arange(2 * L, dtype=np.float64) - L)
    p[0] = 0.0
    t = p / (L - 1)
    w = 2.0 * math.pi * p / L
    f = np.linspace(1e-4, HY_BANDS - 1, HY_BANDS)
    z = np.zeros((HY_EMB_PAD, 2 * L), np.float64)
    z[0] = t
    z[1:1 + HY_BANDS] = np.cos(f[:, None] * w[None, :])
    z[1 + HY_BANDS:HY_EMB] = -np.sin(f[:, None] * w[None, :])
    deltas = np.abs(np.linspace(math.log(HY_TARGET) / HY_SLOW, math.log(HY_TARGET) / HY_FAST, HY_WIDTH))
    decay = np.exp(-t[:, None] * deltas[None, :])
    return z.astype(np.float32), decay.astype(np.float32)


FILTER_BLOCK = 512


def _filter_kernel(z_ref, dec_ref, w1_ref, b1_ref, f1_ref, w2_ref, b2_ref, f2_ref, w3_ref, o_ref, sum_ref):
    half, j = pl.program_id(0), pl.program_id(1)
    h = jnp.sin(f1_ref[...] * (_dot_f32(w1_ref[...], z_ref[...]) + b1_ref[...]))
    h = jnp.sin(f2_ref[...] * (_dot_f32(w2_ref[...], h) + b2_ref[...]))
    k = lax.dot_general(h, w3_ref[0], (((0,), (0,)), ((), ())), preferred_element_type=F32,
                        precision=lax.Precision.HIGHEST)
    unused = jnp.logical_and(lax.broadcasted_iota(jnp.int32, dec_ref.shape, 0) == 0,
                             jnp.logical_and(half == 0, j == 0))
    for o in range(2):
        ko = k[:, o * HY_WIDTH:(o + 1) * HY_WIDTH] * dec_ref[...]
        part = jnp.sum(jnp.abs(ko), axis=0, keepdims=True)

        @pl.when(j == 0)
        def _():
            sum_ref[0, o] = part

        @pl.when(j > 0)
        def _():
            sum_ref[0, o] = sum_ref[0, o] + part

        o_ref[o] = jnp.where(unused, 0.0, ko)


def _hyena_filters(l, L, w1, b1, f1, w2, b2, f2, w3):
    z, decay = _hyena_position_consts(L)
    fb = min(L, FILTER_BLOCK)
    nblk = L // fb
    return pl.pallas_call(
        _filter_kernel,
        grid=(2, nblk),
        in_specs=[
            pl.BlockSpec((HY_EMB_PAD, fb), lambda hf, j: (0, hf * nblk + j)),
            pl.BlockSpec((fb, HY_WIDTH), lambda hf, j: (hf * nblk + j, 0)),
            _layer((HY_FFN, HY_EMB_PAD), l), _layer((HY_FFN, 1), l), _layer((HY_FFN, 1), l),
            _layer((HY_FFN, HY_FFN), l), _layer((HY_FFN, 1), l), _layer((HY_FFN, 1), l),
            pl.BlockSpec((None, 1, HY_FFN, 2 * HY_WIDTH), lambda hf, j: (l, 1 - hf, 0, 0)),
        ],
        out_specs=[
            pl.BlockSpec((2, fb, HY_WIDTH), lambda hf, j: (0, hf * nblk + j, 0)),
            pl.BlockSpec((1, 2, 1, HY_WIDTH), lambda hf, j: (hf, 0, 0, 0)),
        ],
        out_shape=[
            jax.ShapeDtypeStruct((2, 2 * L, HY_WIDTH), F32),
            jax.ShapeDtypeStruct((2, 2, 1, HY_WIDTH), F32),
        ],
        compiler_params=_params("parallel", "arbitrary"),
        name="hyena_filters",
    )(jnp.asarray(z), jnp.asarray(decay), w1, b1, f1, w2, b2, f2, w3)


@functools.lru_cache(maxsize=None)
def _dft_consts():
    n = np.arange(DFT_BLOCK, dtype=np.float64)
    j = np.arange(DFT_BLOCK, dtype=np.float64)
    ang = 2.0 * math.pi * np.outer(j, n) / DFT_N
    fwd = np.concatenate([np.cos(ang), -np.sin(ang)], axis=0)
    fwd[DFT_BLOCK] = np.cos(math.pi * n)
    wgt = np.where(j == 0, 1.0, 2.0)[None, :]
    inv = np.concatenate([wgt * np.cos(ang.T), -2.0 * np.sin(ang.T)], axis=1) / DFT_N
    inv[:, DFT_BLOCK] = np.cos(math.pi * n) / DFT_N
    sign = np.tile(np.where(np.arange(DFT_BLOCK) % 2 == 0, 1.0, -1.0), 2)[:, None]
    col0 = fwd[:, 0:1].copy()
    return fwd, inv, sign.astype(np.float32), col0.astype(np.float32)


def _split2(a64):
    hi = _np_bf16(a64)
    return hi, _np_bf16(a64 - hi.astype(np.float64))


def _np_bf16(a64):
    return a64.astype(np.float32).astype(BF16)


def _spectra_kernel(a_ref, sum_ref, fh_ref, fl_ref, sign_ref, col0_ref, g_ref, ny_ref, prev_ref):
    @pl.when(pl.program_id(1) == 0)
    def _():
        prev_ref[...] = jnp.zeros(prev_ref.shape, F32)

    inv_norm = 1.0 / (sum_ref[0, 0] + 1e-6)
    row = lax.broadcasted_iota(jnp.int32, (DFT_N, HY_WIDTH), 0)
    for t in range(g_ref.shape[1]):
        a = a_ref[0, t * DFT_BLOCK:(t + 1) * DFT_BLOCK, :] * inv_norm
        a_hi = a.astype(BF16)
        a_lo = (a - a_hi.astype(F32)).astype(BF16)
        fa = _dot(fh_ref[...], a_hi) + (_dot(fh_ref[...], a_lo) + _dot(fl_ref[...], a_hi))
        g = fa + sign_ref[...] * prev_ref[...]
        ny_ref[0, t] = g[DFT_BLOCK:DFT_BLOCK + 1, :]
        g_ref[0, t] = jnp.where(row == DFT_BLOCK, 0.0, g)
        prev_ref[...] = fa - col0_ref[...] * a[0:1, :]


SPECTRA_PER_STEP = 4


def _hyena_spectra(gg, sums):
    nb = gg.shape[1] // (2 * DFT_BLOCK)
    per = min(nb, SPECTRA_PER_STEP)
    steps_per_half = nb // per
    fwd, _, sign, col0 = _dft_consts()
    fh, fl = _split2(fwd)
    return pl.pallas_call(
        _spectra_kernel,
        grid=(2, 2 * steps_per_half),
        in_specs=[
            pl.BlockSpec((1, per * DFT_BLOCK, HY_WIDTH), lambda o, s: (o, s, 0)),
            pl.BlockSpec((1, 1, 1, HY_WIDTH), lambda o, s: (s // steps_per_half, o, 0, 0)),
            _resident((DFT_N, DFT_BLOCK)), _resident((DFT_N, DFT_BLOCK)),
            _resident((DFT_N, 1)), _resident((DFT_N, 1)),
        ],
        out_specs=[
            pl.BlockSpec((1, per, DFT_N, HY_WIDTH), lambda o, s: (o, s, 0, 0)),
            pl.BlockSpec((1, per, 1, HY_WIDTH), lambda o, s: (o, s, 0, 0)),
        ],
        out_shape=[
            jax.ShapeDtypeStruct((2, 2 * nb, DFT_N, HY_WIDTH), F32),
            jax.ShapeDtypeStruct((2, 2 * nb, 1, HY_WIDTH), F32),
        ],
        scratch_shapes=[pltpu.VMEM((DFT_N, HY_WIDTH), F32)],
        compiler_params=_params("parallel", "arbitrary"),
        name="hyena_spectra",
    )(gg, sums, fh, fl, jnp.asarray(sign), jnp.asarray(col0))


SPEC_ROWS = 32
NY_ROWS = 16


def _hyena_kernel(u_ref, cw_ref, cb_ref, hb_ref, g_ref, ny_ref, fwd_ref, inv_ref, o_ref,
                  z_ref, gate_ref, zf_ref, y_ref):
    nseq, L = u_ref.shape[0], u_ref.shape[1]
    nb = L // DFT_BLOCK
    t = lax.broadcasted_iota(jnp.int32, (L, HY_WIDTH), 0)
    ny_row = lax.broadcasted_iota(jnp.int32, (NY_ROWS, HY_WIDTH), 0) == 0

    for b in range(nseq):
        for part in range(3):
            cols = slice(part * HY_WIDTH, (part + 1) * HY_WIDTH)
            u = u_ref[b, :, cols]
            prev = jnp.where(t == 0, 0.0, pltpu.roll(u, 1, axis=0))
            nxt = jnp.where(t == L - 1, 0.0, pltpu.roll(u, L - 1, axis=0))
            y = cb_ref[:, cols] + prev * cw_ref[0:1, cols] + u * cw_ref[1:2, cols] + nxt * cw_ref[2:3, cols]
            if part < 2:
                gate_ref[part] = y
            else:
                z_ref[...] = y

        for o in range(2):
            for j in range(nb):
                blk = slice(j * DFT_BLOCK, (j + 1) * DFT_BLOCK)
                zf_ref[j] = _dot(fwd_ref[...], z_ref[blk, :].astype(BF16))
            bias = hb_ref[o]

            def out_block(i, carry, o=o, bias=bias):
                def spec_rows(c, carry2):
                    r0 = pl.multiple_of(c * SPEC_ROWS, SPEC_ROWS)
                    re = pl.ds(r0, SPEC_ROWS)
                    im = pl.ds(DFT_BLOCK + r0, SPEC_ROWS)
                    top = jnp.zeros((SPEC_ROWS, HY_WIDTH), F32)
                    bot = jnp.zeros((SPEC_ROWS, HY_WIDTH), F32)
                    for j in range(nb):
                        d = i - j + nb
                        gr, gi = g_ref[o, d, re, :], g_ref[o, d, im, :]
                        zr, zi = zf_ref[j, re, :], zf_ref[j, im, :]
                        top = top + (gr * zr - gi * zi)
                        bot = bot + (gr * zi + gi * zr)
                    y_ref[re, :] = top.astype(BF16)
                    y_ref[im, :] = bot.astype(BF16)
                    return carry2

                lax.fori_loop(0, DFT_BLOCK // SPEC_ROWS, spec_rows, 0, unroll=(nb == 1))
                ny = jnp.zeros((1, HY_WIDTH), F32)
                for j in range(nb):
                    ny = ny + ny_ref[o, i - j + nb] * zf_ref[j, DFT_BLOCK:DFT_BLOCK + 1, :]
                head = y_ref[DFT_BLOCK:DFT_BLOCK + NY_ROWS, :].astype(F32)
                y_ref[DFT_BLOCK:DFT_BLOCK + NY_ROWS, :] = jnp.where(ny_row, ny, head).astype(BF16)
                conv = _dot(inv_ref[...], y_ref[...])
                rows = pl.ds(pl.multiple_of(i * DFT_BLOCK, DFT_BLOCK), DFT_BLOCK)
                z_ref[rows, :] = gate_ref[o, rows, :] * (conv + bias * z_ref[rows, :])
                return carry

            lax.fori_loop(0, nb, out_block, 0, unroll=(nb == 1))
        o_ref[b] = z_ref[...].astype(o_ref.dtype)


def _hyena_mixer(l, u_hy, conv_w, conv_b, hy_bias, spec, nyq, per_step):
    B, L, _ = u_hy.shape
    nd = spec.shape[1]
    fwd, inv, _, _ = _dft_consts()
    return pl.pallas_call(
        _hyena_kernel,
        grid=(B // per_step,),
        in_specs=[
            pl.BlockSpec((per_step, L, 3 * HY_WIDTH), lambda b: (b, 0, 0)),
            _layer((3, 3 * HY_WIDTH), l),
            _layer((1, 3 * HY_WIDTH), l),
            _layer((2, 1, HY_WIDTH), l),
            _resident((2, nd, DFT_N, HY_WIDTH)),
            _resident((2, nd, 1, HY_WIDTH)),
            _resident((DFT_N, DFT_BLOCK)),
            _resident((DFT_BLOCK, DFT_N)),
        ],
        out_specs=pl.BlockSpec((per_step, L, HY_WIDTH), lambda b: (b, 0, 0)),
        out_shape=jax.ShapeDtypeStruct((B, L, HY_WIDTH), BF16),
        scratch_shapes=[
            pltpu.VMEM((L, HY_WIDTH), F32),
            pltpu.VMEM((2, L, HY_WIDTH), F32),
            pltpu.VMEM((L // DFT_BLOCK, DFT_N, HY_WIDTH), F32),
            pltpu.VMEM((DFT_N, HY_WIDTH), BF16),
        ],
        compiler_params=_params("parallel"),
        name="hyena_mixer",
    )(u_hy, conv_w, conv_b, hy_bias, spec, nyq,
      jnp.asarray(fwd, F32).astype(BF16), jnp.asarray(inv, F32).astype(BF16))


HEAD_PAIR = 2 * NA_HEAD_DIM
SOFTMAX_ROWS = 64
PV_CHUNK = 256


def _paired_heads(q, keys, values, bias, s_ref, p_ref):
    M = q.shape[0]
    lane = lax.broadcasted_iota(jnp.int32, (M, HEAD_PAIR), 1)

    def own_lanes(ln, e):
        return (ln < NA_HEAD_DIM) if e == 0 else (ln >= NA_HEAD_DIM)

    def scores(h):
        cols = slice((h // 2) * HEAD_PAIR, (h // 2 + 1) * HEAD_PAIR)
        qh = jnp.where(own_lanes(lane, h % 2), q[:, cols], jnp.zeros((M, HEAD_PAIR), BF16))
        off, m = 0, None
        for i, k in enumerate(keys):
            s = _dot_nt(qh, k[:, cols])
            if bias is not None and i == 0:
                s = s + bias(h)
            s_ref[h % 2, :, off:off + k.shape[0]] = s
            mi = jnp.max(s, axis=-1, keepdims=True)
            m = mi if m is None else jnp.maximum(m, mi)
            off += k.shape[0]
        return m

    def attend(h, m):
        cols = slice((h // 2) * HEAD_PAIR, (h // 2 + 1) * HEAD_PAIR)
        acc, off = None, 0
        for v in values:
            vl = lax.broadcasted_iota(jnp.int32, (v.shape[0], HEAD_PAIR), 1)
            vh = jnp.where(own_lanes(vl, h % 2), v[:, cols], jnp.ones((v.shape[0], HEAD_PAIR), BF16))
            for c0 in range(0, v.shape[0], PV_CHUNK):
                kc = slice(off + c0, off + c0 + PV_CHUNK)
                for r0 in range(0, M, SOFTMAX_ROWS):
                    rows = slice(r0, r0 + SOFTMAX_ROWS)
                    p_ref[h % 2, rows, kc] = jnp.exp(s_ref[h % 2, rows, kc] - m[rows]).astype(BF16)
                part = _dot(p_ref[h % 2, :, kc], vh[c0:c0 + PV_CHUNK])
                acc = part if acc is None else acc + part
            off += v.shape[0]
        return acc / pltpu.roll(acc, NA_HEAD_DIM, axis=1)

    outs, halves = [], []
    m_next = scores(0)
    for h in range(NA_HEADS):
        m = m_next
        if h + 1 < NA_HEADS:
            m_next = scores(h + 1)
        halves.append(attend(h, m))
        if h % 2 == 1:
            outs.append(jnp.where(lane < NA_HEAD_DIM, halves[0], halves[1]))
            halves = []
    return jnp.concatenate(outs, axis=-1)


def _ctx_attn_kernel(q_ref, k_ref, v_ref, o_ref, s_ref, p_ref):
    nseq, L = q_ref.shape[0], q_ref.shape[1]
    even = lax.broadcasted_iota(jnp.int32, (L, HEAD_PAIR), 1) < NA_HEAD_DIM
    zero = jnp.zeros((L, HEAD_PAIR), BF16)
    units = [(b, p) for b in range(nseq) for p in range(NA_HEADS // 2)]

    def scores(n):
        b, p = units[n]
        cols = slice(p * HEAD_PAIR, (p + 1) * HEAD_PAIR)
        qp = q_ref[b, :, cols]
        q2 = jnp.concatenate([jnp.where(even, qp, zero), jnp.where(even, zero, qp)], axis=0)
        s_ref[n % 2] = _dot_nt(q2, k_ref[b, 0, :, cols].astype(BF16))

    def attend(n):
        b, p = units[n]
        cols = slice(p * HEAD_PAIR, (p + 1) * HEAD_PAIR)
        for r0 in range(0, 2 * L, SOFTMAX_ROWS):
            rows = slice(r0, r0 + SOFTMAX_ROWS)
            s = s_ref[n % 2, rows, :]
            e = jnp.exp(s - jnp.max(s, axis=-1, keepdims=True))
            p_ref[n % 2, rows, :] = (e * (1.0 / jnp.sum(e, axis=-1, keepdims=True))).astype(BF16)
        acc = _dot(p_ref[n % 2], v_ref[b, 0, :, cols].astype(BF16))
        o_ref[b, :, cols] = jnp.where(even, acc[:L], acc[L:]).astype(o_ref.dtype)

    scores(0)
    for n in range(len(units)):
        if n + 1 < len(units):
            scores(n + 1)
        attend(n)


def _context_attention(q, k_cache, v_cache, layer, per_step):
    B, L, _ = q.shape
    spec = pl.BlockSpec((per_step, L, NA_WIDTH), lambda b: (b, 0, 0))
    kv = pl.BlockSpec((per_step, 1, L, NA_WIDTH), lambda b: (b, layer, 0, 0))
    return pl.pallas_call(
        _ctx_attn_kernel,
        grid=(B // per_step,),
        in_specs=[spec, kv, kv],
        out_specs=spec,
        out_shape=jax.ShapeDtypeStruct((B, L, NA_WIDTH), BF16),
        scratch_shapes=[pltpu.VMEM((2, 2 * L, L), F32), pltpu.VMEM((2, 2 * L, L), BF16)],
        compiler_params=_params("parallel"),
        name="context_attention",
    )(q, k_cache, v_cache)


N_DR = 2 * NA_KH - 1
N_DC = 2 * NA_KW - 1
NA_RB = 4
NA_WIN = NA_RB + NA_KH
WIN_KEYS = NA_WIN * GRID_W


def _bias_kernel(rb_ref, o_ref):
    h = pl.program_id(0)
    qc = lax.broadcasted_iota(jnp.int32, (GRID_W, GRID_W), 0)
    kc = lax.broadcasted_iota(jnp.int32, (GRID_W, GRID_W), 1)
    dc = jnp.clip(kc - qc, -(NA_KW - 1), NA_KW - 1) + (NA_KW - 1)
    col_start = jnp.clip(qc - NA_KW // 2, 0, GRID_W - NA_KW)
    visible = (kc >= col_start) & (kc < col_start + NA_KW)
    blocks = []
    for dr in range(N_DR):
        acc = jnp.zeros((GRID_W, GRID_W), F32)
        for d in range(N_DC):
            acc = jnp.where(dc == d, rb_ref[h * N_DR + dr, d], acc)
        blocks.append(jnp.where(visible, acc, MASK_VALUE))
    hidden = jnp.full((GRID_W, GRID_W), MASK_VALUE, F32)
    for variant, (first, base) in enumerate(((0, NA_KH - 1), (None, NA_KH // 2 - 1), (NA_RB, -1))):
        for a in range(NA_RB):
            lo = a if first is None else first
            row = [blocks[i - a + base] if lo <= i < lo + NA_KH else hidden for i in range(NA_WIN)]
            o_ref[variant, 0, a * GRID_W:(a + 1) * GRID_W, :] = jnp.concatenate(row, axis=-1)


def _bias_table(rel_bias_l):
    return pl.pallas_call(
        _bias_kernel,
        grid=(NA_HEADS,),
        in_specs=[pl.BlockSpec(memory_space=pltpu.SMEM)],
        out_specs=pl.BlockSpec((3, 1, NA_RB * GRID_W, WIN_KEYS), lambda h: (0, h, 0, 0)),
        out_shape=jax.ShapeDtypeStruct((3, NA_HEADS, NA_RB * GRID_W, WIN_KEYS), F32),
        compiler_params=_params("parallel"),
        name="na_bias_table",
    )(rel_bias_l.reshape(NA_HEADS * N_DR, N_DC))


def _na_kernel(q_ref, k_ref, v_ref, kc_ref, vc_ref, tab_ref, o_ref, s_ref, p_ref):
    rows = k_ref.shape[1] // GRID_W
    rb = pl.program_id(1)
    first_row = jnp.clip(rb * NA_RB - NA_KH // 2, 0, rows - NA_WIN)
    start = pl.multiple_of(first_row * GRID_W, NA_RB * GRID_W)
    keys = [k_ref[0, pl.ds(start, WIN_KEYS), :], kc_ref[0, 0].astype(BF16)]
    values = [v_ref[0, pl.ds(start, WIN_KEYS), :], vc_ref[0, 0].astype(BF16)]
    out = _paired_heads(q_ref[0], keys, values, lambda h: tab_ref[0, h], s_ref, p_ref)
    o_ref[0] = out.astype(o_ref.dtype)


def _neighbourhood_attention(q, k, v, cache_k, cache_v, layer, table):
    B, L, _ = q.shape
    rows = L // GRID_W
    assert rows % NA_RB == 0 and rows >= NA_WIN + NA_RB
    nrb = rows // NA_RB
    Lc = cache_k.shape[2]
    seq = pl.BlockSpec((1, L, NA_WIDTH), lambda b, r: (b, 0, 0))
    blk = pl.BlockSpec((1, NA_RB * GRID_W, NA_WIDTH), lambda b, r: (b, r, 0))
    ctx = pl.BlockSpec((1, 1, Lc, NA_WIDTH), lambda b, r: (b, layer, 0, 0))
    placement = lambda r: jnp.where(r == 0, 0, jnp.where(r == nrb - 1, 2, 1))
    tab = pl.BlockSpec((1, NA_HEADS, NA_RB * GRID_W, WIN_KEYS), lambda b, r: (placement(r), 0, 0, 0))
    return pl.pallas_call(
        _na_kernel,
        grid=(B, nrb),
        in_specs=[blk, seq, seq, ctx, ctx, tab],
        out_specs=blk,
        out_shape=jax.ShapeDtypeStruct((B, L, NA_WIDTH), BF16),
        scratch_shapes=[pltpu.VMEM((2, NA_RB * GRID_W, WIN_KEYS + Lc), F32),
                        pltpu.VMEM((2, NA_RB * GRID_W, WIN_KEYS + Lc), BF16)],
        compiler_params=_params("parallel", "arbitrary"),
        name="neighbourhood_attention",
    )(q, k, v, cache_k, cache_v, table)


FF_CHUNK = 1024


def _out_mlp_kernel(x_ref, yp_ref, yh_ref, ya_ref, mod_ref, g_ref, wo_ref, wu_ref, wd_ref, o_ref):
    y = (_dot(yp_ref[0], wo_ref[0:POOL_WIDTH, :])
         + _dot(yh_ref[0], wo_ref[POOL_WIDTH:POOL_WIDTH + HY_WIDTH, :])
         + _dot(ya_ref[0], wo_ref[POOL_WIDTH + HY_WIDTH:D_MODEL, :]))
    x = x_ref[0] + mod_ref[0, 2:3, :] * y
    h = _modulated_norm(x, g_ref[...], mod_ref[0, 3:4, :], mod_ref[0, 4:5, :]).astype(BF16)
    acc = jnp.zeros(x.shape, F32)
    for c in range(D_FF // FF_CHUNK):
        cols = slice(c * FF_CHUNK, (c + 1) * FF_CHUNK)
        a = jnp.maximum(_dot(h, wu_ref[:, cols]), 0.0)
        acc = acc + _dot((a * a).astype(BF16), wd_ref[cols, :])
    o_ref[0] = x + mod_ref[0, 5:6, :] * acc


def _out_mlp(l, x, y_pool, y_hy, y_na, mod, mod_row0, g2, w_out, w_up, w_down):
    B, L, _ = x.shape
    tm = TOKEN_TILE
    tok = lambda w: pl.BlockSpec((1, tm, w), lambda b, i: (b, i, 0))
    return pl.pallas_call(
        _out_mlp_kernel,
        grid=(B, L // tm),
        in_specs=[
            tok(D_MODEL), tok(POOL_WIDTH), tok(HY_WIDTH), tok(NA_WIDTH),
            _mod_spec(l, mod_row0),
            _layer((1, D_MODEL), l),
            _layer((D_MODEL, D_MODEL), l),
            _layer((D_MODEL, D_FF), l),
            _layer((D_FF, D_MODEL), l),
        ],
        out_specs=tok(D_MODEL),
        out_shape=jax.ShapeDtypeStruct((B, L, D_MODEL), F32),
        compiler_params=_params("parallel", "parallel"),
        name="out_mlp",
    )(x, y_pool, y_hy, y_na, mod, g2, w_out, w_up, w_down)


def _block_diag(blocks):
    *lead, g, c, d = blocks.shape
    eye = jnp.eye(g, dtype=blocks.dtype)
    out = blocks[..., :, :, None, :] * eye[:, None, :, None]
    return out.reshape(*lead, g * c, g * d)


def kernel(x_prompt, x_sample, cache_k, cache_v, c, c_ctx, norm1_g, norm2_g, w_mod, b_mod, w_in,
           pool_w, pool_scale, hy_conv_w, hy_conv_b, hy_f1_w, hy_f1_b, hy_f1_freq, hy_f2_w, hy_f2_b,
           hy_f2_freq, hy_f3_w, hy_bias, q_norm_g, k_norm_g, rel_bias, w_out, w_up, w_down):
    n_prompt, seq, _ = x_prompt.shape
    n_dec, dec_seq, _ = x_sample.shape

    conds = jnp.zeros((MOD_ROWS, D_MODEL), F32).at[0].set(c_ctx).at[1:1 + n_dec].set(c)
    mod = _adaln(conds, w_mod, b_mod).reshape(DEPTH, MOD_ROWS, N_MOD, D_MODEL)

    cache_k = cache_k.reshape(n_dec, DEPTH, -1, NA_WIDTH)
    cache_v = cache_v.reshape(n_dec, DEPTH, -1, NA_WIDTH)

    g1 = norm1_g.reshape(DEPTH, 1, D_MODEL)
    g2 = norm2_g.reshape(DEPTH, 1, D_MODEL)
    w_in_b, w_out_b, w_up_b, w_down_b = (w.astype(BF16) for w in (w_in, w_out, w_up, w_down))
    pool_bd = _block_diag(pool_w).astype(BF16)
    pool_s = pool_scale.reshape(DEPTH, 1, POOL_WIDTH)
    qg = jnp.tile(q_norm_g, (1, NA_HEADS)).reshape(DEPTH, 1, NA_WIDTH)
    kg = jnp.tile(k_norm_g, (1, NA_HEADS)).reshape(DEPTH, 1, NA_WIDTH)
    col = lambda a: a.reshape(DEPTH, HY_FFN, 1)
    f1_w = jnp.zeros((DEPTH, HY_FFN, HY_EMB_PAD), F32).at[:, :, :HY_EMB].set(hy_f1_w.transpose(0, 2, 1))
    f3_w = hy_f3_w.reshape(DEPTH, HY_FFN, 2, 2 * HY_WIDTH).transpose(0, 2, 1, 3)
    filt_args = (f1_w, col(hy_f1_b), col(hy_f1_freq), hy_f2_w.transpose(0, 2, 1), col(hy_f2_b),
                 col(hy_f2_freq), f3_w)
    conv_b = hy_conv_b.reshape(DEPTH, 1, 3 * HY_WIDTH)
    hy_b = hy_bias.reshape(DEPTH, 2, 1, HY_WIDTH)

    xp = x_prompt.reshape(1, n_prompt * seq, D_MODEL)
    xs = x_sample
    new_k = new_v = None
    per_seq = lambda a: a.reshape(n_prompt, seq, a.shape[-1])
    flat = lambda a: a.reshape(1, n_prompt * seq, a.shape[-1])
    for l in range(DEPTH):
        u_pool, u_hy, q, new_k, new_v = _in_proj(l, xp, mod, 0, g1, w_in_b, qg, kg,
                                                 cache=(seq, new_k, new_v))
        spec, nyq = _hyena_spectra(*_hyena_filters(l, seq, *filt_args))
        y_pool = _pool_mixer(l, per_seq(u_pool), pool_bd, pool_s, CTX_PER_STEP)
        y_hy = _hyena_mixer(l, per_seq(u_hy), hy_conv_w, conv_b, hy_b, spec, nyq, CTX_PER_STEP)
        y_na = _context_attention(per_seq(q), new_k, new_v, l, CTX_PER_STEP)
        xp = _out_mlp(l, xp, flat(y_pool), flat(y_hy), flat(y_na), mod, 0, g2, w_out_b, w_up_b, w_down_b)

        u_pool, u_hy, q, k, v = _in_proj(l, xs, mod, 1, g1, w_in_b, qg, kg)
        spec, nyq = _hyena_spectra(*_hyena_filters(l, dec_seq, *filt_args))
        y_pool = _pool_mixer(l, u_pool, pool_bd, pool_s, 1)
        y_hy = _hyena_mixer(l, u_hy, hy_conv_w, conv_b, hy_b, spec, nyq, 1)
        y_na = _neighbourhood_attention(q, k, v, cache_k, cache_v, l, _bias_table(rel_bias[l]))
        xs = _out_mlp(l, xs, y_pool, y_hy, y_na, mod, 1, g2, w_out_b, w_up_b, w_down_b)

    shape_kv = (n_prompt, DEPTH, seq, NA_HEADS, NA_HEAD_DIM)
    return (xp.reshape(n_prompt, seq, D_MODEL), xs, new_k.reshape(shape_kv), new_v.reshape(shape_kv))
```

```python
import functools
import math

import numpy as np
import jax
import jax.numpy as jnp
from jax import lax
from jax.experimental import pallas as pl
from jax.experimental.pallas import tpu as pltpu

F32 = jnp.float32
BF16 = jnp.bfloat16

D_MODEL = 1024
DEPTH = 2
GRID_W = 64
POOL_WIDTH = 256
POOL_GROUPS = 4
POOL_GROUP_DIM = 64
HY_WIDTH = 256
HY_BANDS = 16
HY_EMB = 1 + 2 * HY_BANDS
HY_EMB_PAD = 128
HY_FFN = 64
HY_TARGET = 1e-2
HY_FAST = 0.3
HY_SLOW = 1.5
NA_HEAD_DIM = 64
NA_WIDTH = 512
NA_HEADS = 8
NA_KH = 8
NA_KW = 16
N_MOD = 6
D_FF = 4096
NORM_EPS = 1e-6
ATT_SCALE = NA_HEAD_DIM ** -0.5

C_POOL = 0
C_HY = POOL_WIDTH
C_Q = C_HY + 3 * HY_WIDTH
C_K = C_Q + NA_WIDTH
C_V = C_K + NA_WIDTH
IN_WIDTH = C_V + NA_WIDTH

MAX_DFT_BLOCK = 512
MASK_VALUE = -1e30
VMEM_LIMIT = 56 * 1024 * 1024
TOKEN_TILE = 512
MOD_ROWS = 8
CTX_PER_STEP = 4


def _params(*sem):
    return pltpu.CompilerParams(dimension_semantics=sem, vmem_limit_bytes=VMEM_LIMIT)


def _resident(shape):
    return pl.BlockSpec(shape, lambda *_: (0,) * len(shape), pipeline_mode=pl.Buffered(1))


def _layer(shape, l):
    return pl.BlockSpec((None,) + tuple(shape), lambda *_: (l,) + (0,) * len(shape),
                        pipeline_mode=pl.Buffered(1))


def _mod_spec(l, row0):
    return pl.BlockSpec((None, 1, N_MOD, D_MODEL), lambda b, i: (l, row0 + b, 0, 0))


def _dot(a, b):
    return jnp.dot(a, b, preferred_element_type=F32)


def _dot_nt(a, b):
    return lax.dot_general(a, b, (((1,), (1,)), ((), ())), preferred_element_type=F32)


def _dot_f32(a, b):
    return jnp.dot(a, b, preferred_element_type=F32, precision=lax.Precision.HIGHEST)


def _adaln_kernel(c_ref, w_ref, b_ref, o_ref):
    c = c_ref[...]
    s = c * (1.0 / (1.0 + jnp.exp(-c)))
    o_ref[0] = _dot(s.astype(BF16), w_ref[0].astype(BF16)) + b_ref[0]


def _adaln(conds, w_mod, b_mod):
    tn = 1536
    n = N_MOD * D_MODEL
    return pl.pallas_call(
        _adaln_kernel,
        grid=(DEPTH, n // tn),
        in_specs=[
            pl.BlockSpec((MOD_ROWS, D_MODEL), lambda l, j: (0, 0)),
            pl.BlockSpec((1, D_MODEL, tn), lambda l, j: (l, 0, j)),
            pl.BlockSpec((1, 1, tn), lambda l, j: (l, 0, j)),
        ],
        out_specs=pl.BlockSpec((1, MOD_ROWS, tn), lambda l, j: (l, 0, j)),
        out_shape=jax.ShapeDtypeStruct((DEPTH, MOD_ROWS, n), F32),
        compiler_params=_params("parallel", "parallel"),
        name="adaln",
    )(conds, w_mod, b_mod.reshape(DEPTH, 1, n))


def _modulated_norm(x, g, shift, scale):
    ms = jnp.mean(x * x, axis=-1, keepdims=True)
    return (x * lax.rsqrt(ms + NORM_EPS) * g) * (1.0 + scale) + shift


def _inproj_kernel(x_ref, mod_ref, g_ref, w_ref, qg_ref, kg_ref, *rest):
    up_ref, uh_ref, q_ref, k_ref, v_ref = rest[-5:]
    h = _modulated_norm(x_ref[0], g_ref[...], mod_ref[0, 0:1, :], mod_ref[0, 1:2, :]).astype(BF16)
    up_ref[0] = _dot(h, w_ref[:, C_POOL:C_HY])
    uh_ref[0] = _dot(h, w_ref[:, C_HY:C_Q])

    even = lax.broadcasted_iota(jnp.int32, (x_ref.shape[1], HEAD_PAIR), 1) < NA_HEAD_DIM

    def head_norm(u, g):
        outs = []
        for p in range(NA_HEADS // 2):
            cols = slice(p * HEAD_PAIR, (p + 1) * HEAD_PAIR)
            up = u[:, cols]
            sq = up * up
            ms_even = jnp.sum(jnp.where(even, sq, 0.0), axis=-1, keepdims=True) * (1.0 / NA_HEAD_DIM)
            ms_odd = jnp.sum(jnp.where(even, 0.0, sq), axis=-1, keepdims=True) * (1.0 / NA_HEAD_DIM)
            r = jnp.where(even, lax.rsqrt(ms_even + NORM_EPS), lax.rsqrt(ms_odd + NORM_EPS))
            outs.append(up * r * g[:, cols])
        return jnp.concatenate(outs, axis=-1)

    q = head_norm(_dot(h, w_ref[:, C_Q:C_K]), qg_ref[...])
    q_ref[0] = (q * ATT_SCALE).astype(q_ref.dtype)
    k = head_norm(_dot(h, w_ref[:, C_K:C_V]), kg_ref[...])
    v = _dot(h, w_ref[:, C_V:IN_WIDTH])
    k_ref[...] = k.astype(k_ref.dtype).reshape(k_ref.shape)
    v_ref[...] = v.astype(v_ref.dtype).reshape(v_ref.shape)


def _in_proj(l, x, mod, mod_row0, g1, w_in, qg, kg, cache=None):
    B, L, _ = x.shape
    tm = TOKEN_TILE
    tok = lambda w: pl.BlockSpec((1, tm, w), lambda b, i: (b, i, 0))
    in_specs = [
        tok(D_MODEL),
        _mod_spec(l, mod_row0),
        _layer((1, D_MODEL), l),
        _layer((D_MODEL, IN_WIDTH), l),
        _layer((1, NA_WIDTH), l),
        _layer((1, NA_WIDTH), l),
    ]
    args = [x, mod, g1, w_in, qg, kg]
    aliases = {}
    if cache is None:
        kv_spec = tok(NA_WIDTH)
        kv_shape = jax.ShapeDtypeStruct((B, L, NA_WIDTH), BF16)
    else:
        seq, k_cache, v_cache = cache
        assert B == 1 and tm % seq == 0
        per_tile = tm // seq
        kv_spec = pl.BlockSpec((per_tile, 1, seq, NA_WIDTH), lambda b, i: (i, l, 0, 0))
        kv_shape = jax.ShapeDtypeStruct((L // seq, DEPTH, seq, NA_WIDTH), F32)
        if k_cache is not None:
            in_specs += [pl.BlockSpec(memory_space=pl.ANY)] * 2
            aliases = {len(args): 3, len(args) + 1: 4}
            args += [k_cache, v_cache]
    return pl.pallas_call(
        _inproj_kernel,
        grid=(B, L // tm),
        in_specs=in_specs,
        out_specs=[tok(POOL_WIDTH), tok(3 * HY_WIDTH), tok(NA_WIDTH), kv_spec, kv_spec],
        out_shape=[
            jax.ShapeDtypeStruct((B, L, POOL_WIDTH), F32),
            jax.ShapeDtypeStruct((B, L, 3 * HY_WIDTH), F32),
            jax.ShapeDtypeStruct((B, L, NA_WIDTH), BF16),
            kv_shape,
            kv_shape,
        ],
        input_output_aliases=aliases,
        compiler_params=_params("parallel", "parallel"),
        name="in_proj",
    )(*args)


POOL_PAD = 8


def _pool_kernel(u_ref, w_ref, s_ref, o_ref, pad_ref):
    nseq, L = u_ref.shape[0], u_ref.shape[1]
    n = L + 2 * POOL_PAD

    def sh(v, s):
        return pltpu.roll(v, s % n, axis=0)

    lane_p = lax.broadcasted_iota(jnp.int32, (n, POOL_WIDTH), 1)
    t = lax.broadcasted_iota(jnp.int32, (L, POOL_WIDTH), 0)
    lane = lax.broadcasted_iota(jnp.int32, (L, POOL_WIDTH), 1)
    half = jnp.where(lane < 64, 1, jnp.where(lane < 128, 2, jnp.where(lane < 192, 4, 8)))
    lo = jnp.maximum(t - half, 0)
    hi = jnp.minimum(t + half - 1, L - 1)
    inv_cnt = 1.0 / (hi - lo + 1).astype(F32)

    for b in range(nseq):
        u = u_ref[b]
        pad_ref[b, 0:POOL_PAD, :] = jnp.zeros((POOL_PAD, POOL_WIDTH), F32)
        pad_ref[b, POOL_PAD + L:n, :] = jnp.zeros((POOL_PAD, POOL_WIDTH), F32)
        pad_ref[b, POOL_PAD:POOL_PAD + L, :] = u
        a = pad_ref[b]
        p2 = a + sh(a, 1)
        p4 = sh(p2, 1) + sh(p2, -1)
        p8 = sh(p4, 2) + sh(p4, -2)
        p16 = sh(p8, 4) + sh(p8, -4)
        win = jnp.where(lane_p < 64, p2, jnp.where(lane_p < 128, p4, jnp.where(lane_p < 192, p8, p16)))
        pooled = win[POOL_PAD:POOL_PAD + L, :] * inv_cnt - u
        o_ref[b] = (_dot(pooled.astype(BF16), w_ref[...]) * s_ref[...]).astype(o_ref.dtype)


def _pool_mixer(l, u_pool, w_bd, scale, per_step):
    B, L, _ = u_pool.shape
    blk = pl.BlockSpec((per_step, L, POOL_WIDTH), lambda b: (b, 0, 0))
    return pl.pallas_call(
        _pool_kernel,
        grid=(B // per_step,),
        in_specs=[blk, _layer((POOL_WIDTH, POOL_WIDTH), l), _layer((1, POOL_WIDTH), l)],
        out_specs=blk,
        out_shape=jax.ShapeDtypeStruct((B, L, POOL_WIDTH), BF16),
        scratch_shapes=[pltpu.VMEM((per_step, L + 2 * POOL_PAD, POOL_WIDTH), F32)],
        compiler_params=_params("parallel"),
        name="pool_mixer",
    )(u_pool, w_bd, scale)


@functools.lru_cache(maxsize=None)
def _hyena_position_consts(L):
    p = np.abs(np.arange(2 * L, dtype=np.float64) - L)
    p[0] = 0.0
    t = p / (L - 1)
    w = 2.0 * math.pi * p / L
    f = np.linspace(1e-4, HY_BANDS - 1, HY_BANDS)
    z = np.zeros((HY_EMB_PAD, 2 * L), np.float64)
    z[0] = t
    z[1:1 + HY_BANDS] = np.cos(f[:, None] * w[None, :])
    z[1 + HY_BANDS:HY_EMB] = -np.sin(f[:, None] * w[None, :])
    deltas = np.abs(np.linspace(math.log(HY_TARGET) / HY_SLOW, math.log(HY_TARGET) / HY_FAST, HY_WIDTH))
    decay = np.exp(-t[:, None] * deltas[None, :])
    return z.astype(np.float32), decay.astype(np.float32)


FILTER_BLOCK = 512


def _filter_kernel(z_ref, dec_ref, w1_ref, b1_ref, f1_ref, w2_ref, b2_ref, f2_ref, w3_ref, o_ref, sum_ref):
    half, j = pl.program_id(0), pl.program_id(1)
    h = jnp.sin(f1_ref[...] * (_dot_f32(w1_ref[...], z_ref[...]) + b1_ref[...]))
    h = jnp.sin(f2_ref[...] * (_dot_f32(w2_ref[...], h) + b2_ref[...]))
    k = lax.dot_general(h, w3_ref[0], (((0,), (0,)), ((), ())), preferred_element_type=F32,
                        precision=lax.Precision.HIGHEST)
    unused = jnp.logical_and(lax.broadcasted_iota(jnp.int32, dec_ref.shape, 0) == 0,
                             jnp.logical_and(half == 0, j == 0))
    for o in range(2):
        ko = k[:, o * HY_WIDTH:(o + 1) * HY_WIDTH] * dec_ref[...]
        part = jnp.sum(jnp.abs(ko), axis=0, keepdims=True)

        @pl.when(j == 0)
        def _():
            sum_ref[0, o] = part

        @pl.when(j > 0)
        def _():
            sum_ref[0, o] = sum_ref[0, o] + part

        o_ref[o] = jnp.where(unused, 0.0, ko)


def _hyena_filters(l, L, w1, b1, f1, w2, b2, f2, w3):
    z, decay = _hyena_position_consts(L)
    fb = min(L, FILTER_BLOCK)
    nblk = L // fb
    return pl.pallas_call(
        _filter_kernel,
        grid=(2, nblk),
        in_specs=[
            pl.BlockSpec((HY_EMB_PAD, fb), lambda hf, j: (0, hf * nblk + j)),
            pl.BlockSpec((fb, HY_WIDTH), lambda hf, j: (hf * nblk + j, 0)),
            _layer((HY_FFN, HY_EMB_PAD), l), _layer((HY_FFN, 1), l), _layer((HY_FFN, 1), l),
            _layer((HY_FFN, HY_FFN), l), _layer((HY_FFN, 1), l), _layer((HY_FFN, 1), l),
            pl.BlockSpec((None, 1, HY_FFN, 2 * HY_WIDTH), lambda hf, j: (l, 1 - hf, 0, 0)),
        ],
        out_specs=[
            pl.BlockSpec((2, fb, HY_WIDTH), lambda hf, j: (0, hf * nblk + j, 0)),
            pl.BlockSpec((1, 2, 1, HY_WIDTH), lambda hf, j: (hf, 0, 0, 0)),
        ],
        out_shape=[
            jax.ShapeDtypeStruct((2, 2 * L, HY_WIDTH), F32),
            jax.ShapeDtypeStruct((2, 2, 1, HY_WIDTH), F32),
        ],
        compiler_params=_params("parallel", "arbitrary"),
        name="hyena_filters",
    )(jnp.asarray(z), jnp.asarray(decay), w1, b1, f1, w2, b2, f2, w3)


def _dft_block(L):
    return min(L, MAX_DFT_BLOCK)


@functools.lru_cache(maxsize=None)
def _dft_consts(blk):
    n = np.arange(blk, dtype=np.float64)
    j = np.arange(blk, dtype=np.float64)
    ang = 2.0 * math.pi * np.outer(j, n) / (2 * blk)
    fwd = np.concatenate([np.cos(ang), -np.sin(ang)], axis=0)
    fwd[blk] = np.cos(math.pi * n)
    wgt = np.where(j == 0, 1.0, 2.0)[None, :]
    inv = np.concatenate([wgt * np.cos(ang.T), -2.0 * np.sin(ang.T)], axis=1) / (2 * blk)
    inv[:, blk] = np.cos(math.pi * n) / (2 * blk)
    sign = np.tile(np.where(np.arange(blk) % 2 == 0, 1.0, -1.0), 2)[:, None]
    col0 = fwd[:, 0:1].copy()
    return fwd, inv, sign.astype(np.float32), col0.astype(np.float32)


def _split2(a64):
    hi = _np_bf16(a64)
    return hi, _np_bf16(a64 - hi.astype(np.float64))


def _np_bf16(a64):
    return a64.astype(np.float32).astype(BF16)


def _spectra_kernel(a_ref, sum_ref, fh_ref, fl_ref, sign_ref, col0_ref, g_ref, ny_ref, prev_ref):
    @pl.when(pl.program_id(1) == 0)
    def _():
        prev_ref[...] = jnp.zeros(prev_ref.shape, F32)

    inv_norm = 1.0 / (sum_ref[0, 0] + 1e-6)
    n, blk = fh_ref.shape
    row = lax.broadcasted_iota(jnp.int32, (n, HY_WIDTH), 0)
    for t in range(g_ref.shape[1]):
        a = a_ref[0, t * blk:(t + 1) * blk, :] * inv_norm
        a_hi = a.astype(BF16)
        a_lo = (a - a_hi.astype(F32)).astype(BF16)
        fa = _dot(fh_ref[...], a_hi) + (_dot(fh_ref[...], a_lo) + _dot(fl_ref[...], a_hi))
        g = fa + sign_ref[...] * prev_ref[...]
        ny_ref[0, t] = g[blk:blk + 1, :]
        g_ref[0, t] = jnp.where(row == blk, 0.0, g)
        prev_ref[...] = fa - col0_ref[...] * a[0:1, :]


SPECTRA_PER_STEP = 4


def _hyena_spectra(gg, sums):
    blk = _dft_block(gg.shape[1] // 2)
    n = 2 * blk
    nb = gg.shape[1] // n
    per = min(nb, SPECTRA_PER_STEP)
    steps_per_half = nb // per
    fwd, _, sign, col0 = _dft_consts(blk)
    fh, fl = _split2(fwd)
    return pl.pallas_call(
        _spectra_kernel,
        grid=(2, 2 * steps_per_half),
        in_specs=[
            pl.BlockSpec((1, per * blk, HY_WIDTH), lambda o, s: (o, s, 0)),
            pl.BlockSpec((1, 1, 1, HY_WIDTH), lambda o, s: (s // steps_per_half, o, 0, 0)),
            _resident((n, blk)), _resident((n, blk)),
            _resident((n, 1)), _resident((n, 1)),
        ],
        out_specs=[
            pl.BlockSpec((1, per, n, HY_WIDTH), lambda o, s: (o, s, 0, 0)),
            pl.BlockSpec((1, per, 1, HY_WIDTH), lambda o, s: (o, s, 0, 0)),
        ],
        out_shape=[
            jax.ShapeDtypeStruct((2, 2 * nb, n, HY_WIDTH), F32),
            jax.ShapeDtypeStruct((2, 2 * nb, 1, HY_WIDTH), F32),
        ],
        scratch_shapes=[pltpu.VMEM((n, HY_WIDTH), F32)],
        compiler_params=_params("parallel", "arbitrary"),
        name="hyena_spectra",
    )(gg, sums, fh, fl, jnp.asarray(sign), jnp.asarray(col0))


SPEC_ROWS = 32
NY_ROWS = 16
CONV_ROWS = 64


def _hyena_kernel(u_ref, cw_ref, cb_ref, hb_ref, g_ref, ny_ref, fwd_ref, inv_ref, o_ref,
                  z_ref, gate_ref, zf_ref, y_ref):
    nseq, L = u_ref.shape[0], u_ref.shape[1]
    blk = fwd_ref.shape[1]
    nb = L // blk
    t = lax.broadcasted_iota(jnp.int32, (CONV_ROWS, HY_WIDTH), 0)
    ny_row = lax.broadcasted_iota(jnp.int32, (NY_ROWS, HY_WIDTH), 0) == 0

    for b in range(nseq):
        def short_conv(c, carry, b=b):
            r0 = pl.multiple_of(c * CONV_ROWS, CONV_ROWS)
            rows = pl.ds(r0, CONV_ROWS)
            above = u_ref[b, pl.ds(pl.multiple_of(jnp.maximum(r0 - 8, 0), 8), 8), :]
            below = u_ref[b, pl.ds(pl.multiple_of(jnp.minimum(r0 + CONV_ROWS, L - 8), 8), 8), :]
            first = jnp.where(r0 == 0, 0.0, above[7:8, :])
            last = jnp.where(r0 + CONV_ROWS == L, 0.0, below[0:1, :])
            for part in range(3):
                cols = slice(part * HY_WIDTH, (part + 1) * HY_WIDTH)
                u = u_ref[b, rows, cols]
                prev = jnp.where(t == 0, first[:, cols], pltpu.roll(u, 1, axis=0))
                nxt = jnp.where(t == CONV_ROWS - 1, last[:, cols], pltpu.roll(u, CONV_ROWS - 1, axis=0))
                y = (cb_ref[:, cols] + prev * cw_ref[0:1, cols] + u * cw_ref[1:2, cols]
                     + nxt * cw_ref[2:3, cols])
                if part < 2:
                    gate_ref[part, rows, :] = y
                else:
                    z_ref[rows, :] = y
            return carry

        lax.fori_loop(0, L // CONV_ROWS, short_conv, 0, unroll=(nb == 1))

        for o in range(2):
            for j in range(nb):
                zf_ref[j] = _dot(fwd_ref[...], z_ref[j * blk:(j + 1) * blk, :].astype(BF16))
            bias = hb_ref[o]

            def out_block(i, carry, o=o, bias=bias):
                def spec_rows(c, carry2):
                    r0 = pl.multiple_of(c * SPEC_ROWS, SPEC_ROWS)
                    re = pl.ds(r0, SPEC_ROWS)
                    im = pl.ds(blk + r0, SPEC_ROWS)
                    top = jnp.zeros((SPEC_ROWS, HY_WIDTH), F32)
                    bot = jnp.zeros((SPEC_ROWS, HY_WIDTH), F32)
                    for j in range(nb):
                        d = i - j + nb
                        gr, gi = g_ref[o, d, re, :], g_ref[o, d, im, :]
                        zr, zi = zf_ref[j, re, :], zf_ref[j, im, :]
                        top = top + (gr * zr - gi * zi)
                        bot = bot + (gr * zi + gi * zr)
                    y_ref[re, :] = top.astype(BF16)
                    y_ref[im, :] = bot.astype(BF16)
                    return carry2

                lax.fori_loop(0, blk // SPEC_ROWS, spec_rows, 0, unroll=(nb == 1))
                ny = jnp.zeros((1, HY_WIDTH), F32)
                for j in range(nb):
                    ny = ny + ny_ref[o, i - j + nb] * zf_ref[j, blk:blk + 1, :]
                head = y_ref[blk:blk + NY_ROWS, :].astype(F32)
                y_ref[blk:blk + NY_ROWS, :] = jnp.where(ny_row, ny, head).astype(BF16)
                conv = _dot(inv_ref[...], y_ref[...])
                rows = pl.ds(pl.multiple_of(i * blk, blk), blk)
                z_ref[rows, :] = gate_ref[o, rows, :] * (conv + bias * z_ref[rows, :])
                return carry

            lax.fori_loop(0, nb, out_block, 0, unroll=(nb == 1))
        o_ref[b] = z_ref[...].astype(o_ref.dtype)


def _hyena_mixer(l, u_hy, conv_w, conv_b, hy_bias, spec, nyq, per_step):
    B, L, _ = u_hy.shape
    nd = spec.shape[1]
    blk = _dft_block(L)
    n = 2 * blk
    fwd, inv, _, _ = _dft_consts(blk)
    return pl.pallas_call(
        _hyena_kernel,
        grid=(B // per_step,),
        in_specs=[
            pl.BlockSpec((per_step, L, 3 * HY_WIDTH), lambda b: (b, 0, 0)),
            _layer((3, 3 * HY_WIDTH), l),
            _layer((1, 3 * HY_WIDTH), l),
            _layer((2, 1, HY_WIDTH), l),
            _resident((2, nd, n, HY_WIDTH)),
            _resident((2, nd, 1, HY_WIDTH)),
            _resident((n, blk)),
            _resident((blk, n)),
        ],
        out_specs=pl.BlockSpec((per_step, L, HY_WIDTH), lambda b: (b, 0, 0)),
        out_shape=jax.ShapeDtypeStruct((B, L, HY_WIDTH), BF16),
        scratch_shapes=[
            pltpu.VMEM((L, HY_WIDTH), F32),
            pltpu.VMEM((2, L, HY_WIDTH), F32),
            pltpu.VMEM((L // blk, n, HY_WIDTH), F32),
            pltpu.VMEM((n, HY_WIDTH), BF16),
        ],
        compiler_params=_params("parallel"),
        name="hyena_mixer",
    )(u_hy, conv_w, conv_b, hy_bias, spec, nyq,
      jnp.asarray(fwd, F32).astype(BF16), jnp.asarray(inv, F32).astype(BF16))


HEAD_PAIR = 2 * NA_HEAD_DIM
SOFTMAX_ROWS = 64
PV_CHUNK = 256


def _paired_heads(q, keys, values, bias, s_ref, p_ref):
    M = q.shape[0]
    lane = lax.broadcasted_iota(jnp.int32, (M, HEAD_PAIR), 1)

    def own_lanes(ln, e):
        return (ln < NA_HEAD_DIM) if e == 0 else (ln >= NA_HEAD_DIM)

    def scores(h):
        cols = slice((h // 2) * HEAD_PAIR, (h // 2 + 1) * HEAD_PAIR)
        qh = jnp.where(own_lanes(lane, h % 2), q[:, cols], jnp.zeros((M, HEAD_PAIR), BF16))
        off, m = 0, None
        for i, k in enumerate(keys):
            s = _dot_nt(qh, k[:, cols])
            if bias is not None and i == 0:
                s = s + bias(h)
            s_ref[h % 2, :, off:off + k.shape[0]] = s
            mi = jnp.max(s, axis=-1, keepdims=True)
            m = mi if m is None else jnp.maximum(m, mi)
            off += k.shape[0]
        return m

    def attend(h, m):
        cols = slice((h // 2) * HEAD_PAIR, (h // 2 + 1) * HEAD_PAIR)
        acc, off = None, 0
        for v in values:
            vl = lax.broadcasted_iota(jnp.int32, (v.shape[0], HEAD_PAIR), 1)
            vh = jnp.where(own_lanes(vl, h % 2), v[:, cols], jnp.ones((v.shape[0], HEAD_PAIR), BF16))
            for c0 in range(0, v.shape[0], PV_CHUNK):
                kc = slice(off + c0, off + c0 + PV_CHUNK)
                for r0 in range(0, M, SOFTMAX_ROWS):
                    rows = slice(r0, r0 + SOFTMAX_ROWS)
                    p_ref[h % 2, rows, kc] = jnp.exp(s_ref[h % 2, rows, kc] - m[rows]).astype(BF16)
                part = _dot(p_ref[h % 2, :, kc], vh[c0:c0 + PV_CHUNK])
                acc = part if acc is None else acc + part
            off += v.shape[0]
        return acc / pltpu.roll(acc, NA_HEAD_DIM, axis=1)

    outs, halves = [], []
    m_next = scores(0)
    for h in range(NA_HEADS):
        m = m_next
        if h + 1 < NA_HEADS:
            m_next = scores(h + 1)
        halves.append(attend(h, m))
        if h % 2 == 1:
            outs.append(jnp.where(lane < NA_HEAD_DIM, halves[0], halves[1]))
            halves = []
    return jnp.concatenate(outs, axis=-1)


def _ctx_attn_kernel(q_ref, k_ref, v_ref, o_ref, s_ref, p_ref):
    nseq, L = q_ref.shape[0], q_ref.shape[1]
    even = lax.broadcasted_iota(jnp.int32, (L, HEAD_PAIR), 1) < NA_HEAD_DIM
    zero = jnp.zeros((L, HEAD_PAIR), BF16)
    units = [(b, p) for b in range(nseq) for p in range(NA_HEADS // 2)]

    def scores(n):
        b, p = units[n]
        cols = slice(p * HEAD_PAIR, (p + 1) * HEAD_PAIR)
        qp = q_ref[b, :, cols]
        q2 = jnp.concatenate([jnp.where(even, qp, zero), jnp.where(even, zero, qp)], axis=0)
        s_ref[n % 2] = _dot_nt(q2, k_ref[b, 0, :, cols].astype(BF16))

    def attend(n):
        b, p = units[n]
        cols = slice(p * HEAD_PAIR, (p + 1) * HEAD_PAIR)
        for r0 in range(0, 2 * L, SOFTMAX_ROWS):
            rows = slice(r0, r0 + SOFTMAX_ROWS)
            s = s_ref[n % 2, rows, :]
            e = jnp.exp(s - jnp.max(s, axis=-1, keepdims=True))
            p_ref[n % 2, rows, :] = (e * (1.0 / jnp.sum(e, axis=-1, keepdims=True))).astype(BF16)
        acc = _dot(p_ref[n % 2], v_ref[b, 0, :, cols].astype(BF16))
        o_ref[b, :, cols] = jnp.where(even, acc[:L], acc[L:]).astype(o_ref.dtype)

    scores(0)
    for n in range(len(units)):
        if n + 1 < len(units):
            scores(n + 1)
        attend(n)


def _context_attention(q, k_cache, v_cache, layer, per_step):
    B, L, _ = q.shape
    spec = pl.BlockSpec((per_step, L, NA_WIDTH), lambda b: (b, 0, 0))
    kv = pl.BlockSpec((per_step, 1, L, NA_WIDTH), lambda b: (b, layer, 0, 0))
    return pl.pallas_call(
        _ctx_attn_kernel,
        grid=(B // per_step,),
        in_specs=[spec, kv, kv],
        out_specs=spec,
        out_shape=jax.ShapeDtypeStruct((B, L, NA_WIDTH), BF16),
        scratch_shapes=[pltpu.VMEM((2, 2 * L, L), F32), pltpu.VMEM((2, 2 * L, L), BF16)],
        compiler_params=_params("parallel"),
        name="context_attention",
    )(q, k_cache, v_cache)


N_DR = 2 * NA_KH - 1
N_DC = 2 * NA_KW - 1
NA_RB = 4
NA_WIN = NA_RB + NA_KH
WIN_KEYS = NA_WIN * GRID_W


def _bias_kernel(rb_ref, o_ref):
    h = pl.program_id(0)
    qc = lax.broadcasted_iota(jnp.int32, (GRID_W, GRID_W), 0)
    kc = lax.broadcasted_iota(jnp.int32, (GRID_W, GRID_W), 1)
    dc = jnp.clip(kc - qc, -(NA_KW - 1), NA_KW - 1) + (NA_KW - 1)
    col_start = jnp.clip(qc - NA_KW // 2, 0, GRID_W - NA_KW)
    visible = (kc >= col_start) & (kc < col_start + NA_KW)
    blocks = []
    for dr in range(N_DR):
        acc = jnp.zeros((GRID_W, GRID_W), F32)
        for d in range(N_DC):
            acc = jnp.where(dc == d, rb_ref[h * N_DR + dr, d], acc)
        blocks.append(jnp.where(visible, acc, MASK_VALUE))
    hidden = jnp.full((GRID_W, GRID_W), MASK_VALUE, F32)
    for variant, (first, base) in enumerate(((0, NA_KH - 1), (None, NA_KH // 2 - 1), (NA_RB, -1))):
        for a in range(NA_RB):
            lo = a if first is None else first
            row = [blocks[i - a + base] if lo <= i < lo + NA_KH else hidden for i in range(NA_WIN)]
            o_ref[variant, 0, a * GRID_W:(a + 1) * GRID_W, :] = jnp.concatenate(row, axis=-1)


def _bias_table(rel_bias_l):
    return pl.pallas_call(
        _bias_kernel,
        grid=(NA_HEADS,),
        in_specs=[pl.BlockSpec(memory_space=pltpu.SMEM)],
        out_specs=pl.BlockSpec((3, 1, NA_RB * GRID_W, WIN_KEYS), lambda h: (0, h, 0, 0)),
        out_shape=jax.ShapeDtypeStruct((3, NA_HEADS, NA_RB * GRID_W, WIN_KEYS), F32),
        compiler_params=_params("parallel"),
        name="na_bias_table",
    )(rel_bias_l.reshape(NA_HEADS * N_DR, N_DC))


def _na_kernel(q_ref, k_ref, v_ref, kc_ref, vc_ref, tab_ref, o_ref, s_ref, p_ref):
    rows = k_ref.shape[1] // GRID_W
    rb = pl.program_id(1)
    first_row = jnp.clip(rb * NA_RB - NA_KH // 2, 0, rows - NA_WIN)
    start = pl.multiple_of(first_row * GRID_W, NA_RB * GRID_W)
    keys = [k_ref[0, pl.ds(start, WIN_KEYS), :], kc_ref[0, 0].astype(BF16)]
    values = [v_ref[0, pl.ds(start, WIN_KEYS), :], vc_ref[0, 0].astype(BF16)]
    out = _paired_heads(q_ref[0], keys, values, lambda h: tab_ref[0, h], s_ref, p_ref)
    o_ref[0] = out.astype(o_ref.dtype)


def _neighbourhood_attention(q, k, v, cache_k, cache_v, layer, table):
    B, L, _ = q.shape
    rows = L // GRID_W
    assert rows % NA_RB == 0 and rows >= NA_WIN + NA_RB
    nrb = rows // NA_RB
    Lc = cache_k.shape[2]
    seq = pl.BlockSpec((1, L, NA_WIDTH), lambda b, r: (b, 0, 0))
    blk = pl.BlockSpec((1, NA_RB * GRID_W, NA_WIDTH), lambda b, r: (b, r, 0))
    ctx = pl.BlockSpec((1, 1, Lc, NA_WIDTH), lambda b, r: (b, layer, 0, 0))
    placement = lambda r: jnp.where(r == 0, 0, jnp.where(r == nrb - 1, 2, 1))
    tab = pl.BlockSpec((1, NA_HEADS, NA_RB * GRID_W, WIN_KEYS), lambda b, r: (placement(r), 0, 0, 0))
    return pl.pallas_call(
        _na_kernel,
        grid=(B, nrb),
        in_specs=[blk, seq, seq, ctx, ctx, tab],
        out_specs=blk,
        out_shape=jax.ShapeDtypeStruct((B, L, NA_WIDTH), BF16),
        scratch_shapes=[pltpu.VMEM((2, NA_RB * GRID_W, WIN_KEYS + Lc), F32),
                        pltpu.VMEM((2, NA_RB * GRID_W, WIN_KEYS + Lc), BF16)],
        compiler_params=_params("parallel", "arbitrary"),
        name="neighbourhood_attention",
    )(q, k, v, cache_k, cache_v, table)


FF_CHUNK = 1024


def _out_mlp_kernel(x_ref, yp_ref, yh_ref, ya_ref, mod_ref, g_ref, wo_ref, wu_ref, wd_ref, o_ref):
    y = (_dot(yp_ref[0], wo_ref[0:POOL_WIDTH, :])
         + _dot(yh_ref[0], wo_ref[POOL_WIDTH:POOL_WIDTH + HY_WIDTH, :])
         + _dot(ya_ref[0], wo_ref[POOL_WIDTH + HY_WIDTH:D_MODEL, :]))
    x = x_ref[0] + mod_ref[0, 2:3, :] * y
    h = _modulated_norm(x, g_ref[...], mod_ref[0, 3:4, :], mod_ref[0, 4:5, :]).astype(BF16)
    acc = jnp.zeros(x.shape, F32)
    for c in range(D_FF // FF_CHUNK):
        cols = slice(c * FF_CHUNK, (c + 1) * FF_CHUNK)
        a = jnp.maximum(_dot(h, wu_ref[:, cols]), 0.0)
        acc = acc + _dot((a * a).astype(BF16), wd_ref[cols, :])
    o_ref[0] = x + mod_ref[0, 5:6, :] * acc


def _out_mlp(l, x, y_pool, y_hy, y_na, mod, mod_row0, g2, w_out, w_up, w_down):
    B, L, _ = x.shape
    tm = TOKEN_TILE
    tok = lambda w: pl.BlockSpec((1, tm, w), lambda b, i: (b, i, 0))
    return pl.pallas_call(
        _out_mlp_kernel,
        grid=(B, L // tm),
        in_specs=[
            tok(D_MODEL), tok(POOL_WIDTH), tok(HY_WIDTH), tok(NA_WIDTH),
            _mod_spec(l, mod_row0),
            _layer((1, D_MODEL), l),
            _layer((D_MODEL, D_MODEL), l),
            _layer((D_MODEL, D_FF), l),
            _layer((D_FF, D_MODEL), l),
        ],
        out_specs=tok(D_MODEL),
        out_shape=jax.ShapeDtypeStruct((B, L, D_MODEL), F32),
        compiler_params=_params("parallel", "parallel"),
        name="out_mlp",
    )(x, y_pool, y_hy, y_na, mod, g2, w_out, w_up, w_down)


def _block_diag(blocks):
    *lead, g, c, d = blocks.shape
    eye = jnp.eye(g, dtype=blocks.dtype)
    out = blocks[..., :, :, None, :] * eye[:, None, :, None]
    return out.reshape(*lead, g * c, g * d)


def kernel(x_prompt, x_sample, cache_k, cache_v, c, c_ctx, norm1_g, norm2_g, w_mod, b_mod, w_in,
           pool_w, pool_scale, hy_conv_w, hy_conv_b, hy_f1_w, hy_f1_b, hy_f1_freq, hy_f2_w, hy_f2_b,
           hy_f2_freq, hy_f3_w, hy_bias, q_norm_g, k_norm_g, rel_bias, w_out, w_up, w_down):
    n_prompt, seq, _ = x_prompt.shape
    n_dec, dec_seq, _ = x_sample.shape

    conds = jnp.zeros((MOD_ROWS, D_MODEL), F32).at[0].set(c_ctx).at[1:1 + n_dec].set(c)
    mod = _adaln(conds, w_mod, b_mod).reshape(DEPTH, MOD_ROWS, N_MOD, D_MODEL)

    cache_k = cache_k.reshape(n_dec, DEPTH, -1, NA_WIDTH)
    cache_v = cache_v.reshape(n_dec, DEPTH, -1, NA_WIDTH)

    g1 = norm1_g.reshape(DEPTH, 1, D_MODEL)
    g2 = norm2_g.reshape(DEPTH, 1, D_MODEL)
    w_in_b, w_out_b, w_up_b, w_down_b = (w.astype(BF16) for w in (w_in, w_out, w_up, w_down))
    pool_bd = _block_diag(pool_w).astype(BF16)
    pool_s = pool_scale.reshape(DEPTH, 1, POOL_WIDTH)
    qg = jnp.tile(q_norm_g, (1, NA_HEADS)).reshape(DEPTH, 1, NA_WIDTH)
    kg = jnp.tile(k_norm_g, (1, NA_HEADS)).reshape(DEPTH, 1, NA_WIDTH)
    col = lambda a: a.reshape(DEPTH, HY_FFN, 1)
    f1_w = jnp.zeros((DEPTH, HY_FFN, HY_EMB_PAD), F32).at[:, :, :HY_EMB].set(hy_f1_w.transpose(0, 2, 1))
    f3_w = hy_f3_w.reshape(DEPTH, HY_FFN, 2, 2 * HY_WIDTH).transpose(0, 2, 1, 3)
    filt_args = (f1_w, col(hy_f1_b), col(hy_f1_freq), hy_f2_w.transpose(0, 2, 1), col(hy_f2_b),
                 col(hy_f2_freq), f3_w)
    conv_b = hy_conv_b.reshape(DEPTH, 1, 3 * HY_WIDTH)
    hy_b = hy_bias.reshape(DEPTH, 2, 1, HY_WIDTH)

    xp = x_prompt.reshape(1, n_prompt * seq, D_MODEL)
    xs = x_sample
    new_k = new_v = None
    per_seq = lambda a: a.reshape(n_prompt, seq, a.shape[-1])
    flat = lambda a: a.reshape(1, n_prompt * seq, a.shape[-1])
    for l in range(DEPTH):
        u_pool, u_hy, q, new_k, new_v = _in_proj(l, xp, mod, 0, g1, w_in_b, qg, kg,
                                                 cache=(seq, new_k, new_v))
        spec, nyq = _hyena_spectra(*_hyena_filters(l, seq, *filt_args))
        y_pool = _pool_mixer(l, per_seq(u_pool), pool_bd, pool_s, CTX_PER_STEP)
        y_hy = _hyena_mixer(l, per_seq(u_hy), hy_conv_w, conv_b, hy_b, spec, nyq, CTX_PER_STEP)
        y_na = _context_attention(per_seq(q), new_k, new_v, l, CTX_PER_STEP)
        xp = _out_mlp(l, xp, flat(y_pool), flat(y_hy), flat(y_na), mod, 0, g2, w_out_b, w_up_b, w_down_b)

        u_pool, u_hy, q, k, v = _in_proj(l, xs, mod, 1, g1, w_in_b, qg, kg)
        spec, nyq = _hyena_spectra(*_hyena_filters(l, dec_seq, *filt_args))
        y_pool = _pool_mixer(l, u_pool, pool_bd, pool_s, 1)
        y_hy = _hyena_mixer(l, u_hy, hy_conv_w, conv_b, hy_b, spec, nyq, 1)
        y_na = _neighbourhood_attention(q, k, v, cache_k, cache_v, l, _bias_table(rel_bias[l]))
        xs = _out_mlp(l, xs, y_pool, y_hy, y_na, mod, 1, g2, w_out_b, w_up_b, w_down_b)

    shape_kv = (n_prompt, DEPTH, seq, NA_HEADS, NA_HEAD_DIM)
    return (xp.reshape(n_prompt, seq, D_MODEL), xs, new_k.reshape(shape_kv), new_v.reshape(shape_kv))
```

```python
import functools
import math

import numpy as np
import jax
import jax.numpy as jnp
from jax import lax
from jax.experimental import pallas as pl
from jax.experimental.pallas import tpu as pltpu

F32 = jnp.float32
BF16 = jnp.bfloat16

D_MODEL = 1024
DEPTH = 2
GRID_W = 64
POOL_WIDTH = 256
POOL_GROUPS = 4
POOL_GROUP_DIM = 64
HY_WIDTH = 256
HY_BANDS = 16
HY_EMB = 1 + 2 * HY_BANDS
HY_EMB_PAD = 128
HY_FFN = 64
HY_TARGET = 1e-2
HY_FAST = 0.3
HY_SLOW = 1.5
NA_HEAD_DIM = 64
NA_WIDTH = 512
NA_HEADS = 8
NA_KH = 8
NA_KW = 16
N_MOD = 6
D_FF = 4096
NORM_EPS = 1e-6
ATT_SCALE = NA_HEAD_DIM ** -0.5

C_POOL = 0
C_HY = POOL_WIDTH
C_Q = C_HY + 3 * HY_WIDTH
C_K = C_Q + NA_WIDTH
C_V = C_K + NA_WIDTH
IN_WIDTH = C_V + NA_WIDTH

MAX_DFT_BLOCK = 512
MASK_VALUE = -1e30
VMEM_LIMIT = 56 * 1024 * 1024
TOKEN_TILE = 512
IN_TOKEN_TILE = 1024
MOD_ROWS = 8
CTX_PER_STEP = 4


def _params(*sem):
    return pltpu.CompilerParams(dimension_semantics=sem, vmem_limit_bytes=VMEM_LIMIT)


def _resident(shape):
    return pl.BlockSpec(shape, lambda *_: (0,) * len(shape), pipeline_mode=pl.Buffered(1))


def _layer(shape, l):
    return pl.BlockSpec((None,) + tuple(shape), lambda *_: (l,) + (0,) * len(shape),
                        pipeline_mode=pl.Buffered(1))


def _mod_spec(l, row0):
    return pl.BlockSpec((None, 1, N_MOD, D_MODEL), lambda b, i: (l, row0 + b, 0, 0))


def _dot(a, b):
    return jnp.dot(a, b, preferred_element_type=F32)


def _dot_nt(a, b):
    return lax.dot_general(a, b, (((1,), (1,)), ((), ())), preferred_element_type=F32)


def _dot_f32(a, b):
    return jnp.dot(a, b, preferred_element_type=F32, precision=lax.Precision.HIGHEST)


def _adaln_kernel(c_ref, w_ref, b_ref, o_ref):
    c = c_ref[...]
    s = c * (1.0 / (1.0 + jnp.exp(-c)))
    o_ref[0] = _dot(s.astype(BF16), w_ref[0].astype(BF16)) + b_ref[0]


def _adaln(conds, w_mod, b_mod):
    tn = 1536
    n = N_MOD * D_MODEL
    return pl.pallas_call(
        _adaln_kernel,
        grid=(DEPTH, n // tn),
        in_specs=[
            pl.BlockSpec((MOD_ROWS, D_MODEL), lambda l, j: (0, 0)),
            pl.BlockSpec((1, D_MODEL, tn), lambda l, j: (l, 0, j)),
            pl.BlockSpec((1, 1, tn), lambda l, j: (l, 0, j)),
        ],
        out_specs=pl.BlockSpec((1, MOD_ROWS, tn), lambda l, j: (l, 0, j)),
        out_shape=jax.ShapeDtypeStruct((DEPTH, MOD_ROWS, n), F32),
        compiler_params=_params("parallel", "parallel"),
        name="adaln",
    )(conds, w_mod, b_mod.reshape(DEPTH, 1, n))


def _modulated_norm(x, g, shift, scale):
    ms = jnp.mean(x * x, axis=-1, keepdims=True)
    return (x * lax.rsqrt(ms + NORM_EPS) * g) * (1.0 + scale) + shift


def _inproj_kernel(x_ref, mod_ref, g_ref, w_ref, qg_ref, kg_ref, *rest):
    up_ref, uh_ref, q_ref, k_ref, v_ref = rest[-5:]
    h = _modulated_norm(x_ref[0], g_ref[...], mod_ref[0, 0:1, :], mod_ref[0, 1:2, :]).astype(BF16)
    up_ref[0] = _dot(h, w_ref[:, C_POOL:C_HY])
    uh_ref[0] = _dot(h, w_ref[:, C_HY:C_Q])

    even = lax.broadcasted_iota(jnp.int32, (x_ref.shape[1], HEAD_PAIR), 1) < NA_HEAD_DIM

    def head_norm(u, g):
        outs = []
        for p in range(NA_HEADS // 2):
            cols = slice(p * HEAD_PAIR, (p + 1) * HEAD_PAIR)
            up = u[:, cols]
            sq = up * up
            ms_even = jnp.sum(jnp.where(even, sq, 0.0), axis=-1, keepdims=True) * (1.0 / NA_HEAD_DIM)
            ms_odd = jnp.sum(jnp.where(even, 0.0, sq), axis=-1, keepdims=True) * (1.0 / NA_HEAD_DIM)
            r = jnp.where(even, lax.rsqrt(ms_even + NORM_EPS), lax.rsqrt(ms_odd + NORM_EPS))
            outs.append(up * r * g[:, cols])
        return jnp.concatenate(outs, axis=-1)

    q = head_norm(_dot(h, w_ref[:, C_Q:C_K]), qg_ref[...])
    q_ref[0] = (q * ATT_SCALE).astype(q_ref.dtype)
    k = head_norm(_dot(h, w_ref[:, C_K:C_V]), kg_ref[...])
    v = _dot(h, w_ref[:, C_V:IN_WIDTH])
    k_ref[...] = k.astype(k_ref.dtype).reshape(k_ref.shape)
    v_ref[...] = v.astype(v_ref.dtype).reshape(v_ref.shape)


def _in_proj(l, x, mod, mod_row0, g1, w_in, qg, kg, cache=None):
    B, L, _ = x.shape
    tm = IN_TOKEN_TILE
    assert L % tm == 0
    tok = lambda w: pl.BlockSpec((1, tm, w), lambda b, i: (b, i, 0))
    in_specs = [
        tok(D_MODEL),
        _mod_spec(l, mod_row0),
        _layer((1, D_MODEL), l),
        _layer((D_MODEL, IN_WIDTH), l),
        _layer((1, NA_WIDTH), l),
        _layer((1, NA_WIDTH), l),
    ]
    args = [x, mod, g1, w_in, qg, kg]
    aliases = {}
    if cache is None:
        kv_spec = tok(NA_WIDTH)
        kv_shape = jax.ShapeDtypeStruct((B, L, NA_WIDTH), BF16)
    else:
        seq, k_cache, v_cache = cache
        assert B == 1 and tm % seq == 0
        per_tile = tm // seq
        kv_spec = pl.BlockSpec((per_tile, 1, seq, NA_WIDTH), lambda b, i: (i, l, 0, 0))
        kv_shape = jax.ShapeDtypeStruct((L // seq, DEPTH, seq, NA_WIDTH), F32)
        if k_cache is not None:
            in_specs += [pl.BlockSpec(memory_space=pl.ANY)] * 2
            aliases = {len(args): 3, len(args) + 1: 4}
            args += [k_cache, v_cache]
    return pl.pallas_call(
        _inproj_kernel,
        grid=(B, L // tm),
        in_specs=in_specs,
        out_specs=[tok(POOL_WIDTH), tok(3 * HY_WIDTH), tok(NA_WIDTH), kv_spec, kv_spec],
        out_shape=[
            jax.ShapeDtypeStruct((B, L, POOL_WIDTH), F32),
            jax.ShapeDtypeStruct((B, L, 3 * HY_WIDTH), F32),
            jax.ShapeDtypeStruct((B, L, NA_WIDTH), BF16),
            kv_shape,
            kv_shape,
        ],
        input_output_aliases=aliases,
        compiler_params=_params("parallel", "parallel"),
        name="in_proj",
    )(*args)


POOL_PAD = 8


def _pool_kernel(u_ref, w_ref, s_ref, o_ref, pad_ref):
    nseq, L = u_ref.shape[0], u_ref.shape[1]
    n = L + 2 * POOL_PAD

    def sh(v, s):
        return pltpu.roll(v, s % n, axis=0)

    lane_p = lax.broadcasted_iota(jnp.int32, (n, POOL_WIDTH), 1)
    t = lax.broadcasted_iota(jnp.int32, (L, POOL_WIDTH), 0)
    lane = lax.broadcasted_iota(jnp.int32, (L, POOL_WIDTH), 1)
    half = jnp.where(lane < 64, 1, jnp.where(lane < 128, 2, jnp.where(lane < 192, 4, 8)))
    lo = jnp.maximum(t - half, 0)
    hi = jnp.minimum(t + half - 1, L - 1)
    inv_cnt = 1.0 / (hi - lo + 1).astype(F32)

    for b in range(nseq):
        u = u_ref[b]
        pad_ref[b, 0:POOL_PAD, :] = jnp.zeros((POOL_PAD, POOL_WIDTH), F32)
        pad_ref[b, POOL_PAD + L:n, :] = jnp.zeros((POOL_PAD, POOL_WIDTH), F32)
        pad_ref[b, POOL_PAD:POOL_PAD + L, :] = u
        a = pad_ref[b]
        p2 = a + sh(a, 1)
        p4 = sh(p2, 1) + sh(p2, -1)
        p8 = sh(p4, 2) + sh(p4, -2)
        p16 = sh(p8, 4) + sh(p8, -4)
        win = jnp.where(lane_p < 64, p2, jnp.where(lane_p < 128, p4, jnp.where(lane_p < 192, p8, p16)))
        pooled = win[POOL_PAD:POOL_PAD + L, :] * inv_cnt - u
        o_ref[b] = (_dot(pooled.astype(BF16), w_ref[...]) * s_ref[...]).astype(o_ref.dtype)


def _pool_mixer(l, u_pool, w_bd, scale, per_step):
    B, L, _ = u_pool.shape
    blk = pl.BlockSpec((per_step, L, POOL_WIDTH), lambda b: (b, 0, 0))
    return pl.pallas_call(
        _pool_kernel,
        grid=(B // per_step,),
        in_specs=[blk, _layer((POOL_WIDTH, POOL_WIDTH), l), _layer((1, POOL_WIDTH), l)],
        out_specs=blk,
        out_shape=jax.ShapeDtypeStruct((B, L, POOL_WIDTH), BF16),
        scratch_shapes=[pltpu.VMEM((per_step, L + 2 * POOL_PAD, POOL_WIDTH), F32)],
        compiler_params=_params("parallel"),
        name="pool_mixer",
    )(u_pool, w_bd, scale)


@functools.lru_cache(maxsize=None)
def _hyena_position_consts(L):
    p = np.arange(L, dtype=np.float64)
    t = p / (L - 1)
    w = 2.0 * math.pi * p / L
    f = np.linspace(1e-4, HY_BANDS - 1, HY_BANDS)
    z = np.zeros((HY_EMB_PAD, L), np.float64)
    z[0] = t
    z[1:1 + HY_BANDS] = np.cos(f[:, None] * w[None, :])
    z[1 + HY_BANDS:HY_EMB] = -np.sin(f[:, None] * w[None, :])
    deltas = np.abs(np.linspace(math.log(HY_TARGET) / HY_SLOW, math.log(HY_TARGET) / HY_FAST, HY_WIDTH))
    decay = np.exp(-t[:, None] * deltas[None, :])
    return z.astype(np.float32), decay.astype(np.float32)


FILTER_BLOCK = 512


def _filter_kernel(z_ref, dec_ref, w1_ref, b1_ref, f1_ref, w2_ref, b2_ref, f2_ref, w3_ref,
                   kf_ref, kb_ref, sum_ref):
    j = pl.program_id(0)
    h = jnp.sin(f1_ref[...] * (_dot_f32(w1_ref[...], z_ref[...]) + b1_ref[...]))
    h = jnp.sin(f2_ref[...] * (_dot_f32(w2_ref[...], h) + b2_ref[...]))
    k = lax.dot_general(h, w3_ref[...], (((0,), (0,)), ((), ())), preferred_element_type=F32,
                        precision=lax.Precision.HIGHEST)
    for dirn, out_ref in enumerate((kf_ref, kb_ref)):
        for o in range(2):
            c0 = (dirn * 2 + o) * HY_WIDTH
            ko = k[:, c0:c0 + HY_WIDTH] * dec_ref[...]
            part = jnp.sum(jnp.abs(ko), axis=0, keepdims=True)

            @pl.when(j == 0)
            def _():
                sum_ref[dirn, o] = part

            @pl.when(j > 0)
            def _():
                sum_ref[dirn, o] = sum_ref[dirn, o] + part

            out_ref[o] = ko


def _hyena_filters(l, L, w1, b1, f1, w2, b2, f2, w3):
    z, decay = _hyena_position_consts(L)
    fb = min(L, FILTER_BLOCK)
    filt = pl.BlockSpec((2, fb, HY_WIDTH), lambda j: (0, j, 0))
    return pl.pallas_call(
        _filter_kernel,
        grid=(L // fb,),
        in_specs=[
            pl.BlockSpec((HY_EMB_PAD, fb), lambda j: (0, j)),
            pl.BlockSpec((fb, HY_WIDTH), lambda j: (j, 0)),
            _layer((HY_FFN, HY_EMB_PAD), l), _layer((HY_FFN, 1), l), _layer((HY_FFN, 1), l),
            _layer((HY_FFN, HY_FFN), l), _layer((HY_FFN, 1), l), _layer((HY_FFN, 1), l),
            _layer((HY_FFN, 4 * HY_WIDTH), l),
        ],
        out_specs=[filt, filt, pl.BlockSpec((2, 2, 1, HY_WIDTH), lambda j: (0, 0, 0, 0))],
        out_shape=[
            jax.ShapeDtypeStruct((2, L, HY_WIDTH), F32),
            jax.ShapeDtypeStruct((2, L, HY_WIDTH), F32),
            jax.ShapeDtypeStruct((2, 2, 1, HY_WIDTH), F32),
        ],
        compiler_params=_params("arbitrary"),
        name="hyena_filters",
    )(jnp.asarray(z), jnp.asarray(decay), w1, b1, f1, w2, b2, f2, w3)


def _dft_block(L):
    return min(L, MAX_DFT_BLOCK)


@functools.lru_cache(maxsize=None)
def _dft_consts(blk):
    n = np.arange(blk, dtype=np.float64)
    j = np.arange(blk, dtype=np.float64)
    ang = 2.0 * math.pi * np.outer(j, n) / (2 * blk)
    fwd = np.concatenate([np.cos(ang), -np.sin(ang)], axis=0)
    fwd[blk] = np.cos(math.pi * n)
    wgt = np.where(j == 0, 1.0, 2.0)[None, :]
    inv = np.concatenate([wgt * np.cos(ang.T), -2.0 * np.sin(ang.T)], axis=1) / (2 * blk)
    inv[:, blk] = np.cos(math.pi * n) / (2 * blk)
    sign = np.tile(np.where(np.arange(blk) % 2 == 0, 1.0, -1.0), 2)[:, None]
    col0 = fwd[:, 0:1].copy()
    return fwd, inv, sign.astype(np.float32), col0.astype(np.float32)


def _split2(a64):
    hi = _np_bf16(a64)
    return hi, _np_bf16(a64 - hi.astype(np.float64))


def _np_bf16(a64):
    return a64.astype(np.float32).astype(BF16)


def _spectra_kernel(kb_ref, kf_ref, sum_ref, fh_ref, fl_ref, sign_ref, signconj_ref, col0_ref,
                    g_ref, ny_ref, prev_ref, next0_ref):
    s = pl.program_id(1)
    per = g_ref.shape[1]
    n, blk = fh_ref.shape
    backward = s < pl.num_programs(1) // 2

    @pl.when(s == 0)
    def _():
        prev_ref[...] = jnp.zeros(prev_ref.shape, F32)
        next0_ref[...] = jnp.zeros(next0_ref.shape, F32)

    inv_norm = 1.0 / (sum_ref[0, 0] + 1e-6)
    row = lax.broadcasted_iota(jnp.int32, (n, HY_WIDTH), 0)

    def dft(a):
        a_hi = a.astype(BF16)
        a_lo = (a - a_hi.astype(F32)).astype(BF16)
        return _dot(fh_ref[...], a_hi) + (_dot(fh_ref[...], a_lo) + _dot(fl_ref[...], a_hi))

    def emit(t, fa, fa_tail):
        g = fa + sign_ref[...] * prev_ref[...]
        ny_ref[0, t] = g[blk:blk + 1, :]
        g_ref[0, t] = jnp.where(row == blk, 0.0, g)
        prev_ref[...] = fa_tail

    @pl.when(backward)
    def _():
        for t in range(per):
            q = per - 1 - t
            b = kb_ref[0, q * blk:(q + 1) * blk, :] * inv_norm
            tail = signconj_ref[...] * (dft(b) - col0_ref[...] * b[0:1, :])
            emit(t, col0_ref[...] * next0_ref[0:1, :] + tail, tail)
            next0_ref[0:1, :] = b[0:1, :]

    @pl.when(jnp.logical_not(backward))
    def _():
        for t in range(per):
            a = kf_ref[0, t * blk:(t + 1) * blk, :] * inv_norm
            fa = dft(a)
            emit(t, fa, fa - col0_ref[...] * a[0:1, :])


SPECTRA_PER_STEP = 4


def _hyena_spectra(kf, kb, sums):
    L = kf.shape[1]
    blk = _dft_block(L)
    n = 2 * blk
    nb = L // blk
    per = min(nb, SPECTRA_PER_STEP)
    half = nb // per
    fwd, _, sign, col0 = _dft_consts(blk)
    fh, fl = _split2(fwd)
    conj = np.where(np.arange(n) > blk, -1.0, 1.0).astype(np.float32)[:, None]
    return pl.pallas_call(
        _spectra_kernel,
        grid=(2, 2 * half),
        in_specs=[
            pl.BlockSpec((1, per * blk, HY_WIDTH), lambda o, s: (o, jnp.maximum(half - 1 - s, 0), 0)),
            pl.BlockSpec((1, per * blk, HY_WIDTH), lambda o, s: (o, jnp.maximum(s - half, 0), 0)),
            pl.BlockSpec((1, 1, 1, HY_WIDTH), lambda o, s: (jnp.where(s < half, 1, 0), o, 0, 0)),
            _resident((n, blk)), _resident((n, blk)),
            _resident((n, 1)), _resident((n, 1)), _resident((n, 1)),
        ],
        out_specs=[
            pl.BlockSpec((1, per, n, HY_WIDTH), lambda o, s: (o, s, 0, 0)),
            pl.BlockSpec((1, per, 1, HY_WIDTH), lambda o, s: (o, s, 0, 0)),
        ],
        out_shape=[
            jax.ShapeDtypeStruct((2, 2 * nb, n, HY_WIDTH), F32),
            jax.ShapeDtypeStruct((2, 2 * nb, 1, HY_WIDTH), F32),
        ],
        scratch_shapes=[pltpu.VMEM((n, HY_WIDTH), F32), pltpu.VMEM((8, HY_WIDTH), F32)],
        compiler_params=_params("parallel", "arbitrary"),
        name="hyena_spectra",
    )(kb, kf, sums, fh, fl, jnp.asarray(sign), jnp.asarray(sign * conj), jnp.asarray(col0))


SPEC_ROWS = 32
NY_ROWS = 16
CONV_ROWS = 64


def _hyena_kernel(u_ref, cw_ref, cb_ref, hb_ref, g_ref, ny_ref, fwd_ref, inv_ref, o_ref,
                  z_ref, gate_ref, zf_ref, y_ref):
    nseq, L = u_ref.shape[0], u_ref.shape[1]
    blk = fwd_ref.shape[1]
    nb = L // blk
    t = lax.broadcasted_iota(jnp.int32, (CONV_ROWS, HY_WIDTH), 0)
    ny_row = lax.broadcasted_iota(jnp.int32, (NY_ROWS, HY_WIDTH), 0) == 0

    for b in range(nseq):
        def short_conv(c, carry, b=b):
            r0 = pl.multiple_of(c * CONV_ROWS, CONV_ROWS)
            rows = pl.ds(r0, CONV_ROWS)
            above = u_ref[b, pl.ds(pl.multiple_of(jnp.maximum(r0 - 8, 0), 8), 8), :]
            below = u_ref[b, pl.ds(pl.multiple_of(jnp.minimum(r0 + CONV_ROWS, L - 8), 8), 8), :]
            first = jnp.where(r0 == 0, 0.0, above[7:8, :])
            last = jnp.where(r0 + CONV_ROWS == L, 0.0, below[0:1, :])
            for part in range(3):
                cols = slice(part * HY_WIDTH, (part + 1) * HY_WIDTH)
                u = u_ref[b, rows, cols]
                prev = jnp.where(t == 0, first[:, cols], pltpu.roll(u, 1, axis=0))
                nxt = jnp.where(t == CONV_ROWS - 1, last[:, cols], pltpu.roll(u, CONV_ROWS - 1, axis=0))
                y = (cb_ref[:, cols] + prev * cw_ref[0:1, cols] + u * cw_ref[1:2, cols]
                     + nxt * cw_ref[2:3, cols])
                if part < 2:
                    gate_ref[part, rows, :] = y
                else:
                    z_ref[rows, :] = y
            return carry

        lax.fori_loop(0, L // CONV_ROWS, short_conv, 0, unroll=(nb == 1))

        for o in range(2):
            for j in range(nb):
                zf_ref[j] = _dot(fwd_ref[...], z_ref[j * blk:(j + 1) * blk, :].astype(BF16))
            bias = hb_ref[o]

            def out_block(i, carry, o=o, bias=bias):
                def spec_rows(c, carry2):
                    r0 = pl.multiple_of(c * SPEC_ROWS, SPEC_ROWS)
                    re = pl.ds(r0, SPEC_ROWS)
                    im = pl.ds(blk + r0, SPEC_ROWS)
                    top = jnp.zeros((SPEC_ROWS, HY_WIDTH), F32)
                    bot = jnp.zeros((SPEC_ROWS, HY_WIDTH), F32)
                    for j in range(nb):
                        d = i - j + nb
                        gr, gi = g_ref[o, d, re, :], g_ref[o, d, im, :]
                        zr, zi = zf_ref[j, re, :], zf_ref[j, im, :]
                        top = top + (gr * zr - gi * zi)
                        bot = bot + (gr * zi + gi * zr)
                    y_ref[re, :] = top.astype(BF16)
                    y_ref[im, :] = bot.astype(BF16)
                    return carry2

                lax.fori_loop(0, blk // SPEC_ROWS, spec_rows, 0, unroll=(nb == 1))
                ny = jnp.zeros((1, HY_WIDTH), F32)
                for j in range(nb):
                    ny = ny + ny_ref[o, i - j + nb] * zf_ref[j, blk:blk + 1, :]
                head = y_ref[blk:blk + NY_ROWS, :].astype(F32)
                y_ref[blk:blk + NY_ROWS, :] = jnp.where(ny_row, ny, head).astype(BF16)
                conv = _dot(inv_ref[...], y_ref[...])
                rows = pl.ds(pl.multiple_of(i * blk, blk), blk)
                z_ref[rows, :] = gate_ref[o, rows, :] * (conv + bias * z_ref[rows, :])
                return carry

            lax.fori_loop(0, nb, out_block, 0, unroll=(nb == 1))
        o_ref[b] = z_ref[...].astype(o_ref.dtype)


def _hyena_mixer(l, u_hy, conv_w, conv_b, hy_bias, spec, nyq, per_step):
    B, L, _ = u_hy.shape
    nd = spec.shape[1]
    blk = _dft_block(L)
    n = 2 * blk
    fwd, inv, _, _ = _dft_consts(blk)
    return pl.pallas_call(
        _hyena_kernel,
        grid=(B // per_step,),
        in_specs=[
            pl.BlockSpec((per_step, L, 3 * HY_WIDTH), lambda b: (b, 0, 0)),
            _layer((3, 3 * HY_WIDTH), l),
            _layer((1, 3 * HY_WIDTH), l),
            _layer((2, 1, HY_WIDTH), l),
            _resident((2, nd, n, HY_WIDTH)),
            _resident((2, nd, 1, HY_WIDTH)),
            _resident((n, blk)),
            _resident((blk, n)),
        ],
        out_specs=pl.BlockSpec((per_step, L, HY_WIDTH), lambda b: (b, 0, 0)),
        out_shape=jax.ShapeDtypeStruct((B, L, HY_WIDTH), BF16),
        scratch_shapes=[
            pltpu.VMEM((L, HY_WIDTH), F32),
            pltpu.VMEM((2, L, HY_WIDTH), F32),
            pltpu.VMEM((L // blk, n, HY_WIDTH), F32),
            pltpu.VMEM((n, HY_WIDTH), BF16),
        ],
        compiler_params=_params("parallel"),
        name="hyena_mixer",
    )(u_hy, conv_w, conv_b, hy_bias, spec, nyq,
      jnp.asarray(fwd, F32).astype(BF16), jnp.asarray(inv, F32).astype(BF16))


HEAD_PAIR = 2 * NA_HEAD_DIM
SOFTMAX_ROWS = 64
PV_CHUNK = 256


def _paired_heads(q, keys, values, bias, s_ref, p_ref):
    M = q.shape[0]
    n_slots = s_ref.shape[0]
    lane = lax.broadcasted_iota(jnp.int32, (M, HEAD_PAIR), 1)

    def own_lanes(ln, e):
        return (ln < NA_HEAD_DIM) if e == 0 else (ln >= NA_HEAD_DIM)

    def scores(h):
        cols = slice((h // 2) * HEAD_PAIR, (h // 2 + 1) * HEAD_PAIR)
        qh = jnp.where(own_lanes(lane, h % 2), q[:, cols], jnp.zeros((M, HEAD_PAIR), BF16))
        off, m = 0, None
        for i, k in enumerate(keys):
            for c0 in range(0, k.shape[0], PV_CHUNK):
                s = _dot_nt(qh, k[c0:c0 + PV_CHUNK, cols])
                if bias is not None and i == 0:
                    s = s + bias(h, slice(c0, c0 + PV_CHUNK))
                s_ref[h % n_slots, :, off + c0:off + c0 + PV_CHUNK] = s
                mi = jnp.max(s, axis=-1, keepdims=True)
                m = mi if m is None else jnp.maximum(m, mi)
            off += k.shape[0]
        return m

    def attend(h, m):
        cols = slice((h // 2) * HEAD_PAIR, (h // 2 + 1) * HEAD_PAIR)
        acc, off = None, 0
        for v in values:
            vl = lax.broadcasted_iota(jnp.int32, (v.shape[0], HEAD_PAIR), 1)
            vh = jnp.where(own_lanes(vl, h % 2), v[:, cols], jnp.ones((v.shape[0], HEAD_PAIR), BF16))
            for c0 in range(0, v.shape[0], PV_CHUNK):
                kc = slice(off + c0, off + c0 + PV_CHUNK)
                for r0 in range(0, M, SOFTMAX_ROWS):
                    rows = slice(r0, r0 + SOFTMAX_ROWS)
                    p_ref[h % 2, rows, kc] = jnp.exp(s_ref[h % n_slots, rows, kc] - m[rows]).astype(BF16)
                part = _dot(p_ref[h % 2, :, kc], vh[c0:c0 + PV_CHUNK])
                acc = part if acc is None else acc + part
            off += v.shape[0]
        return acc / pltpu.roll(acc, NA_HEAD_DIM, axis=1)

    outs, halves = [], []
    ahead = n_slots - 1
    maxima = {h: scores(h) for h in range(ahead)}
    for h in range(NA_HEADS):
        if h + ahead < NA_HEADS:
            maxima[h + ahead] = scores(h + ahead)
        halves.append(attend(h, maxima.pop(h)))
        if h % 2 == 1:
            outs.append(jnp.where(lane < NA_HEAD_DIM, halves[0], halves[1]))
            halves = []
    return jnp.concatenate(outs, axis=-1)


def _ctx_attn_kernel(q_ref, k_ref, v_ref, o_ref, s_ref, p_ref):
    nseq, L = q_ref.shape[0], q_ref.shape[1]
    even = lax.broadcasted_iota(jnp.int32, (L, HEAD_PAIR), 1) < NA_HEAD_DIM
    zero = jnp.zeros((L, HEAD_PAIR), BF16)
    units = [(b, p) for b in range(nseq) for p in range(NA_HEADS // 2)]

    def scores(n):
        b, p = units[n]
        cols = slice(p * HEAD_PAIR, (p + 1) * HEAD_PAIR)
        qp = q_ref[b, :, cols]
        q2 = jnp.concatenate([jnp.where(even, qp, zero), jnp.where(even, zero, qp)], axis=0)
        s_ref[n % 2] = _dot_nt(q2, k_ref[b, 0, :, cols].astype(BF16))

    def attend(n):
        b, p = units[n]
        cols = slice(p * HEAD_PAIR, (p + 1) * HEAD_PAIR)
        for r0 in range(0, 2 * L, SOFTMAX_ROWS):
            rows = slice(r0, r0 + SOFTMAX_ROWS)
            s = s_ref[n % 2, rows, :]
            e = jnp.exp(s - jnp.max(s, axis=-1, keepdims=True))
            p_ref[n % 2, rows, :] = (e * (1.0 / jnp.sum(e, axis=-1, keepdims=True))).astype(BF16)
        acc = _dot(p_ref[n % 2], v_ref[b, 0, :, cols].astype(BF16))
        o_ref[b, :, cols] = jnp.where(even, acc[:L], acc[L:]).astype(o_ref.dtype)

    scores(0)
    for n in range(len(units)):
        if n + 1 < len(units):
            scores(n + 1)
        attend(n)


def _context_attention(q, k_cache, v_cache, layer, per_step):
    B, L, _ = q.shape
    spec = pl.BlockSpec((per_step, L, NA_WIDTH), lambda b: (b, 0, 0))
    kv = pl.BlockSpec((per_step, 1, L, NA_WIDTH), lambda b: (b, layer, 0, 0))
    return pl.pallas_call(
        _ctx_attn_kernel,
        grid=(B // per_step,),
        in_specs=[spec, kv, kv],
        out_specs=spec,
        out_shape=jax.ShapeDtypeStruct((B, L, NA_WIDTH), BF16),
        scratch_shapes=[pltpu.VMEM((2, 2 * L, L), F32), pltpu.VMEM((2, 2 * L, L), BF16)],
        compiler_params=_params("parallel"),
        name="context_attention",
    )(q, k_cache, v_cache)


N_DR = 2 * NA_KH - 1
N_DC = 2 * NA_KW - 1
NA_RB = 4
NA_WIN = NA_RB + NA_KH
WIN_KEYS = NA_WIN * GRID_W
NA_SCORE_SLOTS = 2


def _bias_kernel(rb_ref, o_ref):
    h = pl.program_id(0)
    qc = lax.broadcasted_iota(jnp.int32, (GRID_W, GRID_W), 0)
    kc = lax.broadcasted_iota(jnp.int32, (GRID_W, GRID_W), 1)
    dc = jnp.clip(kc - qc, -(NA_KW - 1), NA_KW - 1) + (NA_KW - 1)
    col_start = jnp.clip(qc - NA_KW // 2, 0, GRID_W - NA_KW)
    visible = (kc >= col_start) & (kc < col_start + NA_KW)
    blocks = []
    for dr in range(N_DR):
        acc = jnp.zeros((GRID_W, GRID_W), F32)
        for d in range(N_DC):
            acc = jnp.where(dc == d, rb_ref[h * N_DR + dr, d], acc)
        blocks.append(jnp.where(visible, acc, MASK_VALUE))
    hidden = jnp.full((GRID_W, GRID_W), MASK_VALUE, F32)
    for variant, (first, base) in enumerate(((0, NA_KH - 1), (None, NA_KH // 2 - 1), (NA_RB, -1))):
        for a in range(NA_RB):
            lo = a if first is None else first
            row = [blocks[i - a + base] if lo <= i < lo + NA_KH else hidden for i in range(NA_WIN)]
            o_ref[variant, 0, a * GRID_W:(a + 1) * GRID_W, :] = jnp.concatenate(row, axis=-1)


def _bias_table(rel_bias_l):
    return pl.pallas_call(
        _bias_kernel,
        grid=(NA_HEADS,),
        in_specs=[pl.BlockSpec(memory_space=pltpu.SMEM)],
        out_specs=pl.BlockSpec((3, 1, NA_RB * GRID_W, WIN_KEYS), lambda h: (0, h, 0, 0)),
        out_shape=jax.ShapeDtypeStruct((3, NA_HEADS, NA_RB * GRID_W, WIN_KEYS), F32),
        compiler_params=_params("parallel"),
        name="na_bias_table",
    )(rel_bias_l.reshape(NA_HEADS * N_DR, N_DC))


def _na_kernel(q_ref, k_ref, v_ref, kc_ref, vc_ref, tab_ref, o_ref, s_ref, p_ref):
    rows = k_ref.shape[1] // GRID_W
    rb = pl.program_id(1)
    first_row = jnp.clip(rb * NA_RB - NA_KH // 2, 0, rows - NA_WIN)
    start = pl.multiple_of(first_row * GRID_W, NA_RB * GRID_W)
    keys = [k_ref[0, pl.ds(start, WIN_KEYS), :], kc_ref[0, 0].astype(BF16)]
    values = [v_ref[0, pl.ds(start, WIN_KEYS), :], vc_ref[0, 0].astype(BF16)]
    out = _paired_heads(q_ref[0], keys, values, lambda h, kc: tab_ref[0, h, :, kc], s_ref, p_ref)
    o_ref[0] = out.astype(o_ref.dtype)


def _neighbourhood_attention(q, k, v, cache_k, cache_v, layer, table):
    B, L, _ = q.shape
    rows = L // GRID_W
    assert rows % NA_RB == 0 and rows >= NA_WIN + NA_RB
    nrb = rows // NA_RB
    Lc = cache_k.shape[2]
    seq = pl.BlockSpec((1, L, NA_WIDTH), lambda b, r: (b, 0, 0))
    blk = pl.BlockSpec((1, NA_RB * GRID_W, NA_WIDTH), lambda b, r: (b, r, 0))
    ctx = pl.BlockSpec((1, 1, Lc, NA_WIDTH), lambda b, r: (b, layer, 0, 0))
    placement = lambda r: jnp.where(r == 0, 0, jnp.where(r == nrb - 1, 2, 1))
    tab = pl.BlockSpec((1, NA_HEADS, NA_RB * GRID_W, WIN_KEYS), lambda b, r: (placement(r), 0, 0, 0))
    return pl.pallas_call(
        _na_kernel,
        grid=(B, nrb),
        in_specs=[blk, seq, seq, ctx, ctx, tab],
        out_specs=blk,
        out_shape=jax.ShapeDtypeStruct((B, L, NA_WIDTH), BF16),
        scratch_shapes=[pltpu.VMEM((NA_SCORE_SLOTS, NA_RB * GRID_W, WIN_KEYS + Lc), F32),
                        pltpu.VMEM((2, NA_RB * GRID_W, WIN_KEYS + Lc), BF16)],
        compiler_params=_params("parallel", "arbitrary"),
        name="neighbourhood_attention",
    )(q, k, v, cache_k, cache_v, table)


FF_CHUNK = 1024


def _out_mlp_kernel(x_ref, yp_ref, yh_ref, ya_ref, mod_ref, g_ref, wo_ref, wu_ref, wd_ref, o_ref):
    y = (_dot(yp_ref[0], wo_ref[0:POOL_WIDTH, :])
         + _dot(yh_ref[0], wo_ref[POOL_WIDTH:POOL_WIDTH + HY_WIDTH, :])
         + _dot(ya_ref[0], wo_ref[POOL_WIDTH + HY_WIDTH:D_MODEL, :]))
    x = x_ref[0] + mod_ref[0, 2:3, :] * y
    h = _modulated_norm(x, g_ref[...], mod_ref[0, 3:4, :], mod_ref[0, 4:5, :]).astype(BF16)
    acc = jnp.zeros(x.shape, F32)
    for c in range(D_FF // FF_CHUNK):
        cols = slice(c * FF_CHUNK, (c + 1) * FF_CHUNK)
        a = jnp.maximum(_dot(h, wu_ref[:, cols]), 0.0)
        acc = acc + _dot((a * a).astype(BF16), wd_ref[cols, :])
    o_ref[0] = x + mod_ref[0, 5:6, :] * acc


def _out_mlp(l, x, y_pool, y_hy, y_na, mod, mod_row0, g2, w_out, w_up, w_down):
    B, L, _ = x.shape
    tm = TOKEN_TILE
    assert L % tm == 0
    tok = lambda w: pl.BlockSpec((1, tm, w), lambda b, i: (b, i, 0))
    return pl.pallas_call(
        _out_mlp_kernel,
        grid=(B, L // tm),
        in_specs=[
            tok(D_MODEL), tok(POOL_WIDTH), tok(HY_WIDTH), tok(NA_WIDTH),
            _mod_spec(l, mod_row0),
            _layer((1, D_MODEL), l),
            _layer((D_MODEL, D_MODEL), l),
            _layer((D_MODEL, D_FF), l),
            _layer((D_FF, D_MODEL), l),
        ],
        out_specs=tok(D_MODEL),
        out_shape=jax.ShapeDtypeStruct((B, L, D_MODEL), F32),
        compiler_params=_params("parallel", "parallel"),
        name="out_mlp",
    )(x, y_pool, y_hy, y_na, mod, g2, w_out, w_up, w_down)


def _block_diag(blocks):
    *lead, g, c, d = blocks.shape
    eye = jnp.eye(g, dtype=blocks.dtype)
    out = blocks[..., :, :, None, :] * eye[:, None, :, None]
    return out.reshape(*lead, g * c, g * d)


def kernel(x_prompt, x_sample, cache_k, cache_v, c, c_ctx, norm1_g, norm2_g, w_mod, b_mod, w_in,
           pool_w, pool_scale, hy_conv_w, hy_conv_b, hy_f1_w, hy_f1_b, hy_f1_freq, hy_f2_w, hy_f2_b,
           hy_f2_freq, hy_f3_w, hy_bias, q_norm_g, k_norm_g, rel_bias, w_out, w_up, w_down):
    n_prompt, seq, _ = x_prompt.shape
    n_dec, dec_seq, _ = x_sample.shape

    conds = jnp.zeros((MOD_ROWS, D_MODEL), F32).at[0].set(c_ctx).at[1:1 + n_dec].set(c)
    mod = _adaln(conds, w_mod, b_mod).reshape(DEPTH, MOD_ROWS, N_MOD, D_MODEL)

    cache_k = cache_k.reshape(n_dec, DEPTH, -1, NA_WIDTH)
    cache_v = cache_v.reshape(n_dec, DEPTH, -1, NA_WIDTH)

    g1 = norm1_g.reshape(DEPTH, 1, D_MODEL)
    g2 = norm2_g.reshape(DEPTH, 1, D_MODEL)
    w_in_b, w_out_b, w_up_b, w_down_b = (w.astype(BF16) for w in (w_in, w_out, w_up, w_down))
    pool_bd = _block_diag(pool_w).astype(BF16)
    pool_s = pool_scale.reshape(DEPTH, 1, POOL_WIDTH)
    qg = jnp.tile(q_norm_g, (1, NA_HEADS)).reshape(DEPTH, 1, NA_WIDTH)
    kg = jnp.tile(k_norm_g, (1, NA_HEADS)).reshape(DEPTH, 1, NA_WIDTH)
    col = lambda a: a.reshape(DEPTH, HY_FFN, 1)
    f1_w = jnp.zeros((DEPTH, HY_FFN, HY_EMB_PAD), F32).at[:, :, :HY_EMB].set(hy_f1_w.transpose(0, 2, 1))
    filt_args = (f1_w, col(hy_f1_b), col(hy_f1_freq), hy_f2_w.transpose(0, 2, 1), col(hy_f2_b),
                 col(hy_f2_freq), hy_f3_w)
    conv_b = hy_conv_b.reshape(DEPTH, 1, 3 * HY_WIDTH)
    hy_b = hy_bias.reshape(DEPTH, 2, 1, HY_WIDTH)

    xp = x_prompt.reshape(1, n_prompt * seq, D_MODEL)
    xs = x_sample
    new_k = new_v = None
    per_seq = lambda a: a.reshape(n_prompt, seq, a.shape[-1])
    flat = lambda a: a.reshape(1, n_prompt * seq, a.shape[-1])
    for l in range(DEPTH):
        u_pool, u_hy, q, new_k, new_v = _in_proj(l, xp, mod, 0, g1, w_in_b, qg, kg,
                                                 cache=(seq, new_k, new_v))
        spec, nyq = _hyena_spectra(*_hyena_filters(l, seq, *filt_args))
        y_pool = _pool_mixer(l, per_seq(u_pool), pool_bd, pool_s, CTX_PER_STEP)
        y_hy = _hyena_mixer(l, per_seq(u_hy), hy_conv_w, conv_b, hy_b, spec, nyq, CTX_PER_STEP)
        y_na = _context_attention(per_seq(q), new_k, new_v, l, CTX_PER_STEP)
        xp = _out_mlp(l, xp, flat(y_pool), flat(y_hy), flat(y_na), mod, 0, g2, w_out_b, w_up_b, w_down_b)

        u_pool, u_hy, q, k, v = _in_proj(l, xs, mod, 1, g1, w_in_b, qg, kg)
        spec, nyq = _hyena_spectra(*_hyena_filters(l, dec_seq, *filt_args))
        y_pool = _pool_mixer(l, u_pool, pool_bd, pool_s, 1)
        y_hy = _hyena_mixer(l, u_hy, hy_conv_w, conv_b, hy_b, spec, nyq, 1)
        y_na = _neighbourhood_attention(q, k, v, cache_k, cache_v, l, _bias_table(rel_bias[l]))
        xs = _out_mlp(l, xs, y_pool, y_hy, y_na, mod, 1, g2, w_out_b, w_up_b, w_down_b)

    shape_kv = (n_prompt, DEPTH, seq, NA_HEADS, NA_HEAD_DIM)
    return (xp.reshape(n_prompt, seq, D_MODEL), xs, new_k.reshape(shape_kv), new_v.reshape(shape_kv))
```

```python
import functools
import math

import numpy as np
import jax
import jax.numpy as jnp
from jax import lax
from jax.experimental import pallas as pl
from jax.experimental.pallas import tpu as pltpu

F32 = jnp.float32
BF16 = jnp.bfloat16

D_MODEL = 1024
DEPTH = 2
GRID_W = 64
POOL_WIDTH = 256
POOL_GROUPS = 4
POOL_GROUP_DIM = 64
HY_WIDTH = 256
HY_BANDS = 16
HY_EMB = 1 + 2 * HY_BANDS
HY_EMB_PAD = 128
HY_FFN = 64
HY_TARGET = 1e-2
HY_FAST = 0.3
HY_SLOW = 1.5
NA_HEAD_DIM = 64
NA_WIDTH = 512
NA_HEADS = 8
NA_KH = 8
NA_KW = 16
N_MOD = 6
D_FF = 4096
NORM_EPS = 1e-6
ATT_SCALE = NA_HEAD_DIM ** -0.5

C_POOL = 0
C_HY = POOL_WIDTH
C_Q = C_HY + 3 * HY_WIDTH
C_K = C_Q + NA_WIDTH
C_V = C_K + NA_WIDTH
IN_WIDTH = C_V + NA_WIDTH

MAX_DFT_BLOCK = 512
MASK_VALUE = -1e30
VMEM_LIMIT = 56 * 1024 * 1024
TOKEN_TILE = 512
IN_TOKEN_TILE = 1024
MOD_ROWS = 8
CTX_PER_STEP = 4


def _params(*sem):
    return pltpu.CompilerParams(dimension_semantics=sem, vmem_limit_bytes=VMEM_LIMIT)


def _resident(shape):
    return pl.BlockSpec(shape, lambda *_: (0,) * len(shape), pipeline_mode=pl.Buffered(1))


def _layer(shape, l):
    return pl.BlockSpec((None,) + tuple(shape), lambda *_: (l,) + (0,) * len(shape),
                        pipeline_mode=pl.Buffered(1))


def _mod_spec(l, row0):
    return pl.BlockSpec((None, 1, N_MOD, D_MODEL), lambda b, i: (l, row0 + b, 0, 0))


def _dot(a, b):
    return jnp.dot(a, b, preferred_element_type=F32)


def _dot_nt(a, b):
    return lax.dot_general(a, b, (((1,), (1,)), ((), ())), preferred_element_type=F32)


def _dot_f32(a, b):
    return jnp.dot(a, b, preferred_element_type=F32, precision=lax.Precision.HIGHEST)


def _adaln_kernel(c_ref, w_ref, b_ref, o_ref):
    c = c_ref[...]
    s = c * (1.0 / (1.0 + jnp.exp(-c)))
    o_ref[0] = _dot(s.astype(BF16), w_ref[0].astype(BF16)) + b_ref[0]


def _adaln(conds, w_mod, b_mod):
    tn = 3072
    n = N_MOD * D_MODEL
    return pl.pallas_call(
        _adaln_kernel,
        grid=(DEPTH, n // tn),
        in_specs=[
            pl.BlockSpec((MOD_ROWS, D_MODEL), lambda l, j: (0, 0)),
            pl.BlockSpec((1, D_MODEL, tn), lambda l, j: (l, 0, j)),
            pl.BlockSpec((1, 1, tn), lambda l, j: (l, 0, j)),
        ],
        out_specs=pl.BlockSpec((1, MOD_ROWS, tn), lambda l, j: (l, 0, j)),
        out_shape=jax.ShapeDtypeStruct((DEPTH, MOD_ROWS, n), F32),
        compiler_params=_params("parallel", "parallel"),
        name="adaln",
    )(conds, w_mod, b_mod.reshape(DEPTH, 1, n))


def _modulated_norm(x, g, shift, scale):
    ms = jnp.mean(x * x, axis=-1, keepdims=True)
    return (x * lax.rsqrt(ms + NORM_EPS) * g) * (1.0 + scale) + shift


def _inproj_kernel(x_ref, mod_ref, g_ref, w_ref, qg_ref, kg_ref, *rest):
    up_ref, uh_ref, q_ref, k_ref, v_ref = rest[-5:]
    h = _modulated_norm(x_ref[0], g_ref[...], mod_ref[0, 0:1, :], mod_ref[0, 1:2, :]).astype(BF16)
    up_ref[0] = _dot(h, w_ref[:, C_POOL:C_HY])
    uh_ref[0] = _dot(h, w_ref[:, C_HY:C_Q])

    even = lax.broadcasted_iota(jnp.int32, (x_ref.shape[1], HEAD_PAIR), 1) < NA_HEAD_DIM

    def head_norm(u, g):
        outs = []
        for p in range(NA_HEADS // 2):
            cols = slice(p * HEAD_PAIR, (p + 1) * HEAD_PAIR)
            up = u[:, cols]
            sq = up * up
            ms_even = jnp.sum(jnp.where(even, sq, 0.0), axis=-1, keepdims=True) * (1.0 / NA_HEAD_DIM)
            ms_odd = jnp.sum(jnp.where(even, 0.0, sq), axis=-1, keepdims=True) * (1.0 / NA_HEAD_DIM)
            r = jnp.where(even, lax.rsqrt(ms_even + NORM_EPS), lax.rsqrt(ms_odd + NORM_EPS))
            outs.append(up * r * g[:, cols])
        return jnp.concatenate(outs, axis=-1)

    q = head_norm(_dot(h, w_ref[:, C_Q:C_K]), qg_ref[...])
    q_ref[0] = (q * ATT_SCALE).astype(q_ref.dtype)
    k = head_norm(_dot(h, w_ref[:, C_K:C_V]), kg_ref[...])
    v = _dot(h, w_ref[:, C_V:IN_WIDTH])
    k_ref[...] = k.astype(k_ref.dtype).reshape(k_ref.shape)
    v_ref[...] = v.astype(v_ref.dtype).reshape(v_ref.shape)


def _in_proj(l, x, mod, mod_row0, g1, w_in, qg, kg, cache=None):
    B, L, _ = x.shape
    tm = IN_TOKEN_TILE if cache is None else TOKEN_TILE
    assert L % tm == 0
    tok = lambda w: pl.BlockSpec((1, tm, w), lambda b, i: (b, i, 0))
    in_specs = [
        tok(D_MODEL),
        _mod_spec(l, mod_row0),
        _layer((1, D_MODEL), l),
        _layer((D_MODEL, IN_WIDTH), l),
        _layer((1, NA_WIDTH), l),
        _layer((1, NA_WIDTH), l),
    ]
    args = [x, mod, g1, w_in, qg, kg]
    aliases = {}
    if cache is None:
        kv_spec = tok(NA_WIDTH)
        kv_shape = jax.ShapeDtypeStruct((B, L, NA_WIDTH), BF16)
    else:
        seq, k_cache, v_cache = cache
        assert B == 1 and tm % seq == 0
        per_tile = tm // seq
        kv_spec = pl.BlockSpec((per_tile, 1, seq, NA_WIDTH), lambda b, i: (i, l, 0, 0))
        kv_shape = jax.ShapeDtypeStruct((L // seq, DEPTH, seq, NA_WIDTH), F32)
        if k_cache is not None:
            in_specs += [pl.BlockSpec(memory_space=pl.ANY)] * 2
            aliases = {len(args): 3, len(args) + 1: 4}
            args += [k_cache, v_cache]
    return pl.pallas_call(
        _inproj_kernel,
        grid=(B, L // tm),
        in_specs=in_specs,
        out_specs=[tok(POOL_WIDTH), tok(3 * HY_WIDTH), tok(NA_WIDTH), kv_spec, kv_spec],
        out_shape=[
            jax.ShapeDtypeStruct((B, L, POOL_WIDTH), F32),
            jax.ShapeDtypeStruct((B, L, 3 * HY_WIDTH), F32),
            jax.ShapeDtypeStruct((B, L, NA_WIDTH), BF16),
            kv_shape,
            kv_shape,
        ],
        input_output_aliases=aliases,
        compiler_params=_params("parallel", "parallel"),
        name="in_proj",
    )(*args)


POOL_PAD = 8


def _pool_kernel(u_ref, w_ref, s_ref, o_ref, pad_ref):
    nseq, L = u_ref.shape[0], u_ref.shape[1]
    n = L + 2 * POOL_PAD

    def sh(v, s):
        return pltpu.roll(v, s % n, axis=0)

    lane_p = lax.broadcasted_iota(jnp.int32, (n, POOL_WIDTH), 1)
    t = lax.broadcasted_iota(jnp.int32, (L, POOL_WIDTH), 0)
    lane = lax.broadcasted_iota(jnp.int32, (L, POOL_WIDTH), 1)
    half = jnp.where(lane < 64, 1, jnp.where(lane < 128, 2, jnp.where(lane < 192, 4, 8)))
    lo = jnp.maximum(t - half, 0)
    hi = jnp.minimum(t + half - 1, L - 1)
    inv_cnt = 1.0 / (hi - lo + 1).astype(F32)

    for b in range(nseq):
        u = u_ref[b]
        pad_ref[b, 0:POOL_PAD, :] = jnp.zeros((POOL_PAD, POOL_WIDTH), F32)
        pad_ref[b, POOL_PAD + L:n, :] = jnp.zeros((POOL_PAD, POOL_WIDTH), F32)
        pad_ref[b, POOL_PAD:POOL_PAD + L, :] = u
        a = pad_ref[b]
        p2 = a + sh(a, 1)
        p4 = sh(p2, 1) + sh(p2, -1)
        p8 = sh(p4, 2) + sh(p4, -2)
        p16 = sh(p8, 4) + sh(p8, -4)
        win = jnp.where(lane_p < 64, p2, jnp.where(lane_p < 128, p4, jnp.where(lane_p < 192, p8, p16)))
        pooled = win[POOL_PAD:POOL_PAD + L, :] * inv_cnt - u
        o_ref[b] = (_dot(pooled.astype(BF16), w_ref[...]) * s_ref[...]).astype(o_ref.dtype)


def _pool_mixer(l, u_pool, w_bd, scale, per_step):
    B, L, _ = u_pool.shape
    blk = pl.BlockSpec((per_step, L, POOL_WIDTH), lambda b: (b, 0, 0))
    return pl.pallas_call(
        _pool_kernel,
        grid=(B // per_step,),
        in_specs=[blk, _layer((POOL_WIDTH, POOL_WIDTH), l), _layer((1, POOL_WIDTH), l)],
        out_specs=blk,
        out_shape=jax.ShapeDtypeStruct((B, L, POOL_WIDTH), BF16),
        scratch_shapes=[pltpu.VMEM((per_step, L + 2 * POOL_PAD, POOL_WIDTH), F32)],
        compiler_params=_params("parallel"),
        name="pool_mixer",
    )(u_pool, w_bd, scale)


@functools.lru_cache(maxsize=None)
def _hyena_position_consts(L):
    p = np.arange(L, dtype=np.float64)
    t = p / (L - 1)
    w = 2.0 * math.pi * p / L
    f = np.linspace(1e-4, HY_BANDS - 1, HY_BANDS)
    z = np.zeros((HY_EMB_PAD, L), np.float64)
    z[0] = t
    z[1:1 + HY_BANDS] = np.cos(f[:, None] * w[None, :])
    z[1 + HY_BANDS:HY_EMB] = -np.sin(f[:, None] * w[None, :])
    deltas = np.abs(np.linspace(math.log(HY_TARGET) / HY_SLOW, math.log(HY_TARGET) / HY_FAST, HY_WIDTH))
    decay = np.exp(-t[:, None] * deltas[None, :])
    return z.astype(np.float32), decay.astype(np.float32)


FILTER_BLOCK = 512


def _filter_kernel(z_ref, dec_ref, w1_ref, b1_ref, f1_ref, w2_ref, b2_ref, f2_ref, w3_ref,
                   kf_ref, kb_ref, sum_ref):
    j = pl.program_id(0)
    h = jnp.sin(f1_ref[...] * (_dot_f32(w1_ref[...], z_ref[...]) + b1_ref[...]))
    h = jnp.sin(f2_ref[...] * (_dot_f32(w2_ref[...], h) + b2_ref[...]))
    k = lax.dot_general(h, w3_ref[...], (((0,), (0,)), ((), ())), preferred_element_type=F32,
                        precision=lax.Precision.HIGHEST)
    for dirn, out_ref in enumerate((kf_ref, kb_ref)):
        for o in range(2):
            c0 = (dirn * 2 + o) * HY_WIDTH
            ko = k[:, c0:c0 + HY_WIDTH] * dec_ref[...]
            part = jnp.sum(jnp.abs(ko), axis=0, keepdims=True)

            @pl.when(j == 0)
            def _():
                sum_ref[dirn, o] = part

            @pl.when(j > 0)
            def _():
                sum_ref[dirn, o] = sum_ref[dirn, o] + part

            out_ref[o] = ko


def _hyena_filters(l, L, w1, b1, f1, w2, b2, f2, w3):
    z, decay = _hyena_position_consts(L)
    fb = min(L, FILTER_BLOCK)
    filt = pl.BlockSpec((2, fb, HY_WIDTH), lambda j: (0, j, 0))
    return pl.pallas_call(
        _filter_kernel,
        grid=(L // fb,),
        in_specs=[
            pl.BlockSpec((HY_EMB_PAD, fb), lambda j: (0, j)),
            pl.BlockSpec((fb, HY_WIDTH), lambda j: (j, 0)),
            _layer((HY_FFN, HY_EMB_PAD), l), _layer((HY_FFN, 1), l), _layer((HY_FFN, 1), l),
            _layer((HY_FFN, HY_FFN), l), _layer((HY_FFN, 1), l), _layer((HY_FFN, 1), l),
            _layer((HY_FFN, 4 * HY_WIDTH), l),
        ],
        out_specs=[filt, filt, pl.BlockSpec((2, 2, 1, HY_WIDTH), lambda j: (0, 0, 0, 0))],
        out_shape=[
            jax.ShapeDtypeStruct((2, L, HY_WIDTH), F32),
            jax.ShapeDtypeStruct((2, L, HY_WIDTH), F32),
            jax.ShapeDtypeStruct((2, 2, 1, HY_WIDTH), F32),
        ],
        compiler_params=_params("arbitrary"),
        name="hyena_filters",
    )(jnp.asarray(z), jnp.asarray(decay), w1, b1, f1, w2, b2, f2, w3)


def _dft_block(L):
    return min(L, MAX_DFT_BLOCK)


@functools.lru_cache(maxsize=None)
def _dft_consts(blk):
    n = np.arange(blk, dtype=np.float64)
    j = np.arange(blk, dtype=np.float64)
    ang = 2.0 * math.pi * np.outer(j, n) / (2 * blk)
    fwd = np.concatenate([np.cos(ang), -np.sin(ang)], axis=0)
    fwd[blk] = np.cos(math.pi * n)
    wgt = np.where(j == 0, 1.0, 2.0)[None, :]
    inv = np.concatenate([wgt * np.cos(ang.T), -2.0 * np.sin(ang.T)], axis=1) / (2 * blk)
    inv[:, blk] = np.cos(math.pi * n) / (2 * blk)
    sign = np.tile(np.where(np.arange(blk) % 2 == 0, 1.0, -1.0), 2)[:, None]
    col0 = fwd[:, 0:1].copy()
    return fwd, inv, sign.astype(np.float32), col0.astype(np.float32)


def _split2(a64):
    hi = _np_bf16(a64)
    return hi, _np_bf16(a64 - hi.astype(np.float64))


def _np_bf16(a64):
    return a64.astype(np.float32).astype(BF16)


def _spectra_kernel(kb_ref, kf_ref, sum_ref, fh_ref, fl_ref, sign_ref, signconj_ref, col0_ref,
                    g_ref, ny_ref, prev_ref, next0_ref):
    s = pl.program_id(1)
    per = g_ref.shape[1]
    n, blk = fh_ref.shape
    backward = s < pl.num_programs(1) // 2

    @pl.when(s == 0)
    def _():
        prev_ref[...] = jnp.zeros(prev_ref.shape, F32)
        next0_ref[...] = jnp.zeros(next0_ref.shape, F32)

    inv_norm = 1.0 / (sum_ref[0, 0] + 1e-6)
    row = lax.broadcasted_iota(jnp.int32, (n, HY_WIDTH), 0)

    def dft(a):
        a_hi = a.astype(BF16)
        a_lo = (a - a_hi.astype(F32)).astype(BF16)
        return _dot(fh_ref[...], a_hi) + (_dot(fh_ref[...], a_lo) + _dot(fl_ref[...], a_hi))

    def emit(t, fa, fa_tail):
        g = fa + sign_ref[...] * prev_ref[...]
        ny_ref[0, t] = g[blk:blk + 1, :]
        g_ref[0, t] = jnp.where(row == blk, 0.0, g)
        prev_ref[...] = fa_tail

    @pl.when(backward)
    def _():
        for t in range(per):
            q = per - 1 - t
            b = kb_ref[0, q * blk:(q + 1) * blk, :] * inv_norm
            tail = signconj_ref[...] * (dft(b) - col0_ref[...] * b[0:1, :])
            emit(t, col0_ref[...] * next0_ref[0:1, :] + tail, tail)
            next0_ref[0:1, :] = b[0:1, :]

    @pl.when(jnp.logical_not(backward))
    def _():
        for t in range(per):
            a = kf_ref[0, t * blk:(t + 1) * blk, :] * inv_norm
            fa = dft(a)
            emit(t, fa, fa - col0_ref[...] * a[0:1, :])


SPECTRA_PER_STEP = 4


def _hyena_spectra(kf, kb, sums):
    L = kf.shape[1]
    blk = _dft_block(L)
    n = 2 * blk
    nb = L // blk
    per = min(nb, SPECTRA_PER_STEP)
    half = nb // per
    fwd, _, sign, col0 = _dft_consts(blk)
    fh, fl = _split2(fwd)
    conj = np.where(np.arange(n) > blk, -1.0, 1.0).astype(np.float32)[:, None]
    return pl.pallas_call(
        _spectra_kernel,
        grid=(2, 2 * half),
        in_specs=[
            pl.BlockSpec((1, per * blk, HY_WIDTH), lambda o, s: (o, jnp.maximum(half - 1 - s, 0), 0)),
            pl.BlockSpec((1, per * blk, HY_WIDTH), lambda o, s: (o, jnp.maximum(s - half, 0), 0)),
            pl.BlockSpec((1, 1, 1, HY_WIDTH), lambda o, s: (jnp.where(s < half, 1, 0), o, 0, 0)),
            _resident((n, blk)), _resident((n, blk)),
            _resident((n, 1)), _resident((n, 1)), _resident((n, 1)),
        ],
        out_specs=[
            pl.BlockSpec((1, per, n, HY_WIDTH), lambda o, s: (o, s, 0, 0)),
            pl.BlockSpec((1, per, 1, HY_WIDTH), lambda o, s: (o, s, 0, 0)),
        ],
        out_shape=[
            jax.ShapeDtypeStruct((2, 2 * nb, n, HY_WIDTH), F32),
            jax.ShapeDtypeStruct((2, 2 * nb, 1, HY_WIDTH), F32),
        ],
        scratch_shapes=[pltpu.VMEM((n, HY_WIDTH), F32), pltpu.VMEM((8, HY_WIDTH), F32)],
        compiler_params=_params("parallel", "arbitrary"),
        name="hyena_spectra",
    )(kb, kf, sums, fh, fl, jnp.asarray(sign), jnp.asarray(sign * conj), jnp.asarray(col0))


SPEC_ROWS = 32
NY_ROWS = 16
CONV_ROWS = 64


def _hyena_kernel(u_ref, cw_ref, cb_ref, hb_ref, g_ref, ny_ref, fwd_ref, inv_ref, o_ref,
                  z_ref, gate_ref, zf_ref, y_ref):
    nseq, L = u_ref.shape[0], u_ref.shape[1]
    blk = fwd_ref.shape[1]
    nb = L // blk
    t = lax.broadcasted_iota(jnp.int32, (CONV_ROWS, HY_WIDTH), 0)
    ny_row = lax.broadcasted_iota(jnp.int32, (NY_ROWS, HY_WIDTH), 0) == 0

    for b in range(nseq):
        def short_conv(c, carry, b=b):
            r0 = pl.multiple_of(c * CONV_ROWS, CONV_ROWS)
            rows = pl.ds(r0, CONV_ROWS)
            above = u_ref[b, pl.ds(pl.multiple_of(jnp.maximum(r0 - 8, 0), 8), 8), :]
            below = u_ref[b, pl.ds(pl.multiple_of(jnp.minimum(r0 + CONV_ROWS, L - 8), 8), 8), :]
            first = jnp.where(r0 == 0, 0.0, above[7:8, :])
            last = jnp.where(r0 + CONV_ROWS == L, 0.0, below[0:1, :])
            for part in range(3):
                cols = slice(part * HY_WIDTH, (part + 1) * HY_WIDTH)
                u = u_ref[b, rows, cols]
                prev = jnp.where(t == 0, first[:, cols], pltpu.roll(u, 1, axis=0))
                nxt = jnp.where(t == CONV_ROWS - 1, last[:, cols], pltpu.roll(u, CONV_ROWS - 1, axis=0))
                y = (cb_ref[:, cols] + prev * cw_ref[0:1, cols] + u * cw_ref[1:2, cols]
                     + nxt * cw_ref[2:3, cols])
                if part < 2:
                    gate_ref[part, rows, :] = y
                else:
                    z_ref[rows, :] = y
            return carry

        lax.fori_loop(0, L // CONV_ROWS, short_conv, 0, unroll=(nb == 1))

        for o in range(2):
            for j in range(nb):
                zf_ref[j] = _dot(fwd_ref[...], z_ref[j * blk:(j + 1) * blk, :].astype(BF16))
            bias = hb_ref[o]

            def out_block(i, carry, o=o, bias=bias):
                def spec_rows(c, carry2):
                    r0 = pl.multiple_of(c * SPEC_ROWS, SPEC_ROWS)
                    re = pl.ds(r0, SPEC_ROWS)
                    im = pl.ds(blk + r0, SPEC_ROWS)
                    top = jnp.zeros((SPEC_ROWS, HY_WIDTH), F32)
                    bot = jnp.zeros((SPEC_ROWS, HY_WIDTH), F32)
                    for j in range(nb):
                        d = i - j + nb
                        gr, gi = g_ref[o, d, re, :], g_ref[o, d, im, :]
                        zr, zi = zf_ref[j, re, :], zf_ref[j, im, :]
                        top = top + (gr * zr - gi * zi)
                        bot = bot + (gr * zi + gi * zr)
                    y_ref[re, :] = top.astype(BF16)
                    y_ref[im, :] = bot.astype(BF16)
                    return carry2

                lax.fori_loop(0, blk // SPEC_ROWS, spec_rows, 0, unroll=(nb == 1))
                ny = jnp.zeros((1, HY_WIDTH), F32)
                for j in range(nb):
                    ny = ny + ny_ref[o, i - j + nb] * zf_ref[j, blk:blk + 1, :]
                head = y_ref[blk:blk + NY_ROWS, :].astype(F32)
                y_ref[blk:blk + NY_ROWS, :] = jnp.where(ny_row, ny, head).astype(BF16)
                conv = _dot(inv_ref[...], y_ref[...])
                rows = pl.ds(pl.multiple_of(i * blk, blk), blk)
                z_ref[rows, :] = gate_ref[o, rows, :] * (conv + bias * z_ref[rows, :])
                return carry

            lax.fori_loop(0, nb, out_block, 0, unroll=(nb == 1))
        o_ref[b] = z_ref[...].astype(o_ref.dtype)


def _hyena_mixer(l, u_hy, conv_w, conv_b, hy_bias, spec, nyq, per_step):
    B, L, _ = u_hy.shape
    nd = spec.shape[1]
    blk = _dft_block(L)
    n = 2 * blk
    fwd, inv, _, _ = _dft_consts(blk)
    return pl.pallas_call(
        _hyena_kernel,
        grid=(B // per_step,),
        in_specs=[
            pl.BlockSpec((per_step, L, 3 * HY_WIDTH), lambda b: (b, 0, 0)),
            _layer((3, 3 * HY_WIDTH), l),
            _layer((1, 3 * HY_WIDTH), l),
            _layer((2, 1, HY_WIDTH), l),
            _resident((2, nd, n, HY_WIDTH)),
            _resident((2, nd, 1, HY_WIDTH)),
            _resident((n, blk)),
            _resident((blk, n)),
        ],
        out_specs=pl.BlockSpec((per_step, L, HY_WIDTH), lambda b: (b, 0, 0)),
        out_shape=jax.ShapeDtypeStruct((B, L, HY_WIDTH), BF16),
        scratch_shapes=[
            pltpu.VMEM((L, HY_WIDTH), F32),
            pltpu.VMEM((2, L, HY_WIDTH), F32),
            pltpu.VMEM((L // blk, n, HY_WIDTH), F32),
            pltpu.VMEM((n, HY_WIDTH), BF16),
        ],
        compiler_params=_params("parallel"),
        name="hyena_mixer",
    )(u_hy, conv_w, conv_b, hy_bias, spec, nyq,
      jnp.asarray(fwd, F32).astype(BF16), jnp.asarray(inv, F32).astype(BF16))


HEAD_PAIR = 2 * NA_HEAD_DIM
SOFTMAX_ROWS = 64
PV_CHUNK = 256


def _paired_heads(q, keys, values, bias, s_ref, p_ref):
    M = q.shape[0]
    n_slots = s_ref.shape[0]
    lane = lax.broadcasted_iota(jnp.int32, (M, HEAD_PAIR), 1)

    def own_lanes(ln, e):
        return (ln < NA_HEAD_DIM) if e == 0 else (ln >= NA_HEAD_DIM)

    def scores(h):
        cols = slice((h // 2) * HEAD_PAIR, (h // 2 + 1) * HEAD_PAIR)
        qh = jnp.where(own_lanes(lane, h % 2), q[:, cols], jnp.zeros((M, HEAD_PAIR), BF16))
        off, m = 0, None
        for i, k in enumerate(keys):
            for c0 in range(0, k.shape[0], PV_CHUNK):
                s = _dot_nt(qh, k[c0:c0 + PV_CHUNK, cols])
                if bias is not None and i == 0:
                    s = s + bias(h, slice(c0, c0 + PV_CHUNK))
                s_ref[h % n_slots, :, off + c0:off + c0 + PV_CHUNK] = s
                mi = jnp.max(s, axis=-1, keepdims=True)
                m = mi if m is None else jnp.maximum(m, mi)
            off += k.shape[0]
        return m

    def attend(h, m):
        cols = slice((h // 2) * HEAD_PAIR, (h // 2 + 1) * HEAD_PAIR)
        acc, off = None, 0
        for v in values:
            vl = lax.broadcasted_iota(jnp.int32, (v.shape[0], HEAD_PAIR), 1)
            vh = jnp.where(own_lanes(vl, h % 2), v[:, cols], jnp.ones((v.shape[0], HEAD_PAIR), BF16))
            for c0 in range(0, v.shape[0], PV_CHUNK):
                kc = slice(off + c0, off + c0 + PV_CHUNK)
                for r0 in range(0, M, SOFTMAX_ROWS):
                    rows = slice(r0, r0 + SOFTMAX_ROWS)
                    p_ref[h % 2, rows, kc] = jnp.exp(s_ref[h % n_slots, rows, kc] - m[rows]).astype(BF16)
                part = _dot(p_ref[h % 2, :, kc], vh[c0:c0 + PV_CHUNK])
                acc = part if acc is None else acc + part
            off += v.shape[0]
        return acc / pltpu.roll(acc, NA_HEAD_DIM, axis=1)

    outs, halves = [], []
    ahead = n_slots - 1
    maxima = {h: scores(h) for h in range(ahead)}
    for h in range(NA_HEADS):
        if h + ahead < NA_HEADS:
            maxima[h + ahead] = scores(h + ahead)
        halves.append(attend(h, maxima.pop(h)))
        if h % 2 == 1:
            outs.append(jnp.where(lane < NA_HEAD_DIM, halves[0], halves[1]))
            halves = []
    return jnp.concatenate(outs, axis=-1)


def _ctx_attn_kernel(q_ref, k_ref, v_ref, o_ref, s_ref, p_ref):
    nseq, L = q_ref.shape[0], q_ref.shape[1]
    even = lax.broadcasted_iota(jnp.int32, (L, HEAD_PAIR), 1) < NA_HEAD_DIM
    zero = jnp.zeros((L, HEAD_PAIR), BF16)
    units = [(b, p) for b in range(nseq) for p in range(NA_HEADS // 2)]

    def scores(n):
        b, p = units[n]
        cols = slice(p * HEAD_PAIR, (p + 1) * HEAD_PAIR)
        qp = q_ref[b, :, cols]
        q2 = jnp.concatenate([jnp.where(even, qp, zero), jnp.where(even, zero, qp)], axis=0)
        s_ref[n % 2] = _dot_nt(q2, k_ref[b, 0, :, cols].astype(BF16))

    def attend(n):
        b, p = units[n]
        cols = slice(p * HEAD_PAIR, (p + 1) * HEAD_PAIR)
        for r0 in range(0, 2 * L, SOFTMAX_ROWS):
            rows = slice(r0, r0 + SOFTMAX_ROWS)
            s = s_ref[n % 2, rows, :]
            e = jnp.exp(s - jnp.max(s, axis=-1, keepdims=True))
            p_ref[n % 2, rows, :] = (e * (1.0 / jnp.sum(e, axis=-1, keepdims=True))).astype(BF16)
        acc = _dot(p_ref[n % 2], v_ref[b, 0, :, cols].astype(BF16))
        o_ref[b, :, cols] = jnp.where(even, acc[:L], acc[L:]).astype(o_ref.dtype)

    scores(0)
    for n in range(len(units)):
        if n + 1 < len(units):
            scores(n + 1)
        attend(n)


def _context_attention(q, k_cache, v_cache, layer, per_step):
    B, L, _ = q.shape
    spec = pl.BlockSpec((per_step, L, NA_WIDTH), lambda b: (b, 0, 0))
    kv = pl.BlockSpec((per_step, 1, L, NA_WIDTH), lambda b: (b, layer, 0, 0))
    return pl.pallas_call(
        _ctx_attn_kernel,
        grid=(B // per_step,),
        in_specs=[spec, kv, kv],
        out_specs=spec,
        out_shape=jax.ShapeDtypeStruct((B, L, NA_WIDTH), BF16),
        scratch_shapes=[pltpu.VMEM((2, 2 * L, L), F32), pltpu.VMEM((2, 2 * L, L), BF16)],
        compiler_params=_params("parallel"),
        name="context_attention",
    )(q, k_cache, v_cache)


N_DR = 2 * NA_KH - 1
N_DC = 2 * NA_KW - 1
NA_RB = 4
NA_WIN = NA_RB + NA_KH
WIN_KEYS = NA_WIN * GRID_W
NA_SCORE_SLOTS = 2
NA_BLOCKS_PER_STEP = 2


def _bias_kernel(rb_ref, o_ref):
    h = pl.program_id(0)
    qc = lax.broadcasted_iota(jnp.int32, (GRID_W, GRID_W), 0)
    kc = lax.broadcasted_iota(jnp.int32, (GRID_W, GRID_W), 1)
    dc = jnp.clip(kc - qc, -(NA_KW - 1), NA_KW - 1) + (NA_KW - 1)
    col_start = jnp.clip(qc - NA_KW // 2, 0, GRID_W - NA_KW)
    visible = (kc >= col_start) & (kc < col_start + NA_KW)
    blocks = []
    for dr in range(N_DR):
        acc = jnp.zeros((GRID_W, GRID_W), F32)
        for d in range(N_DC):
            acc = jnp.where(dc == d, rb_ref[h * N_DR + dr, d], acc)
        blocks.append(jnp.where(visible, acc, MASK_VALUE))
    hidden = jnp.full((GRID_W, GRID_W), MASK_VALUE, F32)
    for variant, (first, base) in enumerate(((0, NA_KH - 1), (None, NA_KH // 2 - 1), (NA_RB, -1))):
        for a in range(NA_RB):
            lo = a if first is None else first
            row = [blocks[i - a + base] if lo <= i < lo + NA_KH else hidden for i in range(NA_WIN)]
            o_ref[variant, 0, a * GRID_W:(a + 1) * GRID_W, :] = jnp.concatenate(row, axis=-1)


def _bias_table(rel_bias_l):
    return pl.pallas_call(
        _bias_kernel,
        grid=(NA_HEADS,),
        in_specs=[pl.BlockSpec(memory_space=pltpu.SMEM)],
        out_specs=pl.BlockSpec((3, 1, NA_RB * GRID_W, WIN_KEYS), lambda h: (0, h, 0, 0)),
        out_shape=jax.ShapeDtypeStruct((3, NA_HEADS, NA_RB * GRID_W, WIN_KEYS), F32),
        compiler_params=_params("parallel"),
        name="na_bias_table",
    )(rel_bias_l.reshape(NA_HEADS * N_DR, N_DC))


def _na_kernel(q_ref, k_ref, v_ref, kc_ref, vc_ref, *rest):
    tab_refs, (o_ref, s_ref, p_ref) = rest[:-3], rest[-3:]
    rows = k_ref.shape[1] // GRID_W
    kc = kc_ref[0, 0].astype(BF16)
    vc = vc_ref[0, 0].astype(BF16)
    for sub, tab_ref in enumerate(tab_refs):
        rb = pl.program_id(1) * len(tab_refs) + sub
        first_row = jnp.clip(rb * NA_RB - NA_KH // 2, 0, rows - NA_WIN)
        start = pl.multiple_of(first_row * GRID_W, NA_RB * GRID_W)
        keys = [k_ref[0, pl.ds(start, WIN_KEYS), :], kc]
        values = [v_ref[0, pl.ds(start, WIN_KEYS), :], vc]
        q_rows = slice(sub * NA_RB * GRID_W, (sub + 1) * NA_RB * GRID_W)
        out = _paired_heads(q_ref[0, q_rows, :], keys, values,
                            lambda h, cols, tab_ref=tab_ref: tab_ref[0, h, :, cols], s_ref, p_ref)
        o_ref[0, q_rows, :] = out.astype(o_ref.dtype)


def _neighbourhood_attention(q, k, v, cache_k, cache_v, layer, table):
    B, L, _ = q.shape
    rows = L // GRID_W
    assert rows % (NA_RB * NA_BLOCKS_PER_STEP) == 0 and rows >= NA_WIN + NA_RB
    nrb = rows // NA_RB
    Lc = cache_k.shape[2]
    seq = pl.BlockSpec((1, L, NA_WIDTH), lambda b, r: (b, 0, 0))
    blk = pl.BlockSpec((1, NA_BLOCKS_PER_STEP * NA_RB * GRID_W, NA_WIDTH), lambda b, r: (b, r, 0))
    ctx = pl.BlockSpec((1, 1, Lc, NA_WIDTH), lambda b, r: (b, layer, 0, 0))
    placement = lambda rb: jnp.where(rb == 0, 0, jnp.where(rb == nrb - 1, 2, 1))
    tabs = [pl.BlockSpec((1, NA_HEADS, NA_RB * GRID_W, WIN_KEYS),
                         lambda b, r, sub=sub: (placement(r * NA_BLOCKS_PER_STEP + sub), 0, 0, 0))
            for sub in range(NA_BLOCKS_PER_STEP)]
    return pl.pallas_call(
        _na_kernel,
        grid=(B, nrb // NA_BLOCKS_PER_STEP),
        in_specs=[blk, seq, seq, ctx, ctx] + tabs,
        out_specs=blk,
        out_shape=jax.ShapeDtypeStruct((B, L, NA_WIDTH), BF16),
        scratch_shapes=[pltpu.VMEM((NA_SCORE_SLOTS, NA_RB * GRID_W, WIN_KEYS + Lc), F32),
                        pltpu.VMEM((2, NA_RB * GRID_W, WIN_KEYS + Lc), BF16)],
        compiler_params=_params("parallel", "arbitrary"),
        name="neighbourhood_attention",
    )(q, k, v, cache_k, cache_v, *([table] * NA_BLOCKS_PER_STEP))


FF_CHUNK = 1024


def _out_mlp_kernel(x_ref, yp_ref, yh_ref, ya_ref, mod_ref, g_ref, wo_ref, wu_ref, wd_ref, o_ref):
    y = (_dot(yp_ref[0], wo_ref[0:POOL_WIDTH, :])
         + _dot(yh_ref[0], wo_ref[POOL_WIDTH:POOL_WIDTH + HY_WIDTH, :])
         + _dot(ya_ref[0], wo_ref[POOL_WIDTH + HY_WIDTH:D_MODEL, :]))
    x = x_ref[0] + mod_ref[0, 2:3, :] * y
    h = _modulated_norm(x, g_ref[...], mod_ref[0, 3:4, :], mod_ref[0, 4:5, :]).astype(BF16)
    acc = jnp.zeros(x.shape, F32)
    for c in range(D_FF // FF_CHUNK):
        cols = slice(c * FF_CHUNK, (c + 1) * FF_CHUNK)
        a = jnp.maximum(_dot(h, wu_ref[:, cols]), 0.0)
        acc = acc + _dot((a * a).astype(BF16), wd_ref[cols, :])
    o_ref[0] = x + mod_ref[0, 5:6, :] * acc


def _out_mlp(l, x, y_pool, y_hy, y_na, mod, mod_row0, g2, w_out, w_up, w_down):
    B, L, _ = x.shape
    tm = TOKEN_TILE
    assert L % tm == 0
    tok = lambda w: pl.BlockSpec((1, tm, w), lambda b, i: (b, i, 0))
    return pl.pallas_call(
        _out_mlp_kernel,
        grid=(B, L // tm),
        in_specs=[
            tok(D_MODEL), tok(POOL_WIDTH), tok(HY_WIDTH), tok(NA_WIDTH),
            _mod_spec(l, mod_row0),
            _layer((1, D_MODEL), l),
            _layer((D_MODEL, D_MODEL), l),
            _layer((D_MODEL, D_FF), l),
            _layer((D_FF, D_MODEL), l),
        ],
        out_specs=tok(D_MODEL),
        out_shape=jax.ShapeDtypeStruct((B, L, D_MODEL), F32),
        compiler_params=_params("parallel", "parallel"),
        name="out_mlp",
    )(x, y_pool, y_hy, y_na, mod, g2, w_out, w_up, w_down)


def _block_diag(blocks):
    *lead, g, c, d = blocks.shape
    eye = jnp.eye(g, dtype=blocks.dtype)
    out = blocks[..., :, :, None, :] * eye[:, None, :, None]
    return out.reshape(*lead, g * c, g * d)


def kernel(x_prompt, x_sample, cache_k, cache_v, c, c_ctx, norm1_g, norm2_g, w_mod, b_mod, w_in,
           pool_w, pool_scale, hy_conv_w, hy_conv_b, hy_f1_w, hy_f1_b, hy_f1_freq, hy_f2_w, hy_f2_b,
           hy_f2_freq, hy_f3_w, hy_bias, q_norm_g, k_norm_g, rel_bias, w_out, w_up, w_down):
    n_prompt, seq, _ = x_prompt.shape
    n_dec, dec_seq, _ = x_sample.shape

    conds = jnp.zeros((MOD_ROWS, D_MODEL), F32).at[0].set(c_ctx).at[1:1 + n_dec].set(c)
    mod = _adaln(conds, w_mod, b_mod).reshape(DEPTH, MOD_ROWS, N_MOD, D_MODEL)

    cache_k = cache_k.reshape(n_dec, DEPTH, -1, NA_WIDTH)
    cache_v = cache_v.reshape(n_dec, DEPTH, -1, NA_WIDTH)

    g1 = norm1_g.reshape(DEPTH, 1, D_MODEL)
    g2 = norm2_g.reshape(DEPTH, 1, D_MODEL)
    w_in_b, w_out_b, w_up_b, w_down_b = (w.astype(BF16) for w in (w_in, w_out, w_up, w_down))
    pool_bd = _block_diag(pool_w).astype(BF16)
    pool_s = pool_scale.reshape(DEPTH, 1, POOL_WIDTH)
    qg = jnp.tile(q_norm_g, (1, NA_HEADS)).reshape(DEPTH, 1, NA_WIDTH)
    kg = jnp.tile(k_norm_g, (1, NA_HEADS)).reshape(DEPTH, 1, NA_WIDTH)
    col = lambda a: a.reshape(DEPTH, HY_FFN, 1)
    f1_w = jnp.zeros((DEPTH, HY_FFN, HY_EMB_PAD), F32).at[:, :, :HY_EMB].set(hy_f1_w.transpose(0, 2, 1))
    filt_args = (f1_w, col(hy_f1_b), col(hy_f1_freq), hy_f2_w.transpose(0, 2, 1), col(hy_f2_b),
                 col(hy_f2_freq), hy_f3_w)
    conv_b = hy_conv_b.reshape(DEPTH, 1, 3 * HY_WIDTH)
    hy_b = hy_bias.reshape(DEPTH, 2, 1, HY_WIDTH)

    xp = x_prompt.reshape(1, n_prompt * seq, D_MODEL)
    xs = x_sample
    new_k = new_v = None
    per_seq = lambda a: a.reshape(n_prompt, seq, a.shape[-1])
    flat = lambda a: a.reshape(1, n_prompt * seq, a.shape[-1])
    for l in range(DEPTH):
        u_pool, u_hy, q, new_k, new_v = _in_proj(l, xp, mod, 0, g1, w_in_b, qg, kg,
                                                 cache=(seq, new_k, new_v))
        spec, nyq = _hyena_spectra(*_hyena_filters(l, seq, *filt_args))
        y_pool = _pool_mixer(l, per_seq(u_pool), pool_bd, pool_s, CTX_PER_STEP)
        y_hy = _hyena_mixer(l, per_seq(u_hy), hy_conv_w, conv_b, hy_b, spec, nyq, CTX_PER_STEP)
        y_na = _context_attention(per_seq(q), new_k, new_v, l, CTX_PER_STEP)
        xp = _out_mlp(l, xp, flat(y_pool), flat(y_hy), flat(y_na), mod, 0, g2, w_out_b, w_up_b, w_down_b)

        u_pool, u_hy, q, k, v = _in_proj(l, xs, mod, 1, g1, w_in_b, qg, kg)
        spec, nyq = _hyena_spectra(*_hyena_filters(l, dec_seq, *filt_args))
        y_pool = _pool_mixer(l, u_pool, pool_bd, pool_s, 1)
        y_hy = _hyena_mixer(l, u_hy, hy_conv_w, conv_b, hy_b, spec, nyq, 1)
        y_na = _neighbourhood_attention(q, k, v, cache_k, cache_v, l, _bias_table(rel_bias[l]))
        xs = _out_mlp(l, xs, y_pool, y_hy, y_na, mod, 1, g2, w_out_b, w_up_b, w_down_b)

    shape_kv = (n_prompt, DEPTH, seq, NA_HEADS, NA_HEAD_DIM)
    return (xp.reshape(n_prompt, seq, D_MODEL), xs, new_k.reshape(shape_kv), new_v.reshape(shape_kv))
```

```python
import functools
import math

import numpy as np
import jax
import jax.numpy as jnp
from jax import lax
from jax.experimental import pallas as pl
from jax.experimental.pallas import tpu as pltpu

F32 = jnp.float32
BF16 = jnp.bfloat16

D_MODEL = 1024
DEPTH = 2
GRID_W = 64
POOL_WIDTH = 256
POOL_GROUPS = 4
POOL_GROUP_DIM = 64
HY_WIDTH = 256
HY_BANDS = 16
HY_EMB = 1 + 2 * HY_BANDS
HY_EMB_PAD = 128
HY_FFN = 64
HY_TARGET = 1e-2
HY_FAST = 0.3
HY_SLOW = 1.5
NA_HEAD_DIM = 64
NA_WIDTH = 512
NA_HEADS = 8
NA_KH = 8
NA_KW = 16
N_MOD = 6
D_FF = 4096
NORM_EPS = 1e-6
ATT_SCALE = NA_HEAD_DIM ** -0.5

C_POOL = 0
C_HY = POOL_WIDTH
C_Q = C_HY + 3 * HY_WIDTH
C_K = C_Q + NA_WIDTH
C_V = C_K + NA_WIDTH
IN_WIDTH = C_V + NA_WIDTH

MAX_DFT_BLOCK = 512
MASK_VALUE = -1e30
VMEM_LIMIT = 56 * 1024 * 1024
TOKEN_TILE = 512
IN_TOKEN_TILE = 1024
MOD_ROWS = 8
CTX_PER_STEP = 4


def _params(*sem):
    return pltpu.CompilerParams(dimension_semantics=sem, vmem_limit_bytes=VMEM_LIMIT)


def _resident(shape):
    return pl.BlockSpec(shape, lambda *_: (0,) * len(shape), pipeline_mode=pl.Buffered(1))


def _layer(shape, l):
    return pl.BlockSpec((None,) + tuple(shape), lambda *_: (l,) + (0,) * len(shape),
                        pipeline_mode=pl.Buffered(1))


def _mod_spec(l, row0):
    return pl.BlockSpec((None, 1, N_MOD, D_MODEL), lambda b, i: (l, row0 + b, 0, 0))


def _dot(a, b):
    return jnp.dot(a, b, preferred_element_type=F32)


def _dot_nt(a, b):
    return lax.dot_general(a, b, (((1,), (1,)), ((), ())), preferred_element_type=F32)


def _dot_f32(a, b):
    return jnp.dot(a, b, preferred_element_type=F32, precision=lax.Precision.HIGHEST)


def _adaln_kernel(c_ref, w_ref, b_ref, o_ref):
    c = c_ref[...]
    s = c * (1.0 / (1.0 + jnp.exp(-c)))
    o_ref[0] = _dot(s.astype(BF16), w_ref[0].astype(BF16)) + b_ref[0]


def _adaln(conds, w_mod, b_mod):
    tn = 1536
    n = N_MOD * D_MODEL
    return pl.pallas_call(
        _adaln_kernel,
        grid=(DEPTH, n // tn),
        in_specs=[
            pl.BlockSpec((MOD_ROWS, D_MODEL), lambda l, j: (0, 0)),
            pl.BlockSpec((1, D_MODEL, tn), lambda l, j: (l, 0, j)),
            pl.BlockSpec((1, 1, tn), lambda l, j: (l, 0, j)),
        ],
        out_specs=pl.BlockSpec((1, MOD_ROWS, tn), lambda l, j: (l, 0, j)),
        out_shape=jax.ShapeDtypeStruct((DEPTH, MOD_ROWS, n), F32),
        compiler_params=_params("parallel", "parallel"),
        name="adaln",
    )(conds, w_mod, b_mod.reshape(DEPTH, 1, n))


def _modulated_norm(x, g, shift, scale):
    ms = jnp.mean(x * x, axis=-1, keepdims=True)
    return (x * lax.rsqrt(ms + NORM_EPS) * g) * (1.0 + scale) + shift


def _inproj_kernel(x_ref, mod_ref, g_ref, w_ref, qg_ref, kg_ref, *rest):
    up_ref, uh_ref, q_ref, k_ref, v_ref = rest[-5:]
    h = _modulated_norm(x_ref[0], g_ref[...], mod_ref[0, 0:1, :], mod_ref[0, 1:2, :]).astype(BF16)
    up_ref[0] = _dot(h, w_ref[:, C_POOL:C_HY])
    uh_ref[0] = _dot(h, w_ref[:, C_HY:C_Q])

    even = lax.broadcasted_iota(jnp.int32, (x_ref.shape[1], HEAD_PAIR), 1) < NA_HEAD_DIM

    def head_norm(u, g):
        outs = []
        for p in range(NA_HEADS // 2):
            cols = slice(p * HEAD_PAIR, (p + 1) * HEAD_PAIR)
            up = u[:, cols]
            sq = up * up
            ms_even = jnp.sum(jnp.where(even, sq, 0.0), axis=-1, keepdims=True) * (1.0 / NA_HEAD_DIM)
            ms_odd = jnp.sum(jnp.where(even, 0.0, sq), axis=-1, keepdims=True) * (1.0 / NA_HEAD_DIM)
            r = jnp.where(even, lax.rsqrt(ms_even + NORM_EPS), lax.rsqrt(ms_odd + NORM_EPS))
            outs.append(up * r * g[:, cols])
        return jnp.concatenate(outs, axis=-1)

    q = head_norm(_dot(h, w_ref[:, C_Q:C_K]), qg_ref[...])
    q_ref[0] = (q * ATT_SCALE).astype(q_ref.dtype)
    k = head_norm(_dot(h, w_ref[:, C_K:C_V]), kg_ref[...])
    v = _dot(h, w_ref[:, C_V:IN_WIDTH])
    k_ref[...] = k.astype(k_ref.dtype).reshape(k_ref.shape)
    v_ref[...] = v.astype(v_ref.dtype).reshape(v_ref.shape)


def _in_proj(l, x, mod, mod_row0, g1, w_in, qg, kg, cache=None):
    B, L, _ = x.shape
    tm = IN_TOKEN_TILE
    assert L % tm == 0
    tok = lambda w: pl.BlockSpec((1, tm, w), lambda b, i: (b, i, 0))
    in_specs = [
        tok(D_MODEL),
        _mod_spec(l, mod_row0),
        _layer((1, D_MODEL), l),
        _layer((D_MODEL, IN_WIDTH), l),
        _layer((1, NA_WIDTH), l),
        _layer((1, NA_WIDTH), l),
    ]
    args = [x, mod, g1, w_in, qg, kg]
    aliases = {}
    if cache is None:
        kv_spec = tok(NA_WIDTH)
        kv_shape = jax.ShapeDtypeStruct((B, L, NA_WIDTH), BF16)
    else:
        seq, k_cache, v_cache = cache
        assert B == 1 and tm % seq == 0
        per_tile = tm // seq
        kv_spec = pl.BlockSpec((per_tile, 1, seq, NA_WIDTH), lambda b, i: (i, l, 0, 0))
        kv_shape = jax.ShapeDtypeStruct((L // seq, DEPTH, seq, NA_WIDTH), F32)
        if k_cache is not None:
            in_specs += [pl.BlockSpec(memory_space=pl.ANY)] * 2
            aliases = {len(args): 3, len(args) + 1: 4}
            args += [k_cache, v_cache]
    return pl.pallas_call(
        _inproj_kernel,
        grid=(B, L // tm),
        in_specs=in_specs,
        out_specs=[tok(POOL_WIDTH), tok(3 * HY_WIDTH), tok(NA_WIDTH), kv_spec, kv_spec],
        out_shape=[
            jax.ShapeDtypeStruct((B, L, POOL_WIDTH), F32),
            jax.ShapeDtypeStruct((B, L, 3 * HY_WIDTH), F32),
            jax.ShapeDtypeStruct((B, L, NA_WIDTH), BF16),
            kv_shape,
            kv_shape,
        ],
        input_output_aliases=aliases,
        compiler_params=_params("parallel", "parallel"),
        name="in_proj",
    )(*args)


POOL_PAD = 8


def _pool_kernel(u_ref, w_ref, s_ref, o_ref, pad_ref):
    nseq, L = u_ref.shape[0], u_ref.shape[1]
    n = L + 2 * POOL_PAD

    def sh(v, s):
        return pltpu.roll(v, s % n, axis=0)

    lane_p = lax.broadcasted_iota(jnp.int32, (n, POOL_WIDTH), 1)
    t = lax.broadcasted_iota(jnp.int32, (L, POOL_WIDTH), 0)
    lane = lax.broadcasted_iota(jnp.int32, (L, POOL_WIDTH), 1)
    half = jnp.where(lane < 64, 1, jnp.where(lane < 128, 2, jnp.where(lane < 192, 4, 8)))
    lo = jnp.maximum(t - half, 0)
    hi = jnp.minimum(t + half - 1, L - 1)
    inv_cnt = 1.0 / (hi - lo + 1).astype(F32)

    for b in range(nseq):
        u = u_ref[b]
        pad_ref[b, 0:POOL_PAD, :] = jnp.zeros((POOL_PAD, POOL_WIDTH), F32)
        pad_ref[b, POOL_PAD + L:n, :] = jnp.zeros((POOL_PAD, POOL_WIDTH), F32)
        pad_ref[b, POOL_PAD:POOL_PAD + L, :] = u
        a = pad_ref[b]
        p2 = a + sh(a, 1)
        p4 = sh(p2, 1) + sh(p2, -1)
        p8 = sh(p4, 2) + sh(p4, -2)
        p16 = sh(p8, 4) + sh(p8, -4)
        win = jnp.where(lane_p < 64, p2, jnp.where(lane_p < 128, p4, jnp.where(lane_p < 192, p8, p16)))
        pooled = win[POOL_PAD:POOL_PAD + L, :] * inv_cnt - u
        o_ref[b] = (_dot(pooled.astype(BF16), w_ref[...]) * s_ref[...]).astype(o_ref.dtype)


def _pool_mixer(l, u_pool, w_bd, scale, per_step):
    B, L, _ = u_pool.shape
    blk = pl.BlockSpec((per_step, L, POOL_WIDTH), lambda b: (b, 0, 0))
    return pl.pallas_call(
        _pool_kernel,
        grid=(B // per_step,),
        in_specs=[blk, _layer((POOL_WIDTH, POOL_WIDTH), l), _layer((1, POOL_WIDTH), l)],
        out_specs=blk,
        out_shape=jax.ShapeDtypeStruct((B, L, POOL_WIDTH), BF16),
        scratch_shapes=[pltpu.VMEM((per_step, L + 2 * POOL_PAD, POOL_WIDTH), F32)],
        compiler_params=_params("parallel"),
        name="pool_mixer",
    )(u_pool, w_bd, scale)


@functools.lru_cache(maxsize=None)
def _hyena_position_consts(L):
    p = np.arange(L, dtype=np.float64)
    t = p / (L - 1)
    w = 2.0 * math.pi * p / L
    f = np.linspace(1e-4, HY_BANDS - 1, HY_BANDS)
    z = np.zeros((HY_EMB_PAD, L), np.float64)
    z[0] = t
    z[1:1 + HY_BANDS] = np.cos(f[:, None] * w[None, :])
    z[1 + HY_BANDS:HY_EMB] = -np.sin(f[:, None] * w[None, :])
    deltas = np.abs(np.linspace(math.log(HY_TARGET) / HY_SLOW, math.log(HY_TARGET) / HY_FAST, HY_WIDTH))
    decay = np.exp(-t[:, None] * deltas[None, :])
    return z.astype(np.float32), decay.astype(np.float32)


FILTER_BLOCK = 512


def _filter_kernel(z_ref, dec_ref, w1_ref, b1_ref, f1_ref, w2_ref, b2_ref, f2_ref, w3_ref,
                   kf_ref, kb_ref, sum_ref):
    j = pl.program_id(0)
    h = jnp.sin(f1_ref[...] * (_dot_f32(w1_ref[...], z_ref[...]) + b1_ref[...]))
    h = jnp.sin(f2_ref[...] * (_dot_f32(w2_ref[...], h) + b2_ref[...]))
    k = lax.dot_general(h, w3_ref[...], (((0,), (0,)), ((), ())), preferred_element_type=F32,
                        precision=lax.Precision.HIGHEST)
    for dirn, out_ref in enumerate((kf_ref, kb_ref)):
        for o in range(2):
            c0 = (dirn * 2 + o) * HY_WIDTH
            ko = k[:, c0:c0 + HY_WIDTH] * dec_ref[...]
            part = jnp.sum(jnp.abs(ko), axis=0, keepdims=True)

            @pl.when(j == 0)
            def _():
                sum_ref[dirn, o] = part

            @pl.when(j > 0)
            def _():
                sum_ref[dirn, o] = sum_ref[dirn, o] + part

            out_ref[o] = ko


def _hyena_filters(l, L, w1, b1, f1, w2, b2, f2, w3):
    z, decay = _hyena_position_consts(L)
    fb = min(L, FILTER_BLOCK)
    filt = pl.BlockSpec((2, fb, HY_WIDTH), lambda j: (0, j, 0))
    return pl.pallas_call(
        _filter_kernel,
        grid=(L // fb,),
        in_specs=[
            pl.BlockSpec((HY_EMB_PAD, fb), lambda j: (0, j)),
            pl.BlockSpec((fb, HY_WIDTH), lambda j: (j, 0)),
            _layer((HY_FFN, HY_EMB_PAD), l), _layer((HY_FFN, 1), l), _layer((HY_FFN, 1), l),
            _layer((HY_FFN, HY_FFN), l), _layer((HY_FFN, 1), l), _layer((HY_FFN, 1), l),
            _layer((HY_FFN, 4 * HY_WIDTH), l),
        ],
        out_specs=[filt, filt, pl.BlockSpec((2, 2, 1, HY_WIDTH), lambda j: (0, 0, 0, 0))],
        out_shape=[
            jax.ShapeDtypeStruct((2, L, HY_WIDTH), F32),
            jax.ShapeDtypeStruct((2, L, HY_WIDTH), F32),
            jax.ShapeDtypeStruct((2, 2, 1, HY_WIDTH), F32),
        ],
        compiler_params=_params("arbitrary"),
        name="hyena_filters",
    )(jnp.asarray(z), jnp.asarray(decay), w1, b1, f1, w2, b2, f2, w3)


def _dft_block(L):
    return min(L, MAX_DFT_BLOCK)


@functools.lru_cache(maxsize=None)
def _dft_consts(blk):
    n = np.arange(blk, dtype=np.float64)
    j = np.arange(blk, dtype=np.float64)
    ang = 2.0 * math.pi * np.outer(j, n) / (2 * blk)
    fwd = np.concatenate([np.cos(ang), -np.sin(ang)], axis=0)
    fwd[blk] = np.cos(math.pi * n)
    wgt = np.where(j == 0, 1.0, 2.0)[None, :]
    inv = np.concatenate([wgt * np.cos(ang.T), -2.0 * np.sin(ang.T)], axis=1) / (2 * blk)
    inv[:, blk] = np.cos(math.pi * n) / (2 * blk)
    sign = np.tile(np.where(np.arange(blk) % 2 == 0, 1.0, -1.0), 2)[:, None]
    col0 = fwd[:, 0:1].copy()
    return fwd, inv, sign.astype(np.float32), col0.astype(np.float32)


def _split2(a64):
    hi = _np_bf16(a64)
    return hi, _np_bf16(a64 - hi.astype(np.float64))


def _np_bf16(a64):
    return a64.astype(np.float32).astype(BF16)


def _spectra_kernel(kb_ref, kf_ref, sum_ref, fh_ref, fl_ref, sign_ref, signconj_ref, col0_ref,
                    g_ref, ny_ref, prev_ref, next0_ref):
    s = pl.program_id(1)
    per = g_ref.shape[1]
    n, blk = fh_ref.shape
    backward = s < pl.num_programs(1) // 2

    @pl.when(s == 0)
    def _():
        prev_ref[...] = jnp.zeros(prev_ref.shape, F32)
        next0_ref[...] = jnp.zeros(next0_ref.shape, F32)

    inv_norm = 1.0 / (sum_ref[0, 0] + 1e-6)
    row = lax.broadcasted_iota(jnp.int32, (n, HY_WIDTH), 0)

    def dft(a):
        a_hi = a.astype(BF16)
        a_lo = (a - a_hi.astype(F32)).astype(BF16)
        return _dot(fh_ref[...], a_hi) + (_dot(fh_ref[...], a_lo) + _dot(fl_ref[...], a_hi))

    def emit(t, fa, fa_tail):
        g = fa + sign_ref[...] * prev_ref[...]
        ny_ref[0, t] = g[blk:blk + 1, :]
        g_ref[0, t] = jnp.where(row == blk, 0.0, g)
        prev_ref[...] = fa_tail

    @pl.when(backward)
    def _():
        for t in range(per):
            q = per - 1 - t
            b = kb_ref[0, q * blk:(q + 1) * blk, :] * inv_norm
            tail = signconj_ref[...] * (dft(b) - col0_ref[...] * b[0:1, :])
            emit(t, col0_ref[...] * next0_ref[0:1, :] + tail, tail)
            next0_ref[0:1, :] = b[0:1, :]

    @pl.when(jnp.logical_not(backward))
    def _():
        for t in range(per):
            a = kf_ref[0, t * blk:(t + 1) * blk, :] * inv_norm
            fa = dft(a)
            emit(t, fa, fa - col0_ref[...] * a[0:1, :])


SPECTRA_PER_STEP = 4


def _hyena_spectra(kf, kb, sums):
    L = kf.shape[1]
    blk = _dft_block(L)
    n = 2 * blk
    nb = L // blk
    per = min(nb, SPECTRA_PER_STEP)
    half = nb // per
    fwd, _, sign, col0 = _dft_consts(blk)
    fh, fl = _split2(fwd)
    conj = np.where(np.arange(n) > blk, -1.0, 1.0).astype(np.float32)[:, None]
    return pl.pallas_call(
        _spectra_kernel,
        grid=(2, 2 * half),
        in_specs=[
            pl.BlockSpec((1, per * blk, HY_WIDTH), lambda o, s: (o, jnp.maximum(half - 1 - s, 0), 0)),
            pl.BlockSpec((1, per * blk, HY_WIDTH), lambda o, s: (o, jnp.maximum(s - half, 0), 0)),
            pl.BlockSpec((1, 1, 1, HY_WIDTH), lambda o, s: (jnp.where(s < half, 1, 0), o, 0, 0)),
            _resident((n, blk)), _resident((n, blk)),
            _resident((n, 1)), _resident((n, 1)), _resident((n, 1)),
        ],
        out_specs=[
            pl.BlockSpec((1, per, n, HY_WIDTH), lambda o, s: (o, s, 0, 0)),
            pl.BlockSpec((1, per, 1, HY_WIDTH), lambda o, s: (o, s, 0, 0)),
        ],
        out_shape=[
            jax.ShapeDtypeStruct((2, 2 * nb, n, HY_WIDTH), F32),
            jax.ShapeDtypeStruct((2, 2 * nb, 1, HY_WIDTH), F32),
        ],
        scratch_shapes=[pltpu.VMEM((n, HY_WIDTH), F32), pltpu.VMEM((8, HY_WIDTH), F32)],
        compiler_params=_params("parallel", "arbitrary"),
        name="hyena_spectra",
    )(kb, kf, sums, fh, fl, jnp.asarray(sign), jnp.asarray(sign * conj), jnp.asarray(col0))


SPEC_ROWS = 32
NY_ROWS = 16
CONV_ROWS = 64


def _hyena_kernel(u_ref, cw_ref, cb_ref, hb_ref, g_ref, ny_ref, fwd_ref, inv_ref, o_ref,
                  z_ref, gate_ref, zf_ref, y_ref):
    nseq, L = u_ref.shape[0], u_ref.shape[1]
    blk = fwd_ref.shape[1]
    nb = L // blk
    t = lax.broadcasted_iota(jnp.int32, (CONV_ROWS, HY_WIDTH), 0)
    ny_row = lax.broadcasted_iota(jnp.int32, (NY_ROWS, HY_WIDTH), 0) == 0

    for b in range(nseq):
        def short_conv(c, carry, b=b):
            r0 = pl.multiple_of(c * CONV_ROWS, CONV_ROWS)
            rows = pl.ds(r0, CONV_ROWS)
            above = u_ref[b, pl.ds(pl.multiple_of(jnp.maximum(r0 - 8, 0), 8), 8), :]
            below = u_ref[b, pl.ds(pl.multiple_of(jnp.minimum(r0 + CONV_ROWS, L - 8), 8), 8), :]
            first = jnp.where(r0 == 0, 0.0, above[7:8, :])
            last = jnp.where(r0 + CONV_ROWS == L, 0.0, below[0:1, :])
            for part in range(3):
                cols = slice(part * HY_WIDTH, (part + 1) * HY_WIDTH)
                u = u_ref[b, rows, cols]
                prev = jnp.where(t == 0, first[:, cols], pltpu.roll(u, 1, axis=0))
                nxt = jnp.where(t == CONV_ROWS - 1, last[:, cols], pltpu.roll(u, CONV_ROWS - 1, axis=0))
                y = (cb_ref[:, cols] + prev * cw_ref[0:1, cols] + u * cw_ref[1:2, cols]
                     + nxt * cw_ref[2:3, cols])
                if part < 2:
                    gate_ref[part, rows, :] = y
                else:
                    z_ref[rows, :] = y
            return carry

        lax.fori_loop(0, L // CONV_ROWS, short_conv, 0, unroll=(nb == 1))

        for o in range(2):
            for j in range(nb):
                zf_ref[j] = _dot(fwd_ref[...], z_ref[j * blk:(j + 1) * blk, :].astype(BF16))
            bias = hb_ref[o]

            def out_block(i, carry, o=o, bias=bias):
                def spec_rows(c, carry2):
                    r0 = pl.multiple_of(c * SPEC_ROWS, SPEC_ROWS)
                    re = pl.ds(r0, SPEC_ROWS)
                    im = pl.ds(blk + r0, SPEC_ROWS)
                    top = jnp.zeros((SPEC_ROWS, HY_WIDTH), F32)
                    bot = jnp.zeros((SPEC_ROWS, HY_WIDTH), F32)
                    for j in range(nb):
                        d = i - j + nb
                        gr, gi = g_ref[o, d, re, :], g_ref[o, d, im, :]
                        zr, zi = zf_ref[j, re, :], zf_ref[j, im, :]
                        top = top + (gr * zr - gi * zi)
                        bot = bot + (gr * zi + gi * zr)
                    y_ref[re, :] = top.astype(BF16)
                    y_ref[im, :] = bot.astype(BF16)
                    return carry2

                lax.fori_loop(0, blk // SPEC_ROWS, spec_rows, 0, unroll=(nb == 1))
                ny = jnp.zeros((1, HY_WIDTH), F32)
                for j in range(nb):
                    ny = ny + ny_ref[o, i - j + nb] * zf_ref[j, blk:blk + 1, :]
                head = y_ref[blk:blk + NY_ROWS, :].astype(F32)
                y_ref[blk:blk + NY_ROWS, :] = jnp.where(ny_row, ny, head).astype(BF16)
                conv = _dot(inv_ref[...], y_ref[...])
                rows = pl.ds(pl.multiple_of(i * blk, blk), blk)
                z_ref[rows, :] = gate_ref[o, rows, :] * (conv + bias * z_ref[rows, :])
                return carry

            lax.fori_loop(0, nb, out_block, 0, unroll=(nb == 1))
        o_ref[b] = z_ref[...].astype(o_ref.dtype)


def _hyena_mixer(l, u_hy, conv_w, conv_b, hy_bias, spec, nyq, per_step):
    B, L, _ = u_hy.shape
    nd = spec.shape[1]
    blk = _dft_block(L)
    n = 2 * blk
    fwd, inv, _, _ = _dft_consts(blk)
    return pl.pallas_call(
        _hyena_kernel,
        grid=(B // per_step,),
        in_specs=[
            pl.BlockSpec((per_step, L, 3 * HY_WIDTH), lambda b: (b, 0, 0)),
            _layer((3, 3 * HY_WIDTH), l),
            _layer((1, 3 * HY_WIDTH), l),
            _layer((2, 1, HY_WIDTH), l),
            _resident((2, nd, n, HY_WIDTH)),
            _resident((2, nd, 1, HY_WIDTH)),
            _resident((n, blk)),
            _resident((blk, n)),
        ],
        out_specs=pl.BlockSpec((per_step, L, HY_WIDTH), lambda b: (b, 0, 0)),
        out_shape=jax.ShapeDtypeStruct((B, L, HY_WIDTH), BF16),
        scratch_shapes=[
            pltpu.VMEM((L, HY_WIDTH), F32),
            pltpu.VMEM((2, L, HY_WIDTH), F32),
            pltpu.VMEM((L // blk, n, HY_WIDTH), F32),
            pltpu.VMEM((n, HY_WIDTH), BF16),
        ],
        compiler_params=_params("parallel"),
        name="hyena_mixer",
    )(u_hy, conv_w, conv_b, hy_bias, spec, nyq,
      jnp.asarray(fwd, F32).astype(BF16), jnp.asarray(inv, F32).astype(BF16))


HEAD_PAIR = 2 * NA_HEAD_DIM
SOFTMAX_ROWS = 64
PV_CHUNK = 256


def _paired_heads(q, keys, values, bias, s_ref, p_ref):
    M = q.shape[0]
    n_slots = s_ref.shape[0]
    lane = lax.broadcasted_iota(jnp.int32, (M, HEAD_PAIR), 1)

    def own_lanes(ln, e):
        return (ln < NA_HEAD_DIM) if e == 0 else (ln >= NA_HEAD_DIM)

    def scores(h):
        cols = slice((h // 2) * HEAD_PAIR, (h // 2 + 1) * HEAD_PAIR)
        qh = jnp.where(own_lanes(lane, h % 2), q[:, cols], jnp.zeros((M, HEAD_PAIR), BF16))
        off, m = 0, None
        for i, k in enumerate(keys):
            for c0 in range(0, k.shape[0], PV_CHUNK):
                s = _dot_nt(qh, k[c0:c0 + PV_CHUNK, cols])
                if bias is not None and i == 0:
                    s = s + bias(h, slice(c0, c0 + PV_CHUNK))
                s_ref[h % n_slots, :, off + c0:off + c0 + PV_CHUNK] = s
                mi = jnp.max(s, axis=-1, keepdims=True)
                m = mi if m is None else jnp.maximum(m, mi)
            off += k.shape[0]
        return m

    def attend(h, m):
        cols = slice((h // 2) * HEAD_PAIR, (h // 2 + 1) * HEAD_PAIR)
        acc, off = None, 0
        for v in values:
            vl = lax.broadcasted_iota(jnp.int32, (v.shape[0], HEAD_PAIR), 1)
            vh = jnp.where(own_lanes(vl, h % 2), v[:, cols], jnp.ones((v.shape[0], HEAD_PAIR), BF16))
            for c0 in range(0, v.shape[0], PV_CHUNK):
                kc = slice(off + c0, off + c0 + PV_CHUNK)
                for r0 in range(0, M, SOFTMAX_ROWS):
                    rows = slice(r0, r0 + SOFTMAX_ROWS)
                    p_ref[h % 2, rows, kc] = jnp.exp(s_ref[h % n_slots, rows, kc] - m[rows]).astype(BF16)
                part = _dot(p_ref[h % 2, :, kc], vh[c0:c0 + PV_CHUNK])
                acc = part if acc is None else acc + part
            off += v.shape[0]
        return acc / pltpu.roll(acc, NA_HEAD_DIM, axis=1)

    outs, halves = [], []
    ahead = n_slots - 1
    maxima = {h: scores(h) for h in range(ahead)}
    for h in range(NA_HEADS):
        if h + ahead < NA_HEADS:
            maxima[h + ahead] = scores(h + ahead)
        halves.append(attend(h, maxima.pop(h)))
        if h % 2 == 1:
            outs.append(jnp.where(lane < NA_HEAD_DIM, halves[0], halves[1]))
            halves = []
    return jnp.concatenate(outs, axis=-1)


def _ctx_attn_kernel(q_ref, k_ref, v_ref, o_ref, s_ref, p_ref):
    nseq, L = q_ref.shape[0], q_ref.shape[1]
    even = lax.broadcasted_iota(jnp.int32, (L, HEAD_PAIR), 1) < NA_HEAD_DIM
    zero = jnp.zeros((L, HEAD_PAIR), BF16)
    units = [(b, p) for b in range(nseq) for p in range(NA_HEADS // 2)]

    def scores(n):
        b, p = units[n]
        cols = slice(p * HEAD_PAIR, (p + 1) * HEAD_PAIR)
        qp = q_ref[b, :, cols]
        q2 = jnp.concatenate([jnp.where(even, qp, zero), jnp.where(even, zero, qp)], axis=0)
        s_ref[n % 2] = _dot_nt(q2, k_ref[b, 0, :, cols].astype(BF16))

    def attend(n):
        b, p = units[n]
        cols = slice(p * HEAD_PAIR, (p + 1) * HEAD_PAIR)
        for r0 in range(0, 2 * L, SOFTMAX_ROWS):
            rows = slice(r0, r0 + SOFTMAX_ROWS)
            s = s_ref[n % 2, rows, :]
            e = jnp.exp(s - jnp.max(s, axis=-1, keepdims=True))
            p_ref[n % 2, rows, :] = (e * (1.0 / jnp.sum(e, axis=-1, keepdims=True))).astype(BF16)
        acc = _dot(p_ref[n % 2], v_ref[b, 0, :, cols].astype(BF16))
        o_ref[b, :, cols] = jnp.where(even, acc[:L], acc[L:]).astype(o_ref.dtype)

    scores(0)
    for n in range(len(units)):
        if n + 1 < len(units):
            scores(n + 1)
        attend(n)


def _context_attention(q, k_cache, v_cache, layer, per_step):
    B, L, _ = q.shape
    spec = pl.BlockSpec((per_step, L, NA_WIDTH), lambda b: (b, 0, 0))
    kv = pl.BlockSpec((per_step, 1, L, NA_WIDTH), lambda b: (b, layer, 0, 0))
    return pl.pallas_call(
        _ctx_attn_kernel,
        grid=(B // per_step,),
        in_specs=[spec, kv, kv],
        out_specs=spec,
        out_shape=jax.ShapeDtypeStruct((B, L, NA_WIDTH), BF16),
        scratch_shapes=[pltpu.VMEM((2, 2 * L, L), F32), pltpu.VMEM((2, 2 * L, L), BF16)],
        compiler_params=_params("parallel"),
        name="context_attention",
    )(q, k_cache, v_cache)


N_DR = 2 * NA_KH - 1
N_DC = 2 * NA_KW - 1
NA_RB = 4
NA_WIN = NA_RB + NA_KH
WIN_KEYS = NA_WIN * GRID_W
NA_SCORE_SLOTS = 2
NA_BLOCKS_PER_STEP = 2


def _bias_kernel(rb_ref, o_ref):
    h = pl.program_id(0)
    qc = lax.broadcasted_iota(jnp.int32, (GRID_W, GRID_W), 0)
    kc = lax.broadcasted_iota(jnp.int32, (GRID_W, GRID_W), 1)
    dc = jnp.clip(kc - qc, -(NA_KW - 1), NA_KW - 1) + (NA_KW - 1)
    col_start = jnp.clip(qc - NA_KW // 2, 0, GRID_W - NA_KW)
    visible = (kc >= col_start) & (kc < col_start + NA_KW)
    blocks = []
    for dr in range(N_DR):
        acc = jnp.zeros((GRID_W, GRID_W), F32)
        for d in range(N_DC):
            acc = jnp.where(dc == d, rb_ref[h * N_DR + dr, d], acc)
        blocks.append(jnp.where(visible, acc, MASK_VALUE))
    hidden = jnp.full((GRID_W, GRID_W), MASK_VALUE, F32)
    for variant, (first, base) in enumerate(((0, NA_KH - 1), (None, NA_KH // 2 - 1), (NA_RB, -1))):
        for a in range(NA_RB):
            lo = a if first is None else first
            row = [blocks[i - a + base] if lo <= i < lo + NA_KH else hidden for i in range(NA_WIN)]
            o_ref[variant, 0, a * GRID_W:(a + 1) * GRID_W, :] = jnp.concatenate(row, axis=-1)


def _bias_table(rel_bias_l):
    return pl.pallas_call(
        _bias_kernel,
        grid=(NA_HEADS,),
        in_specs=[pl.BlockSpec(memory_space=pltpu.SMEM)],
        out_specs=pl.BlockSpec((3, 1, NA_RB * GRID_W, WIN_KEYS), lambda h: (0, h, 0, 0)),
        out_shape=jax.ShapeDtypeStruct((3, NA_HEADS, NA_RB * GRID_W, WIN_KEYS), F32),
        compiler_params=_params("parallel"),
        name="na_bias_table",
    )(rel_bias_l.reshape(NA_HEADS * N_DR, N_DC))


def _na_kernel(q_ref, k_ref, v_ref, kc_ref, vc_ref, *rest):
    tab_refs, (o_ref, s_ref, p_ref) = rest[:-3], rest[-3:]
    rows = k_ref.shape[1] // GRID_W
    kc = kc_ref[0, 0].astype(BF16)
    vc = vc_ref[0, 0].astype(BF16)
    for sub, tab_ref in enumerate(tab_refs):
        rb = pl.program_id(1) * len(tab_refs) + sub
        first_row = jnp.clip(rb * NA_RB - NA_KH // 2, 0, rows - NA_WIN)
        start = pl.multiple_of(first_row * GRID_W, NA_RB * GRID_W)
        keys = [k_ref[0, pl.ds(start, WIN_KEYS), :], kc]
        values = [v_ref[0, pl.ds(start, WIN_KEYS), :], vc]
        q_rows = slice(sub * NA_RB * GRID_W, (sub + 1) * NA_RB * GRID_W)
        out = _paired_heads(q_ref[0, q_rows, :], keys, values,
                            lambda h, cols, tab_ref=tab_ref: tab_ref[0, h, :, cols], s_ref, p_ref)
        o_ref[0, q_rows, :] = out.astype(o_ref.dtype)


def _neighbourhood_attention(q, k, v, cache_k, cache_v, layer, table):
    B, L, _ = q.shape
    rows = L // GRID_W
    assert rows % (NA_RB * NA_BLOCKS_PER_STEP) == 0 and rows >= NA_WIN + NA_RB
    nrb = rows // NA_RB
    Lc = cache_k.shape[2]
    seq = pl.BlockSpec((1, L, NA_WIDTH), lambda b, r: (b, 0, 0))
    blk = pl.BlockSpec((1, NA_BLOCKS_PER_STEP * NA_RB * GRID_W, NA_WIDTH), lambda b, r: (b, r, 0))
    ctx = pl.BlockSpec((1, 1, Lc, NA_WIDTH), lambda b, r: (b, layer, 0, 0))
    placement = lambda rb: jnp.where(rb == 0, 0, jnp.where(rb == nrb - 1, 2, 1))
    tabs = [pl.BlockSpec((1, NA_HEADS, NA_RB * GRID_W, WIN_KEYS),
                         lambda b, r, sub=sub: (placement(r * NA_BLOCKS_PER_STEP + sub), 0, 0, 0))
            for sub in range(NA_BLOCKS_PER_STEP)]
    return pl.pallas_call(
        _na_kernel,
        grid=(B, nrb // NA_BLOCKS_PER_STEP),
        in_specs=[blk, seq, seq, ctx, ctx] + tabs,
        out_specs=blk,
        out_shape=jax.ShapeDtypeStruct((B, L, NA_WIDTH), BF16),
        scratch_shapes=[pltpu.VMEM((NA_SCORE_SLOTS, NA_RB * GRID_W, WIN_KEYS + Lc), F32),
                        pltpu.VMEM((2, NA_RB * GRID_W, WIN_KEYS + Lc), BF16)],
        compiler_params=_params("parallel", "arbitrary"),
        name="neighbourhood_attention",
    )(q, k, v, cache_k, cache_v, *([table] * NA_BLOCKS_PER_STEP))


FF_CHUNK = 1024


def _out_mlp_kernel(x_ref, yp_ref, yh_ref, ya_ref, mod_ref, g_ref, wo_ref, wu_ref, wd_ref, o_ref):
    y = (_dot(yp_ref[0], wo_ref[0:POOL_WIDTH, :])
         + _dot(yh_ref[0], wo_ref[POOL_WIDTH:POOL_WIDTH + HY_WIDTH, :])
         + _dot(ya_ref[0], wo_ref[POOL_WIDTH + HY_WIDTH:D_MODEL, :]))
    x = x_ref[0] + mod_ref[0, 2:3, :] * y
    h = _modulated_norm(x, g_ref[...], mod_ref[0, 3:4, :], mod_ref[0, 4:5, :]).astype(BF16)
    acc = jnp.zeros(x.shape, F32)
    for c in range(D_FF // FF_CHUNK):
        cols = slice(c * FF_CHUNK, (c + 1) * FF_CHUNK)
        a = jnp.maximum(_dot(h, wu_ref[:, cols]), 0.0)
        acc = acc + _dot((a * a).astype(BF16), wd_ref[cols, :])
    o_ref[0] = x + mod_ref[0, 5:6, :] * acc


def _out_mlp(l, x, y_pool, y_hy, y_na, mod, mod_row0, g2, w_out, w_up, w_down):
    B, L, _ = x.shape
    tm = TOKEN_TILE
    assert L % tm == 0
    tok = lambda w: pl.BlockSpec((1, tm, w), lambda b, i: (b, i, 0))
    return pl.pallas_call(
        _out_mlp_kernel,
        grid=(B, L // tm),
        in_specs=[
            tok(D_MODEL), tok(POOL_WIDTH), tok(HY_WIDTH), tok(NA_WIDTH),
            _mod_spec(l, mod_row0),
            _layer((1, D_MODEL), l),
            _layer((D_MODEL, D_MODEL), l),
            _layer((D_MODEL, D_FF), l),
            _layer((D_FF, D_MODEL), l),
        ],
        out_specs=tok(D_MODEL),
        out_shape=jax.ShapeDtypeStruct((B, L, D_MODEL), F32),
        compiler_params=_params("parallel", "parallel"),
        name="out_mlp",
    )(x, y_pool, y_hy, y_na, mod, g2, w_out, w_up, w_down)


def _block_diag(blocks):
    *lead, g, c, d = blocks.shape
    eye = jnp.eye(g, dtype=blocks.dtype)
    out = blocks[..., :, :, None, :] * eye[:, None, :, None]
    return out.reshape(*lead, g * c, g * d)


def kernel(x_prompt, x_sample, cache_k, cache_v, c, c_ctx, norm1_g, norm2_g, w_mod, b_mod, w_in,
           pool_w, pool_scale, hy_conv_w, hy_conv_b, hy_f1_w, hy_f1_b, hy_f1_freq, hy_f2_w, hy_f2_b,
           hy_f2_freq, hy_f3_w, hy_bias, q_norm_g, k_norm_g, rel_bias, w_out, w_up, w_down):
    n_prompt, seq, _ = x_prompt.shape
    n_dec, dec_seq, _ = x_sample.shape

    conds = jnp.zeros((MOD_ROWS, D_MODEL), F32).at[0].set(c_ctx).at[1:1 + n_dec].set(c)
    mod = _adaln(conds, w_mod, b_mod).reshape(DEPTH, MOD_ROWS, N_MOD, D_MODEL)

    cache_k = cache_k.reshape(n_dec, DEPTH, -1, NA_WIDTH)
    cache_v = cache_v.reshape(n_dec, DEPTH, -1, NA_WIDTH)

    g1 = norm1_g.reshape(DEPTH, 1, D_MODEL)
    g2 = norm2_g.reshape(DEPTH, 1, D_MODEL)
    w_in_b, w_out_b, w_up_b, w_down_b = (w.astype(BF16) for w in (w_in, w_out, w_up, w_down))
    pool_bd = _block_diag(pool_w).astype(BF16)
    pool_s = pool_scale.reshape(DEPTH, 1, POOL_WIDTH)
    qg = jnp.tile(q_norm_g, (1, NA_HEADS)).reshape(DEPTH, 1, NA_WIDTH)
    kg = jnp.tile(k_norm_g, (1, NA_HEADS)).reshape(DEPTH, 1, NA_WIDTH)
    col = lambda a: a.reshape(DEPTH, HY_FFN, 1)
    f1_w = jnp.zeros((DEPTH, HY_FFN, HY_EMB_PAD), F32).at[:, :, :HY_EMB].set(hy_f1_w.transpose(0, 2, 1))
    filt_args = (f1_w, col(hy_f1_b), col(hy_f1_freq), hy_f2_w.transpose(0, 2, 1), col(hy_f2_b),
                 col(hy_f2_freq), hy_f3_w)
    conv_b = hy_conv_b.reshape(DEPTH, 1, 3 * HY_WIDTH)
    hy_b = hy_bias.reshape(DEPTH, 2, 1, HY_WIDTH)

    xp = x_prompt.reshape(1, n_prompt * seq, D_MODEL)
    xs = x_sample
    new_k = new_v = None
    per_seq = lambda a: a.reshape(n_prompt, seq, a.shape[-1])
    flat = lambda a: a.reshape(1, n_prompt * seq, a.shape[-1])
    for l in range(DEPTH):
        u_pool, u_hy, q, new_k, new_v = _in_proj(l, xp, mod, 0, g1, w_in_b, qg, kg,
                                                 cache=(seq, new_k, new_v))
        spec, nyq = _hyena_spectra(*_hyena_filters(l, seq, *filt_args))
        y_pool = _pool_mixer(l, per_seq(u_pool), pool_bd, pool_s, CTX_PER_STEP)
        y_hy = _hyena_mixer(l, per_seq(u_hy), hy_conv_w, conv_b, hy_b, spec, nyq, CTX_PER_STEP)
        y_na = _context_attention(per_seq(q), new_k, new_v, l, CTX_PER_STEP)
        xp = _out_mlp(l, xp, flat(y_pool), flat(y_hy), flat(y_na), mod, 0, g2, w_out_b, w_up_b, w_down_b)

        u_pool, u_hy, q, k, v = _in_proj(l, xs, mod, 1, g1, w_in_b, qg, kg)
        spec, nyq = _hyena_spectra(*_hyena_filters(l, dec_seq, *filt_args))
        y_pool = _pool_mixer(l, u_pool, pool_bd, pool_s, 1)
        y_hy = _hyena_mixer(l, u_hy, hy_conv_w, conv_b, hy_b, spec, nyq, 1)
        y_na = _neighbourhood_attention(q, k, v, cache_k, cache_v, l, _bias_table(rel_bias[l]))
        xs = _out_mlp(l, xs, y_pool, y_hy, y_na, mod, 1, g2, w_out_b, w_up_b, w_down_b)

    shape_kv = (n_prompt, DEPTH, seq, NA_HEADS, NA_HEAD_DIM)
    return (xp.reshape(n_prompt, seq, D_MODEL), xs, new_k.reshape(shape_kv), new_v.reshape(shape_kv))
```

```python
import functools
import math

import numpy as np
import jax
import jax.numpy as jnp
from jax import lax
from jax.experimental import pallas as pl
from jax.experimental.pallas import tpu as pltpu

F32 = jnp.float32
BF16 = jnp.bfloat16

D_MODEL = 1024
DEPTH = 2
GRID_W = 64
POOL_WIDTH = 256
POOL_GROUPS = 4
POOL_GROUP_DIM = 64
HY_WIDTH = 256
HY_BANDS = 16
HY_EMB = 1 + 2 * HY_BANDS
HY_EMB_PAD = 128
HY_FFN = 64
HY_TARGET = 1e-2
HY_FAST = 0.3
HY_SLOW = 1.5
NA_HEAD_DIM = 64
NA_WIDTH = 512
NA_HEADS = 8
NA_KH = 8
NA_KW = 16
N_MOD = 6
D_FF = 4096
NORM_EPS = 1e-6
ATT_SCALE = NA_HEAD_DIM ** -0.5

C_POOL = 0
C_HY = POOL_WIDTH
C_Q = C_HY + 3 * HY_WIDTH
C_K = C_Q + NA_WIDTH
C_V = C_K + NA_WIDTH
IN_WIDTH = C_V + NA_WIDTH

MAX_DFT_BLOCK = 512
MASK_VALUE = -1e30
VMEM_LIMIT = 56 * 1024 * 1024
TOKEN_TILE = 512
IN_TOKEN_TILE = 1024
MOD_ROWS = 8
CTX_PER_STEP = 4


def _params(*sem):
    return pltpu.CompilerParams(dimension_semantics=sem, vmem_limit_bytes=VMEM_LIMIT)


def _resident(shape):
    return pl.BlockSpec(shape, lambda *_: (0,) * len(shape), pipeline_mode=pl.Buffered(1))


def _layer(shape, l):
    return pl.BlockSpec((None,) + tuple(shape), lambda *_: (l,) + (0,) * len(shape),
                        pipeline_mode=pl.Buffered(1))


def _mod_spec(l, row0):
    return pl.BlockSpec((None, 1, N_MOD, D_MODEL), lambda b, i: (l, row0 + b, 0, 0))


def _dot(a, b):
    return jnp.dot(a, b, preferred_element_type=F32)


def _dot_nt(a, b):
    return lax.dot_general(a, b, (((1,), (1,)), ((), ())), preferred_element_type=F32)


def _dot_f32(a, b):
    return jnp.dot(a, b, preferred_element_type=F32, precision=lax.Precision.HIGHEST)


def _adaln_kernel(c_ref, w_ref, b_ref, o_ref):
    c = c_ref[...]
    s = c * (1.0 / (1.0 + jnp.exp(-c)))
    o_ref[0] = _dot(s.astype(BF16), w_ref[0].astype(BF16)) + b_ref[0]


def _adaln(conds, w_mod, b_mod):
    tn = 1536
    n = N_MOD * D_MODEL
    return pl.pallas_call(
        _adaln_kernel,
        grid=(DEPTH, n // tn),
        in_specs=[
            pl.BlockSpec((MOD_ROWS, D_MODEL), lambda l, j: (0, 0)),
            pl.BlockSpec((1, D_MODEL, tn), lambda l, j: (l, 0, j)),
            pl.BlockSpec((1, 1, tn), lambda l, j: (l, 0, j)),
        ],
        out_specs=pl.BlockSpec((1, MOD_ROWS, tn), lambda l, j: (l, 0, j)),
        out_shape=jax.ShapeDtypeStruct((DEPTH, MOD_ROWS, n), F32),
        compiler_params=_params("parallel", "parallel"),
        name="adaln",
    )(conds, w_mod, b_mod.reshape(DEPTH, 1, n))


def _modulated_norm(x, g, shift, scale):
    ms = jnp.mean(x * x, axis=-1, keepdims=True)
    return (x * lax.rsqrt(ms + NORM_EPS) * g) * (1.0 + scale) + shift


def _inproj_kernel(x_ref, mod_ref, g_ref, w_ref, qg_ref, kg_ref, *rest):
    up_ref, uh_ref, q_ref, k_ref, v_ref = rest[-5:]
    h = _modulated_norm(x_ref[0], g_ref[...], mod_ref[0, 0:1, :], mod_ref[0, 1:2, :]).astype(BF16)
    up_ref[0] = _dot(h, w_ref[:, C_POOL:C_HY])
    uh_ref[0] = _dot(h, w_ref[:, C_HY:C_Q])

    even = lax.broadcasted_iota(jnp.int32, (x_ref.shape[1], HEAD_PAIR), 1) < NA_HEAD_DIM

    def head_norm(u, g):
        outs = []
        for p in range(NA_HEADS // 2):
            cols = slice(p * HEAD_PAIR, (p + 1) * HEAD_PAIR)
            up = u[:, cols]
            sq = up * up
            ms_even = jnp.sum(jnp.where(even, sq, 0.0), axis=-1, keepdims=True) * (1.0 / NA_HEAD_DIM)
            ms_odd = jnp.sum(jnp.where(even, 0.0, sq), axis=-1, keepdims=True) * (1.0 / NA_HEAD_DIM)
            r = jnp.where(even, lax.rsqrt(ms_even + NORM_EPS), lax.rsqrt(ms_odd + NORM_EPS))
            outs.append(up * r * g[:, cols])
        return jnp.concatenate(outs, axis=-1)

    q = head_norm(_dot(h, w_ref[:, C_Q:C_K]), qg_ref[...])
    q_ref[0] = (q * ATT_SCALE).astype(q_ref.dtype)
    k = head_norm(_dot(h, w_ref[:, C_K:C_V]), kg_ref[...])
    v = _dot(h, w_ref[:, C_V:IN_WIDTH])
    k_ref[...] = k.astype(k_ref.dtype).reshape(k_ref.shape)
    v_ref[...] = v.astype(v_ref.dtype).reshape(v_ref.shape)


def _in_proj(l, x, mod, mod_row0, g1, w_in, qg, kg, cache=None):
    B, L, _ = x.shape
    tm = IN_TOKEN_TILE
    assert L % tm == 0
    tok = lambda w: pl.BlockSpec((1, tm, w), lambda b, i: (b, i, 0))
    in_specs = [
        tok(D_MODEL),
        _mod_spec(l, mod_row0),
        _layer((1, D_MODEL), l),
        _layer((D_MODEL, IN_WIDTH), l),
        _layer((1, NA_WIDTH), l),
        _layer((1, NA_WIDTH), l),
    ]
    args = [x, mod, g1, w_in, qg, kg]
    aliases = {}
    if cache is None:
        kv_spec = tok(NA_WIDTH)
        kv_shape = jax.ShapeDtypeStruct((B, L, NA_WIDTH), BF16)
    else:
        seq, k_cache, v_cache = cache
        assert B == 1 and tm % seq == 0
        per_tile = tm // seq
        kv_spec = pl.BlockSpec((per_tile, 1, seq, NA_WIDTH), lambda b, i: (i, l, 0, 0))
        kv_shape = jax.ShapeDtypeStruct((L // seq, DEPTH, seq, NA_WIDTH), F32)
        if k_cache is not None:
            in_specs += [pl.BlockSpec(memory_space=pl.ANY)] * 2
            aliases = {len(args): 3, len(args) + 1: 4}
            args += [k_cache, v_cache]
    return pl.pallas_call(
        _inproj_kernel,
        grid=(B, L // tm),
        in_specs=in_specs,
        out_specs=[tok(POOL_WIDTH), tok(3 * HY_WIDTH), tok(NA_WIDTH), kv_spec, kv_spec],
        out_shape=[
            jax.ShapeDtypeStruct((B, L, POOL_WIDTH), F32),
            jax.ShapeDtypeStruct((B, L, 3 * HY_WIDTH), F32),
            jax.ShapeDtypeStruct((B, L, NA_WIDTH), BF16),
            kv_shape,
            kv_shape,
        ],
        input_output_aliases=aliases,
        compiler_params=_params("parallel", "parallel"),
        name="in_proj",
    )(*args)


POOL_PAD = 8


def _pool_kernel(u_ref, w_ref, s_ref, o_ref, pad_ref, inv_ref):
    nseq, L = u_ref.shape[0], u_ref.shape[1]
    n = L + 2 * POOL_PAD

    def sh(v, s):
        return pltpu.roll(v, s % n, axis=0)

    lane_p = lax.broadcasted_iota(jnp.int32, (n, POOL_WIDTH), 1)

    @pl.when(pl.program_id(0) == 0)
    def _():
        t = lax.broadcasted_iota(jnp.int32, (L, POOL_WIDTH), 0)
        lane = lax.broadcasted_iota(jnp.int32, (L, POOL_WIDTH), 1)
        half = jnp.where(lane < 64, 1, jnp.where(lane < 128, 2, jnp.where(lane < 192, 4, 8)))
        lo = jnp.maximum(t - half, 0)
        hi = jnp.minimum(t + half - 1, L - 1)
        inv_ref[...] = 1.0 / (hi - lo + 1).astype(F32)

    inv_cnt = inv_ref[...]
    for b in range(nseq):
        u = u_ref[b]
        pad_ref[b, 0:POOL_PAD, :] = jnp.zeros((POOL_PAD, POOL_WIDTH), F32)
        pad_ref[b, POOL_PAD + L:n, :] = jnp.zeros((POOL_PAD, POOL_WIDTH), F32)
        pad_ref[b, POOL_PAD:POOL_PAD + L, :] = u
        a = pad_ref[b]
        p2 = a + sh(a, 1)
        p4 = sh(p2, 1) + sh(p2, -1)
        p8 = sh(p4, 2) + sh(p4, -2)
        p16 = sh(p8, 4) + sh(p8, -4)
        win = jnp.where(lane_p < 64, p2, jnp.where(lane_p < 128, p4, jnp.where(lane_p < 192, p8, p16)))
        pooled = win[POOL_PAD:POOL_PAD + L, :] * inv_cnt - u
        o_ref[b] = (_dot(pooled.astype(BF16), w_ref[...]) * s_ref[...]).astype(o_ref.dtype)


def _pool_mixer(l, u_pool, w_bd, scale, per_step):
    B, L, _ = u_pool.shape
    blk = pl.BlockSpec((per_step, L, POOL_WIDTH), lambda b: (b, 0, 0))
    return pl.pallas_call(
        _pool_kernel,
        grid=(B // per_step,),
        in_specs=[blk, _layer((POOL_WIDTH, POOL_WIDTH), l), _layer((1, POOL_WIDTH), l)],
        out_specs=blk,
        out_shape=jax.ShapeDtypeStruct((B, L, POOL_WIDTH), BF16),
        scratch_shapes=[pltpu.VMEM((per_step, L + 2 * POOL_PAD, POOL_WIDTH), F32),
                        pltpu.VMEM((L, POOL_WIDTH), F32)],
        compiler_params=_params("arbitrary"),
        name="pool_mixer",
    )(u_pool, w_bd, scale)


@functools.lru_cache(maxsize=None)
def _hyena_position_consts(L):
    p = np.arange(L, dtype=np.float64)
    t = p / (L - 1)
    w = 2.0 * math.pi * p / L
    f = np.linspace(1e-4, HY_BANDS - 1, HY_BANDS)
    z = np.zeros((HY_EMB_PAD, L), np.float64)
    z[0] = t
    z[1:1 + HY_BANDS] = np.cos(f[:, None] * w[None, :])
    z[1 + HY_BANDS:HY_EMB] = -np.sin(f[:, None] * w[None, :])
    deltas = np.abs(np.linspace(math.log(HY_TARGET) / HY_SLOW, math.log(HY_TARGET) / HY_FAST, HY_WIDTH))
    decay = np.exp(-t[:, None] * deltas[None, :])
    return z.astype(np.float32), decay.astype(np.float32)


FILTER_BLOCK = 512


def _filter_kernel(z_ref, dec_ref, w1_ref, b1_ref, f1_ref, w2_ref, b2_ref, f2_ref, w3_ref,
                   kf_ref, kb_ref, sum_ref):
    j = pl.program_id(0)
    h = jnp.sin(f1_ref[...] * (_dot_f32(w1_ref[...], z_ref[...]) + b1_ref[...]))
    h = jnp.sin(f2_ref[...] * (_dot_f32(w2_ref[...], h) + b2_ref[...]))
    k = lax.dot_general(h, w3_ref[...], (((0,), (0,)), ((), ())), preferred_element_type=F32,
                        precision=lax.Precision.HIGHEST)
    for dirn, out_ref in enumerate((kf_ref, kb_ref)):
        for o in range(2):
            c0 = (dirn * 2 + o) * HY_WIDTH
            ko = k[:, c0:c0 + HY_WIDTH] * dec_ref[...]
            part = jnp.sum(jnp.abs(ko), axis=0, keepdims=True)

            @pl.when(j == 0)
            def _():
                sum_ref[dirn, o] = part

            @pl.when(j > 0)
            def _():
                sum_ref[dirn, o] = sum_ref[dirn, o] + part

            out_ref[o] = ko


def _hyena_filters(l, L, w1, b1, f1, w2, b2, f2, w3):
    z, decay = _hyena_position_consts(L)
    fb = min(L, FILTER_BLOCK)
    filt = pl.BlockSpec((2, fb, HY_WIDTH), lambda j: (0, j, 0))
    return pl.pallas_call(
        _filter_kernel,
        grid=(L // fb,),
        in_specs=[
            pl.BlockSpec((HY_EMB_PAD, fb), lambda j: (0, j)),
            pl.BlockSpec((fb, HY_WIDTH), lambda j: (j, 0)),
            _layer((HY_FFN, HY_EMB_PAD), l), _layer((HY_FFN, 1), l), _layer((HY_FFN, 1), l),
            _layer((HY_FFN, HY_FFN), l), _layer((HY_FFN, 1), l), _layer((HY_FFN, 1), l),
            _layer((HY_FFN, 4 * HY_WIDTH), l),
        ],
        out_specs=[filt, filt, pl.BlockSpec((2, 2, 1, HY_WIDTH), lambda j: (0, 0, 0, 0))],
        out_shape=[
            jax.ShapeDtypeStruct((2, L, HY_WIDTH), F32),
            jax.ShapeDtypeStruct((2, L, HY_WIDTH), F32),
            jax.ShapeDtypeStruct((2, 2, 1, HY_WIDTH), F32),
        ],
        compiler_params=_params("arbitrary"),
        name="hyena_filters",
    )(jnp.asarray(z), jnp.asarray(decay), w1, b1, f1, w2, b2, f2, w3)


def _dft_block(L):
    return min(L, MAX_DFT_BLOCK)


@functools.lru_cache(maxsize=None)
def _dft_consts(blk):
    n = np.arange(blk, dtype=np.float64)
    j = np.arange(blk, dtype=np.float64)
    ang = 2.0 * math.pi * np.outer(j, n) / (2 * blk)
    fwd = np.concatenate([np.cos(ang), -np.sin(ang)], axis=0)
    fwd[blk] = np.cos(math.pi * n)
    wgt = np.where(j == 0, 1.0, 2.0)[None, :]
    inv = np.concatenate([wgt * np.cos(ang.T), -2.0 * np.sin(ang.T)], axis=1) / (2 * blk)
    inv[:, blk] = np.cos(math.pi * n) / (2 * blk)
    sign = np.tile(np.where(np.arange(blk) % 2 == 0, 1.0, -1.0), 2)[:, None]
    col0 = fwd[:, 0:1].copy()
    return fwd, inv, sign.astype(np.float32), col0.astype(np.float32)


def _split2(a64):
    hi = _np_bf16(a64)
    return hi, _np_bf16(a64 - hi.astype(np.float64))


def _np_bf16(a64):
    return a64.astype(np.float32).astype(BF16)


def _spectra_kernel(kb_ref, kf_ref, sum_ref, fh_ref, fl_ref, sign_ref, signconj_ref, col0_ref,
                    g_ref, ny_ref, prev_ref, next0_ref):
    s = pl.program_id(1)
    per = g_ref.shape[1]
    n, blk = fh_ref.shape
    backward = s < pl.num_programs(1) // 2

    @pl.when(s == 0)
    def _():
        prev_ref[...] = jnp.zeros(prev_ref.shape, F32)
        next0_ref[...] = jnp.zeros(next0_ref.shape, F32)

    inv_norm = 1.0 / (sum_ref[0, 0] + 1e-6)
    row = lax.broadcasted_iota(jnp.int32, (n, HY_WIDTH), 0)

    def dft(a):
        a_hi = a.astype(BF16)
        a_lo = (a - a_hi.astype(F32)).astype(BF16)
        return _dot(fh_ref[...], a_hi) + (_dot(fh_ref[...], a_lo) + _dot(fl_ref[...], a_hi))

    def emit(t, fa, fa_tail):
        g = fa + sign_ref[...] * prev_ref[...]
        ny_ref[0, t] = g[blk:blk + 1, :]
        g_ref[0, t] = jnp.where(row == blk, 0.0, g)
        prev_ref[...] = fa_tail

    @pl.when(backward)
    def _():
        for t in range(per):
            q = per - 1 - t
            b = kb_ref[0, q * blk:(q + 1) * blk, :] * inv_norm
            tail = signconj_ref[...] * (dft(b) - col0_ref[...] * b[0:1, :])
            emit(t, col0_ref[...] * next0_ref[0:1, :] + tail, tail)
            next0_ref[0:1, :] = b[0:1, :]

    @pl.when(jnp.logical_not(backward))
    def _():
        for t in range(per):
            a = kf_ref[0, t * blk:(t + 1) * blk, :] * inv_norm
            fa = dft(a)
            emit(t, fa, fa - col0_ref[...] * a[0:1, :])


SPECTRA_PER_STEP = 4


def _hyena_spectra(kf, kb, sums):
    L = kf.shape[1]
    blk = _dft_block(L)
    n = 2 * blk
    nb = L // blk
    per = min(nb, SPECTRA_PER_STEP)
    half = nb // per
    fwd, _, sign, col0 = _dft_consts(blk)
    fh, fl = _split2(fwd)
    conj = np.where(np.arange(n) > blk, -1.0, 1.0).astype(np.float32)[:, None]
    return pl.pallas_call(
        _spectra_kernel,
        grid=(2, 2 * half),
        in_specs=[
            pl.BlockSpec((1, per * blk, HY_WIDTH), lambda o, s: (o, jnp.maximum(half - 1 - s, 0), 0)),
            pl.BlockSpec((1, per * blk, HY_WIDTH), lambda o, s: (o, jnp.maximum(s - half, 0), 0)),
            pl.BlockSpec((1, 1, 1, HY_WIDTH), lambda o, s: (jnp.where(s < half, 1, 0), o, 0, 0)),
            _resident((n, blk)), _resident((n, blk)),
            _resident((n, 1)), _resident((n, 1)), _resident((n, 1)),
        ],
        out_specs=[
            pl.BlockSpec((1, per, n, HY_WIDTH), lambda o, s: (o, s, 0, 0)),
            pl.BlockSpec((1, per, 1, HY_WIDTH), lambda o, s: (o, s, 0, 0)),
        ],
        out_shape=[
            jax.ShapeDtypeStruct((2, 2 * nb, n, HY_WIDTH), F32),
            jax.ShapeDtypeStruct((2, 2 * nb, 1, HY_WIDTH), F32),
        ],
        scratch_shapes=[pltpu.VMEM((n, HY_WIDTH), F32), pltpu.VMEM((8, HY_WIDTH), F32)],
        compiler_params=_params("parallel", "arbitrary"),
        name="hyena_spectra",
    )(kb, kf, sums, fh, fl, jnp.asarray(sign), jnp.asarray(sign * conj), jnp.asarray(col0))


SPEC_ROWS = 32
NY_ROWS = 16
CONV_ROWS = 64


def _hyena_kernel(u_ref, cw_ref, cb_ref, hb_ref, g_ref, ny_ref, fwd_ref, inv_ref, o_ref,
                  z_ref, gate_ref, zf_ref, y_ref):
    nseq, L = u_ref.shape[0], u_ref.shape[1]
    blk = fwd_ref.shape[1]
    nb = L // blk
    t = lax.broadcasted_iota(jnp.int32, (CONV_ROWS, HY_WIDTH), 0)
    ny_row = lax.broadcasted_iota(jnp.int32, (NY_ROWS, HY_WIDTH), 0) == 0

    for b in range(nseq):
        def short_conv(c, carry, b=b):
            r0 = pl.multiple_of(c * CONV_ROWS, CONV_ROWS)
            rows = pl.ds(r0, CONV_ROWS)
            above = u_ref[b, pl.ds(pl.multiple_of(jnp.maximum(r0 - 8, 0), 8), 8), :]
            below = u_ref[b, pl.ds(pl.multiple_of(jnp.minimum(r0 + CONV_ROWS, L - 8), 8), 8), :]
            first = jnp.where(r0 == 0, 0.0, above[7:8, :])
            last = jnp.where(r0 + CONV_ROWS == L, 0.0, below[0:1, :])
            for part in range(3):
                cols = slice(part * HY_WIDTH, (part + 1) * HY_WIDTH)
                u = u_ref[b, rows, cols]
                prev = jnp.where(t == 0, first[:, cols], pltpu.roll(u, 1, axis=0))
                nxt = jnp.where(t == CONV_ROWS - 1, last[:, cols], pltpu.roll(u, CONV_ROWS - 1, axis=0))
                y = (cb_ref[:, cols] + prev * cw_ref[0:1, cols] + u * cw_ref[1:2, cols]
                     + nxt * cw_ref[2:3, cols])
                if part < 2:
                    gate_ref[part, rows, :] = y
                else:
                    z_ref[rows, :] = y
            return carry

        lax.fori_loop(0, L // CONV_ROWS, short_conv, 0, unroll=(nb == 1))

        for o in range(2):
            for j in range(nb):
                zf_ref[j] = _dot(fwd_ref[...], z_ref[j * blk:(j + 1) * blk, :].astype(BF16))
            bias = hb_ref[o]

            def out_block(i, carry, o=o, bias=bias):
                def spec_rows(c, carry2):
                    r0 = pl.multiple_of(c * SPEC_ROWS, SPEC_ROWS)
                    re = pl.ds(r0, SPEC_ROWS)
                    im = pl.ds(blk + r0, SPEC_ROWS)
                    top = jnp.zeros((SPEC_ROWS, HY_WIDTH), F32)
                    bot = jnp.zeros((SPEC_ROWS, HY_WIDTH), F32)
                    for j in range(nb):
                        d = i - j + nb
                        gr, gi = g_ref[o, d, re, :], g_ref[o, d, im, :]
                        zr, zi = zf_ref[j, re, :], zf_ref[j, im, :]
                        top = top + (gr * zr - gi * zi)
                        bot = bot + (gr * zi + gi * zr)
                    y_ref[re, :] = top.astype(BF16)
                    y_ref[im, :] = bot.astype(BF16)
                    return carry2

                lax.fori_loop(0, blk // SPEC_ROWS, spec_rows, 0, unroll=(nb == 1))
                ny = jnp.zeros((1, HY_WIDTH), F32)
                for j in range(nb):
                    ny = ny + ny_ref[o, i - j + nb] * zf_ref[j, blk:blk + 1, :]
                head = y_ref[blk:blk + NY_ROWS, :].astype(F32)
                y_ref[blk:blk + NY_ROWS, :] = jnp.where(ny_row, ny, head).astype(BF16)
                conv = _dot(inv_ref[...], y_ref[...])
                rows = pl.ds(pl.multiple_of(i * blk, blk), blk)
                z_ref[rows, :] = gate_ref[o, rows, :] * (conv + bias * z_ref[rows, :])
                return carry

            lax.fori_loop(0, nb, out_block, 0, unroll=(nb == 1))
        o_ref[b] = z_ref[...].astype(o_ref.dtype)


def _hyena_mixer(l, u_hy, conv_w, conv_b, hy_bias, spec, nyq, per_step):
    B, L, _ = u_hy.shape
    nd = spec.shape[1]
    blk = _dft_block(L)
    n = 2 * blk
    fwd, inv, _, _ = _dft_consts(blk)
    return pl.pallas_call(
        _hyena_kernel,
        grid=(B // per_step,),
        in_specs=[
            pl.BlockSpec((per_step, L, 3 * HY_WIDTH), lambda b: (b, 0, 0)),
            _layer((3, 3 * HY_WIDTH), l),
            _layer((1, 3 * HY_WIDTH), l),
            _layer((2, 1, HY_WIDTH), l),
            _resident((2, nd, n, HY_WIDTH)),
            _resident((2, nd, 1, HY_WIDTH)),
            _resident((n, blk)),
            _resident((blk, n)),
        ],
        out_specs=pl.BlockSpec((per_step, L, HY_WIDTH), lambda b: (b, 0, 0)),
        out_shape=jax.ShapeDtypeStruct((B, L, HY_WIDTH), BF16),
        scratch_shapes=[
            pltpu.VMEM((L, HY_WIDTH), F32),
            pltpu.VMEM((2, L, HY_WIDTH), F32),
            pltpu.VMEM((L // blk, n, HY_WIDTH), F32),
            pltpu.VMEM((n, HY_WIDTH), BF16),
        ],
        compiler_params=_params("parallel"),
        name="hyena_mixer",
    )(u_hy, conv_w, conv_b, hy_bias, spec, nyq,
      jnp.asarray(fwd, F32).astype(BF16), jnp.asarray(inv, F32).astype(BF16))


HEAD_PAIR = 2 * NA_HEAD_DIM
SOFTMAX_ROWS = 64
PV_CHUNK = 256


def _paired_heads(q, keys, values, bias, s_ref, p_ref):
    M = q.shape[0]
    n_slots = s_ref.shape[0]
    lane = lax.broadcasted_iota(jnp.int32, (M, HEAD_PAIR), 1)

    def own_lanes(ln, e):
        return (ln < NA_HEAD_DIM) if e == 0 else (ln >= NA_HEAD_DIM)

    def scores(h):
        cols = slice((h // 2) * HEAD_PAIR, (h // 2 + 1) * HEAD_PAIR)
        qh = jnp.where(own_lanes(lane, h % 2), q[:, cols], jnp.zeros((M, HEAD_PAIR), BF16))
        off, m = 0, None
        for i, k in enumerate(keys):
            for c0 in range(0, k.shape[0], PV_CHUNK):
                s = _dot_nt(qh, k[c0:c0 + PV_CHUNK, cols])
                if bias is not None and i == 0:
                    s = s + bias(h, slice(c0, c0 + PV_CHUNK))
                s_ref[h % n_slots, :, off + c0:off + c0 + PV_CHUNK] = s
                mi = jnp.max(s, axis=-1, keepdims=True)
                m = mi if m is None else jnp.maximum(m, mi)
            off += k.shape[0]
        return m

    def attend(h, m):
        cols = slice((h // 2) * HEAD_PAIR, (h // 2 + 1) * HEAD_PAIR)
        acc, off = None, 0
        for v in values:
            vl = lax.broadcasted_iota(jnp.int32, (v.shape[0], HEAD_PAIR), 1)
            vh = jnp.where(own_lanes(vl, h % 2), v[:, cols], jnp.ones((v.shape[0], HEAD_PAIR), BF16))
            for c0 in range(0, v.shape[0], PV_CHUNK):
                kc = slice(off + c0, off + c0 + PV_CHUNK)
                for r0 in range(0, M, SOFTMAX_ROWS):
                    rows = slice(r0, r0 + SOFTMAX_ROWS)
                    p_ref[h % 2, rows, kc] = jnp.exp(s_ref[h % n_slots, rows, kc] - m[rows]).astype(BF16)
                part = _dot(p_ref[h % 2, :, kc], vh[c0:c0 + PV_CHUNK])
                acc = part if acc is None else acc + part
            off += v.shape[0]
        return acc / pltpu.roll(acc, NA_HEAD_DIM, axis=1)

    outs, halves = [], []
    ahead = n_slots - 1
    maxima = {h: scores(h) for h in range(ahead)}
    for h in range(NA_HEADS):
        if h + ahead < NA_HEADS:
            maxima[h + ahead] = scores(h + ahead)
        halves.append(attend(h, maxima.pop(h)))
        if h % 2 == 1:
            outs.append(jnp.where(lane < NA_HEAD_DIM, halves[0], halves[1]))
            halves = []
    return jnp.concatenate(outs, axis=-1)


def _ctx_attn_kernel(q_ref, k_ref, v_ref, o_ref, s_ref, p_ref):
    nseq, L = q_ref.shape[0], q_ref.shape[1]
    even = lax.broadcasted_iota(jnp.int32, (L, HEAD_PAIR), 1) < NA_HEAD_DIM
    zero = jnp.zeros((L, HEAD_PAIR), BF16)
    units = [(b, p) for b in range(nseq) for p in range(NA_HEADS // 2)]

    def scores(n):
        b, p = units[n]
        cols = slice(p * HEAD_PAIR, (p + 1) * HEAD_PAIR)
        qp = q_ref[b, :, cols]
        q2 = jnp.concatenate([jnp.where(even, qp, zero), jnp.where(even, zero, qp)], axis=0)
        s_ref[n % 2] = _dot_nt(q2, k_ref[b, 0, :, cols].astype(BF16))

    def attend(n):
        b, p = units[n]
        cols = slice(p * HEAD_PAIR, (p + 1) * HEAD_PAIR)
        for r0 in range(0, 2 * L, SOFTMAX_ROWS):
            rows = slice(r0, r0 + SOFTMAX_ROWS)
            s = s_ref[n % 2, rows, :]
            e = jnp.exp(s - jnp.max(s, axis=-1, keepdims=True))
            p_ref[n % 2, rows, :] = (e * (1.0 / jnp.sum(e, axis=-1, keepdims=True))).astype(BF16)
        acc = _dot(p_ref[n % 2], v_ref[b, 0, :, cols].astype(BF16))
        o_ref[b, :, cols] = jnp.where(even, acc[:L], acc[L:]).astype(o_ref.dtype)

    scores(0)
    for n in range(len(units)):
        if n + 1 < len(units):
            scores(n + 1)
        attend(n)


def _context_attention(q, k_cache, v_cache, layer, per_step):
    B, L, _ = q.shape
    spec = pl.BlockSpec((per_step, L, NA_WIDTH), lambda b: (b, 0, 0))
    kv = pl.BlockSpec((per_step, 1, L, NA_WIDTH), lambda b: (b, layer, 0, 0))
    return pl.pallas_call(
        _ctx_attn_kernel,
        grid=(B // per_step,),
        in_specs=[spec, kv, kv],
        out_specs=spec,
        out_shape=jax.ShapeDtypeStruct((B, L, NA_WIDTH), BF16),
        scratch_shapes=[pltpu.VMEM((2, 2 * L, L), F32), pltpu.VMEM((2, 2 * L, L), BF16)],
        compiler_params=_params("parallel"),
        name="context_attention",
    )(q, k_cache, v_cache)


N_DR = 2 * NA_KH - 1
N_DC = 2 * NA_KW - 1
NA_RB = 4
NA_WIN = NA_RB + NA_KH
WIN_KEYS = NA_WIN * GRID_W
NA_SCORE_SLOTS = 2
NA_BLOCKS_PER_STEP = 2


def _bias_kernel(rb_ref, o_ref):
    h = pl.program_id(0)
    qc = lax.broadcasted_iota(jnp.int32, (GRID_W, GRID_W), 0)
    kc = lax.broadcasted_iota(jnp.int32, (GRID_W, GRID_W), 1)
    dc = jnp.clip(kc - qc, -(NA_KW - 1), NA_KW - 1) + (NA_KW - 1)
    col_start = jnp.clip(qc - NA_KW // 2, 0, GRID_W - NA_KW)
    visible = (kc >= col_start) & (kc < col_start + NA_KW)
    blocks = []
    for dr in range(N_DR):
        acc = jnp.zeros((GRID_W, GRID_W), F32)
        for d in range(N_DC):
            acc = jnp.where(dc == d, rb_ref[h * N_DR + dr, d], acc)
        blocks.append(jnp.where(visible, acc, MASK_VALUE))
    hidden = jnp.full((GRID_W, GRID_W), MASK_VALUE, F32)
    for variant, (first, base) in enumerate(((0, NA_KH - 1), (None, NA_KH // 2 - 1), (NA_RB, -1))):
        for a in range(NA_RB):
            lo = a if first is None else first
            row = [blocks[i - a + base] if lo <= i < lo + NA_KH else hidden for i in range(NA_WIN)]
            o_ref[variant, 0, a * GRID_W:(a + 1) * GRID_W, :] = jnp.concatenate(row, axis=-1)


def _bias_table(rel_bias_l):
    return pl.pallas_call(
        _bias_kernel,
        grid=(NA_HEADS,),
        in_specs=[pl.BlockSpec(memory_space=pltpu.SMEM)],
        out_specs=pl.BlockSpec((3, 1, NA_RB * GRID_W, WIN_KEYS), lambda h: (0, h, 0, 0)),
        out_shape=jax.ShapeDtypeStruct((3, NA_HEADS, NA_RB * GRID_W, WIN_KEYS), F32),
        compiler_params=_params("parallel"),
        name="na_bias_table",
    )(rel_bias_l.reshape(NA_HEADS * N_DR, N_DC))


def _na_kernel(q_ref, k_ref, v_ref, kc_ref, vc_ref, *rest):
    tab_refs, (o_ref, s_ref, p_ref) = rest[:-3], rest[-3:]
    rows = k_ref.shape[1] // GRID_W
    kc = kc_ref[0, 0].astype(BF16)
    vc = vc_ref[0, 0].astype(BF16)
    for sub, tab_ref in enumerate(tab_refs):
        rb = pl.program_id(1) * len(tab_refs) + sub
        first_row = jnp.clip(rb * NA_RB - NA_KH // 2, 0, rows - NA_WIN)
        start = pl.multiple_of(first_row * GRID_W, NA_RB * GRID_W)
        keys = [k_ref[0, pl.ds(start, WIN_KEYS), :], kc]
        values = [v_ref[0, pl.ds(start, WIN_KEYS), :], vc]
        q_rows = slice(sub * NA_RB * GRID_W, (sub + 1) * NA_RB * GRID_W)
        out = _paired_heads(q_ref[0, q_rows, :], keys, values,
                            lambda h, cols, tab_ref=tab_ref: tab_ref[0, h, :, cols], s_ref, p_ref)
        o_ref[0, q_rows, :] = out.astype(o_ref.dtype)


def _neighbourhood_attention(q, k, v, cache_k, cache_v, layer, table):
    B, L, _ = q.shape
    rows = L // GRID_W
    assert rows % (NA_RB * NA_BLOCKS_PER_STEP) == 0 and rows >= NA_WIN + NA_RB
    nrb = rows // NA_RB
    Lc = cache_k.shape[2]
    seq = pl.BlockSpec((1, L, NA_WIDTH), lambda b, r: (b, 0, 0))
    blk = pl.BlockSpec((1, NA_BLOCKS_PER_STEP * NA_RB * GRID_W, NA_WIDTH), lambda b, r: (b, r, 0))
    ctx = pl.BlockSpec((1, 1, Lc, NA_WIDTH), lambda b, r: (b, layer, 0, 0))
    placement = lambda rb: jnp.where(rb == 0, 0, jnp.where(rb == nrb - 1, 2, 1))
    tabs = [pl.BlockSpec((1, NA_HEADS, NA_RB * GRID_W, WIN_KEYS),
                         lambda b, r, sub=sub: (placement(r * NA_BLOCKS_PER_STEP + sub), 0, 0, 0))
            for sub in range(NA_BLOCKS_PER_STEP)]
    return pl.pallas_call(
        _na_kernel,
        grid=(B, nrb // NA_BLOCKS_PER_STEP),
        in_specs=[blk, seq, seq, ctx, ctx] + tabs,
        out_specs=blk,
        out_shape=jax.ShapeDtypeStruct((B, L, NA_WIDTH), BF16),
        scratch_shapes=[pltpu.VMEM((NA_SCORE_SLOTS, NA_RB * GRID_W, WIN_KEYS + Lc), F32),
                        pltpu.VMEM((2, NA_RB * GRID_W, WIN_KEYS + Lc), BF16)],
        compiler_params=_params("parallel", "arbitrary"),
        name="neighbourhood_attention",
    )(q, k, v, cache_k, cache_v, *([table] * NA_BLOCKS_PER_STEP))


FF_CHUNK = 1024


def _out_mlp_kernel(x_ref, yp_ref, yh_ref, ya_ref, mod_ref, g_ref, wo_ref, wu_ref, wd_ref, o_ref):
    y = (_dot(yp_ref[0], wo_ref[0:POOL_WIDTH, :])
         + _dot(yh_ref[0], wo_ref[POOL_WIDTH:POOL_WIDTH + HY_WIDTH, :])
         + _dot(ya_ref[0], wo_ref[POOL_WIDTH + HY_WIDTH:D_MODEL, :]))
    x = x_ref[0] + mod_ref[0, 2:3, :] * y
    h = _modulated_norm(x, g_ref[...], mod_ref[0, 3:4, :], mod_ref[0, 4:5, :]).astype(BF16)
    acc = jnp.zeros(x.shape, F32)
    for c in range(D_FF // FF_CHUNK):
        cols = slice(c * FF_CHUNK, (c + 1) * FF_CHUNK)
        a = jnp.maximum(_dot(h, wu_ref[:, cols]), 0.0)
        acc = acc + _dot((a * a).astype(BF16), wd_ref[cols, :])
    o_ref[0] = x + mod_ref[0, 5:6, :] * acc


def _out_mlp(l, x, y_pool, y_hy, y_na, mod, mod_row0, g2, w_out, w_up, w_down):
    B, L, _ = x.shape
    tm = TOKEN_TILE
    assert L % tm == 0
    tok = lambda w: pl.BlockSpec((1, tm, w), lambda b, i: (b, i, 0))
    return pl.pallas_call(
        _out_mlp_kernel,
        grid=(B, L // tm),
        in_specs=[
            tok(D_MODEL), tok(POOL_WIDTH), tok(HY_WIDTH), tok(NA_WIDTH),
            _mod_spec(l, mod_row0),
            _layer((1, D_MODEL), l),
            _layer((D_MODEL, D_MODEL), l),
            _layer((D_MODEL, D_FF), l),
            _layer((D_FF, D_MODEL), l),
        ],
        out_specs=tok(D_MODEL),
        out_shape=jax.ShapeDtypeStruct((B, L, D_MODEL), F32),
        compiler_params=_params("parallel", "parallel"),
        name="out_mlp",
    )(x, y_pool, y_hy, y_na, mod, g2, w_out, w_up, w_down)


def _block_diag(blocks):
    *lead, g, c, d = blocks.shape
    eye = jnp.eye(g, dtype=blocks.dtype)
    out = blocks[..., :, :, None, :] * eye[:, None, :, None]
    return out.reshape(*lead, g * c, g * d)


def kernel(x_prompt, x_sample, cache_k, cache_v, c, c_ctx, norm1_g, norm2_g, w_mod, b_mod, w_in,
           pool_w, pool_scale, hy_conv_w, hy_conv_b, hy_f1_w, hy_f1_b, hy_f1_freq, hy_f2_w, hy_f2_b,
           hy_f2_freq, hy_f3_w, hy_bias, q_norm_g, k_norm_g, rel_bias, w_out, w_up, w_down):
    n_prompt, seq, _ = x_prompt.shape
    n_dec, dec_seq, _ = x_sample.shape

    conds = jnp.zeros((MOD_ROWS, D_MODEL), F32).at[0].set(c_ctx).at[1:1 + n_dec].set(c)
    mod = _adaln(conds, w_mod, b_mod).reshape(DEPTH, MOD_ROWS, N_MOD, D_MODEL)

    cache_k = cache_k.reshape(n_dec, DEPTH, -1, NA_WIDTH)
    cache_v = cache_v.reshape(n_dec, DEPTH, -1, NA_WIDTH)

    g1 = norm1_g.reshape(DEPTH, 1, D_MODEL)
    g2 = norm2_g.reshape(DEPTH, 1, D_MODEL)
    w_in_b, w_out_b, w_up_b, w_down_b = (w.astype(BF16) for w in (w_in, w_out, w_up, w_down))
    pool_bd = _block_diag(pool_w).astype(BF16)
    pool_s = pool_scale.reshape(DEPTH, 1, POOL_WIDTH)
    qg = jnp.tile(q_norm_g, (1, NA_HEADS)).reshape(DEPTH, 1, NA_WIDTH)
    kg = jnp.tile(k_norm_g, (1, NA_HEADS)).reshape(DEPTH, 1, NA_WIDTH)
    col = lambda a: a.reshape(DEPTH, HY_FFN, 1)
    f1_w = jnp.zeros((DEPTH, HY_FFN, HY_EMB_PAD), F32).at[:, :, :HY_EMB].set(hy_f1_w.transpose(0, 2, 1))
    filt_args = (f1_w, col(hy_f1_b), col(hy_f1_freq), hy_f2_w.transpose(0, 2, 1), col(hy_f2_b),
                 col(hy_f2_freq), hy_f3_w)
    conv_b = hy_conv_b.reshape(DEPTH, 1, 3 * HY_WIDTH)
    hy_b = hy_bias.reshape(DEPTH, 2, 1, HY_WIDTH)

    xp = x_prompt.reshape(1, n_prompt * seq, D_MODEL)
    xs = x_sample
    new_k = new_v = None
    per_seq = lambda a: a.reshape(n_prompt, seq, a.shape[-1])
    flat = lambda a: a.reshape(1, n_prompt * seq, a.shape[-1])
    for l in range(DEPTH):
        u_pool, u_hy, q, new_k, new_v = _in_proj(l, xp, mod, 0, g1, w_in_b, qg, kg,
                                                 cache=(seq, new_k, new_v))
        spec, nyq = _hyena_spectra(*_hyena_filters(l, seq, *filt_args))
        y_pool = _pool_mixer(l, per_seq(u_pool), pool_bd, pool_s, CTX_PER_STEP)
        y_hy = _hyena_mixer(l, per_seq(u_hy), hy_conv_w, conv_b, hy_b, spec, nyq, CTX_PER_STEP)
        y_na = _context_attention(per_seq(q), new_k, new_v, l, CTX_PER_STEP)
        xp = _out_mlp(l, xp, flat(y_pool), flat(y_hy), flat(y_na), mod, 0, g2, w_out_b, w_up_b, w_down_b)

        u_pool, u_hy, q, k, v = _in_proj(l, xs, mod, 1, g1, w_in_b, qg, kg)
        spec, nyq = _hyena_spectra(*_hyena_filters(l, dec_seq, *filt_args))
        y_pool = _pool_mixer(l, u_pool, pool_bd, pool_s, 1)
        y_hy = _hyena_mixer(l, u_hy, hy_conv_w, conv_b, hy_b, spec, nyq, 1)
        y_na = _neighbourhood_attention(q, k, v, cache_k, cache_v, l, _bias_table(rel_bias[l]))
        xs = _out_mlp(l, xs, y_pool, y_hy, y_na, mod, 1, g2, w_out_b, w_up_b, w_down_b)

    shape_kv = (n_prompt, DEPTH, seq, NA_HEADS, NA_HEAD_DIM)
    return (xp.reshape(n_prompt, seq, D_MODEL), xs, new_k.reshape(shape_kv), new_v.reshape(shape_kv))
```
